```python
import math
import jax
import jax.numpy as jnp
from jax import lax
import numpy as np

D_MODEL = 2048
BATCH = 2
SEQ = 8192
DEPTH = 4

DN_HEADS = 6
DN_DK = 128
DN_DV = 128
DN_CHUNK = 64
CONV_W = 5
ATT_HEADS = 6
ATT_KV_HEADS = 2
ATT_DH = 128
ROPE_SUB = 64
ROPE_THETA = 10000.0
Q_BLOCK = 128
GRID_W = 64
GLA_HEADS = 4
GLA_DK = 64
GLA_DV = 128
GLA_RANK = 16
GLA_NORMALIZER = 16.0
GLA_CHUNK = 64
DN_QK = DN_HEADS * DN_DK
DN_W = DN_HEADS * DN_DV
ATT_W = ATT_HEADS * ATT_DH
ATT_KV_W = ATT_KV_HEADS * ATT_DH
GLA_QK = GLA_HEADS * GLA_DK
GLA_W = GLA_HEADS * GLA_DV
MIX_W = DN_W + ATT_W + GLA_W
IN_SIZES = (DN_QK, DN_QK, DN_W, DN_W, DN_HEADS, DN_HEADS, DN_HEADS, DN_HEADS,
            ATT_W, ATT_KV_W, ATT_KV_W,
            GLA_QK, GLA_QK, GLA_W, GLA_W, GLA_RANK, GLA_RANK)
IN_COLS = sum(IN_SIZES)
DN_CONV_C = 2 * DN_QK + DN_W
D_FF = 7168
N_EXPERTS = 8
TOP_K = 2
MOE_BLOCK = 256
N_DENSE = (DEPTH + 1) // 2
N_MOE = DEPTH // 2
ALPHA = (2.0 * DEPTH) ** 0.25
BETA_INIT = (8.0 * DEPTH) ** -0.25
EPS = 1e-6

kernel_name = 'hybrid_deltanet_gqa_gla_moe_encoder'


def rmsnorm(x, g):
    xf = x.astype(jnp.float32)
    y = xf * lax.rsqrt(jnp.mean(xf * xf, axis=-1, keepdims=True) + EPS)
    return (y * g.astype(jnp.float32)).astype(x.dtype)


def layernorm(x, g, b):
    xf = x.astype(jnp.float32)
    mu = jnp.mean(xf, axis=-1, keepdims=True)
    xc = xf - mu
    var = jnp.mean(xc * xc, axis=-1, keepdims=True)
    return (xc * lax.rsqrt(var + EPS) * g.astype(jnp.float32) + b.astype(jnp.float32)).astype(x.dtype)


def l2norm(x):
    xf = x.astype(jnp.float32)
    return (xf * lax.rsqrt(jnp.sum(xf * xf, axis=-1, keepdims=True) + EPS)).astype(x.dtype)


def to_heads(t, n_heads):
    b, s, _ = t.shape
    return t.reshape(b, s, n_heads, -1).transpose(0, 2, 1, 3)


def centred_short_conv(x, w):
    c = x.shape[-1]
    y = lax.conv_general_dilated(x, w[:, None, :].astype(x.dtype), window_strides=(1,),
                                 padding=[(CONV_W // 2, CONV_W // 2)],
                                 dimension_numbers=('NWC', 'WIO', 'NWC'),
                                 feature_group_count=c)
    return jax.nn.silu(y)


def chunk_first(t):
    return jnp.moveaxis(t, 2, 0)


def gated_delta_rule(q, k, v, g, beta):
    out_dtype = v.dtype
    f32 = jnp.float32
    z, h, s, dk = q.shape
    dv = v.shape[-1]
    c = DN_CHUNK
    n = s // c
    q = (q.astype(f32) * dk ** -0.5).reshape(z, h, n, c, dk)
    k = k.astype(f32).reshape(z, h, n, c, dk)
    v = v.astype(f32).reshape(z, h, n, c, dv)
    beta = beta.astype(f32).reshape(z, h, n, c)
    gc = jnp.cumsum(g.astype(f32).reshape(z, h, n, c), axis=-1)
    idx = jnp.arange(c)
    incl = idx[:, None] >= idx[None, :]
    strict = idx[:, None] > idx[None, :]
    decay = jnp.exp(jnp.where(incl, gc[..., :, None] - gc[..., None, :], -jnp.inf))
    kb = k * beta[..., None]
    m = jnp.where(strict, jnp.einsum('zhnid,zhnjd->zhnij', kb, k) * decay, 0.0)
    eye = jnp.eye(c, dtype=f32)
    t_inv = lax.linalg.triangular_solve(m + eye, jnp.broadcast_to(eye, m.shape), left_side=True,
                                        lower=True, unit_diagonal=True)
    u = t_inv @ (v * beta[..., None])
    w = t_inv @ (kb * jnp.exp(gc)[..., None])
    qk = jnp.einsum('zhnid,zhnjd->zhnij', q, k) * decay
    q_dec = q * jnp.exp(gc)[..., None]
    g_last = gc[..., -1]
    k_tail = k * jnp.exp(g_last[..., None] - gc)[..., None]

    def step(state, xs):
        u_n, w_n, qk_n, qd_n, kt_n, gl_n = xs
        v_new = u_n - w_n @ state
        o_n = qd_n @ state + qk_n @ v_new
        state = state * jnp.exp(gl_n)[..., None, None] + jnp.swapaxes(kt_n, -1, -2) @ v_new
        return state, o_n

    state0 = jnp.zeros((z, h, dk, dv), f32)
    xs = (chunk_first(u), chunk_first(w), chunk_first(qk), chunk_first(q_dec),
          chunk_first(k_tail), chunk_first(g_last))
    _, o = lax.scan(step, state0, xs)
    return jnp.moveaxis(o, 0, 2).reshape(z, h, s, dv).astype(out_dtype)


def gla_rule(q, k, v, gk):
    out_dtype = v.dtype
    f32 = jnp.float32
    z, h, s, dk = q.shape
    dv = v.shape[-1]
    c = GLA_CHUNK
    n = s // c
    q = (q.astype(f32) * dk ** -0.5).reshape(z, h, n, c, dk)
    k = k.astype(f32).reshape(z, h, n, c, dk)
    v = v.astype(f32).reshape(z, h, n, c, dv)
    gc = jnp.cumsum(gk.astype(f32).reshape(z, h, n, c, dk), axis=-2)
    q_dec = q * jnp.exp(gc)
    g_last = gc[..., -1, :]
    k_tail = k * jnp.exp(g_last[..., None, :] - gc)
    idx = jnp.arange(c)
    incl = (idx[:, None] >= idx[None, :])[..., None]

    def step(state, xs):
        q_n, k_n, v_n, g_n, qd_n, kt_n, gl_n = xs
        rel = jnp.exp(jnp.where(incl, g_n[..., :, None, :] - g_n[..., None, :, :], -jnp.inf))
        a = jnp.sum(q_n[..., :, None, :] * k_n[..., None, :, :] * rel, axis=-1)
        o_n = qd_n @ state + a @ v_n
        state = state * jnp.exp(gl_n)[..., :, None] + jnp.swapaxes(kt_n, -1, -2) @ v_n
        return state, o_n

    state0 = jnp.zeros((z, h, dk, dv), f32)
    xs = (chunk_first(q), chunk_first(k), chunk_first(v), chunk_first(gc),
          chunk_first(q_dec), chunk_first(k_tail), chunk_first(g_last))
    _, o = lax.scan(step, state0, xs)
    return jnp.moveaxis(o, 0, 2).reshape(z, h, s, dv).astype(out_dtype)


def axial_rope_tables(s):
    rows = s // GRID_W
    row = jnp.repeat(jnp.arange(rows, dtype=jnp.int32), GRID_W).astype(jnp.float32)
    col = jnp.tile(jnp.arange(GRID_W, dtype=jnp.int32), rows).astype(jnp.float32)
    inv_freq = ROPE_THETA ** (-jnp.arange(0, ROPE_SUB, 2, dtype=jnp.float32) / ROPE_SUB)
    ang_r = row[:, None] * inv_freq[None, :]
    ang_c = col[:, None] * inv_freq[None, :]
    return (jnp.cos(ang_r), jnp.sin(ang_r), jnp.cos(ang_c), jnp.sin(ang_c))


def rotate_sub(x, cos, sin):
    half = x.shape[-1] // 2
    x1, x2 = x[..., :half], x[..., half:]
    cos = cos.astype(x.dtype)
    sin = sin.astype(x.dtype)
    return jnp.concatenate([x1 * cos - x2 * sin, x2 * cos + x1 * sin], axis=-1)


def apply_axial_rope(x, tabs):
    cos_r, sin_r, cos_c, sin_c = tabs
    return jnp.concatenate([rotate_sub(x[..., :ROPE_SUB], cos_r, sin_r),
                            rotate_sub(x[..., ROPE_SUB:], cos_c, sin_c)], axis=-1)


def gqa_bidirectional(q, k, v):
    b, hq, s, dh = q.shape
    grp = hq // ATT_KV_HEADS
    nb = s // Q_BLOCK
    qb = jnp.moveaxis(q.reshape(b, ATT_KV_HEADS, grp, nb, Q_BLOCK, dh), 3, 0)
    scale = dh ** -0.5

    def block(qi):
        sc = jnp.einsum('bkgqd,bksd->bkgqs', qi, k).astype(jnp.float32) * scale
        p = jax.nn.softmax(sc, axis=-1).astype(v.dtype)
        return jnp.einsum('bkgqs,bksd->bkgqd', p, v)

    o = lax.map(block, qb)
    return jnp.moveaxis(o, 0, 3).reshape(b, hq, s, dh)


def hybrid_mixer(x, w_in, dn_conv, dn_a_log, dn_dt_bias, dn_norm_g, att_qn_g, att_kn_g,
                 gla_up, gla_up_b, gla_norm_g, w_out, rope):
    b, s, _ = x.shape
    proj = x @ w_in
    (dq, dk, dv, dgate, a_f, a_b, b_f, b_b, aq, ak, av,
     gq, gkk, gv, ggate, lr_f, lr_b) = jnp.split(proj, np.cumsum(IN_SIZES)[:-1].tolist(), axis=-1)

    def flip(t):
        return jnp.flip(t, axis=2)

    qkv = centred_short_conv(jnp.concatenate([dq, dk, dv], axis=-1), dn_conv)
    dq, dk, dv = jnp.split(qkv, [DN_QK, 2 * DN_QK], axis=-1)
    q = l2norm(to_heads(dq, DN_HEADS))
    k = l2norm(to_heads(dk, DN_HEADS))
    v = to_heads(dv, DN_HEADS)
    a = jnp.stack([a_f, a_b], axis=0)
    g = -jnp.exp(dn_a_log)[:, None, None, :] * jax.nn.softplus(a + dn_dt_bias[:, None, None, :])
    beta = jax.nn.sigmoid(jnp.stack([b_f, b_b], axis=0))
    g = g.transpose(0, 1, 3, 2)
    beta = beta.transpose(0, 1, 3, 2)
    o2 = gated_delta_rule(jnp.concatenate([q, flip(q)], axis=0),
                          jnp.concatenate([k, flip(k)], axis=0),
                          jnp.concatenate([v, flip(v)], axis=0),
                          jnp.concatenate([g[0], flip(g[1])], axis=0),
                          jnp.concatenate([beta[0], flip(beta[1])], axis=0))
    o_dn = (o2[:b] + flip(o2[b:])).transpose(0, 2, 1, 3)
    o_dn = rmsnorm(o_dn, dn_norm_g) * jax.nn.silu(dgate.reshape(b, s, DN_HEADS, DN_DV))
    o_dn = o_dn.reshape(b, s, DN_W)

    q = apply_axial_rope(rmsnorm(to_heads(aq, ATT_HEADS), att_qn_g), rope)
    k = apply_axial_rope(rmsnorm(to_heads(ak, ATT_KV_HEADS), att_kn_g), rope)
    v = to_heads(av, ATT_KV_HEADS)
    o_att = gqa_bidirectional(q, k, v).transpose(0, 2, 1, 3).reshape(b, s, ATT_W)

    q = to_heads(gq, GLA_HEADS)
    k = to_heads(gkk, GLA_HEADS)
    v = to_heads(gv, GLA_HEADS)
    lr = jnp.stack([lr_f, lr_b], axis=0)
    gk = jax.nn.log_sigmoid(jnp.einsum('zbsr,zrc->zbsc', lr, gla_up) + gla_up_b[:, None, None, :]) / GLA_NORMALIZER
    gk = gk.reshape(2, b, s, GLA_HEADS, GLA_DK).transpose(0, 1, 3, 2, 4)
    o2 = gla_rule(jnp.concatenate([q, flip(q)], axis=0),
                  jnp.concatenate([k, flip(k)], axis=0),
                  jnp.concatenate([v, flip(v)], axis=0),
                  jnp.concatenate([gk[0], flip(gk[1])], axis=0))
    o_gla = (o2[:b] + flip(o2[b:])).transpose(0, 2, 1, 3)
    o_gla = rmsnorm(o_gla, gla_norm_g) * jax.nn.silu(ggate.reshape(b, s, GLA_HEADS, GLA_DV))
    o_gla = o_gla.reshape(b, s, GLA_W)

    return jnp.concatenate([o_dn, o_att, o_gla], axis=-1) @ w_out


def swiglu(x, w_gate, w_up, w_down):
    return (jax.nn.silu(x @ w_gate) * (x @ w_up)) @ w_down


def moe_swiglu(x, router_w, w_gate, w_up, w_down):
    b, s, d = x.shape
    t = b * s
    xt = x.reshape(t, d)
    logits = (xt @ router_w).astype(jnp.float32)
    top_val, top_idx = lax.top_k(logits, TOP_K)
    gates = jax.nn.softmax(top_val, axis=-1)
    e_flat = top_idx.reshape(-1).astype(jnp.int32)
    tok_flat = jnp.repeat(jnp.arange(t, dtype=jnp.int32), TOP_K)
    gate_flat = gates.reshape(-1)
    order = jnp.argsort(e_flat)
    e_sorted = e_flat[order]
    tok_sorted = tok_flat[order]
    gate_sorted = gate_flat[order]
    counts = jax.ops.segment_sum(jnp.ones_like(e_flat), e_flat, num_segments=N_EXPERTS)
    padded = (counts + MOE_BLOCK - 1) // MOE_BLOCK * MOE_BLOCK
    start = jnp.cumsum(counts) - counts
    pstart = jnp.cumsum(padded) - padded
    pend = pstart + padded
    dest = pstart[e_sorted] + jnp.arange(TOP_K * t, dtype=jnp.int32) - start[e_sorted]
    nb = -(-(TOP_K * t) // MOE_BLOCK) + N_EXPERTS
    cap = nb * MOE_BLOCK
    buf_tok = jnp.zeros((cap,), jnp.int32).at[dest].set(tok_sorted)
    buf_gate = jnp.zeros((cap,), gates.dtype).at[dest].set(gate_sorted)
    block_start = jnp.arange(nb, dtype=jnp.int32) * MOE_BLOCK
    block_exp = jnp.minimum(jnp.sum(block_start[:, None] >= pend[None, :], axis=-1), N_EXPERTS - 1)
    xb = xt[buf_tok].reshape(nb, MOE_BLOCK, d)

    def expert_block(args):
        xi, e = args
        return swiglu(xi, w_gate[e], w_up[e], w_down[e])

    yb = lax.map(expert_block, (xb, block_exp)).reshape(cap, d)
    y = jax.ops.segment_sum(yb * buf_gate[:, None].astype(yb.dtype), buf_tok, num_segments=t)
    return y.reshape(b, s, d)


def setup_inputs(seed: int = 0) -> dict:
    key = jax.random.key(seed)
    ks = jax.random.split(key, 24)
    f32 = jnp.float32

    def nrm(k, shape, scale):
        return jax.random.normal(k, shape, f32) * scale

    x = nrm(ks[0], (BATCH, SEQ, D_MODEL), 1.0)
    w_in = nrm(ks[1], (DEPTH, D_MODEL, IN_COLS), D_MODEL ** -0.5)
    dn_conv = nrm(ks[2], (DEPTH, CONV_W, DN_CONV_C), CONV_W ** -0.5)
    dn_a_log = jnp.log(jax.random.uniform(ks[3], (DEPTH, 2, DN_HEADS), f32, minval=1.0, maxval=16.0))
    dt = jnp.exp(jax.random.uniform(ks[4], (DEPTH, 2, DN_HEADS), f32,
                                    minval=math.log(1e-3), maxval=math.log(1e-1)))
    dn_dt_bias = dt + jnp.log(-jnp.expm1(-dt))
    dn_norm_g = 1.0 + nrm(ks[5], (DEPTH, DN_DV), 0.02)
    att_qn_g = 1.0 + nrm(ks[6], (DEPTH, ATT_DH), 0.02)
    att_kn_g = 1.0 + nrm(ks[7], (DEPTH, ATT_DH), 0.02)
    gla_up = nrm(ks[8], (DEPTH, 2, GLA_RANK, GLA_QK), GLA_RANK ** -0.5)
    gla_up_b = nrm(ks[9], (DEPTH, 2, GLA_QK), 0.02)
    gla_norm_g = 1.0 + nrm(ks[10], (DEPTH, GLA_DV), 0.02)
    w_out = nrm(ks[11], (DEPTH, MIX_W, D_MODEL), BETA_INIT * MIX_W ** -0.5)
    ln1_g = 1.0 + nrm(ks[12], (DEPTH, D_MODEL), 0.02)
    ln1_b = nrm(ks[13], (DEPTH, D_MODEL), 0.02)
    ln2_g = 1.0 + nrm(ks[14], (DEPTH, D_MODEL), 0.02)
    ln2_b = nrm(ks[15], (DEPTH, D_MODEL), 0.02)
    ffn_w_gate = nrm(ks[16], (N_DENSE, D_MODEL, D_FF), D_MODEL ** -0.5)
    ffn_w_up = nrm(ks[17], (N_DENSE, D_MODEL, D_FF), D_MODEL ** -0.5)
    ffn_w_down = nrm(ks[18], (N_DENSE, D_FF, D_MODEL), BETA_INIT * D_FF ** -0.5)
    router_w = nrm(ks[19], (N_MOE, D_MODEL, N_EXPERTS), D_MODEL ** -0.5)
    exp_w_gate = nrm(ks[20], (N_MOE, N_EXPERTS, D_MODEL, D_FF), D_MODEL ** -0.5)
    exp_w_up = nrm(ks[21], (N_MOE, N_EXPERTS, D_MODEL, D_FF), D_MODEL ** -0.5)
    exp_w_down = nrm(ks[22], (N_MOE, N_EXPERTS, D_FF, D_MODEL), BETA_INIT * D_FF ** -0.5)
    return {'x': x, 'w_in': w_in, 'dn_conv': dn_conv, 'dn_a_log': dn_a_log, 'dn_dt_bias': dn_dt_bias,
            'dn_norm_g': dn_norm_g, 'att_qn_g': att_qn_g, 'att_kn_g': att_kn_g, 'gla_up': gla_up,
            'gla_up_b': gla_up_b, 'gla_norm_g': gla_norm_g, 'w_out': w_out, 'ln1_g': ln1_g,
            'ln1_b': ln1_b, 'ln2_g': ln2_g, 'ln2_b': ln2_b, 'ffn_w_gate': ffn_w_gate,
            'ffn_w_up': ffn_w_up, 'ffn_w_down': ffn_w_down, 'router_w': router_w,
            'exp_w_gate': exp_w_gate, 'exp_w_up': exp_w_up, 'exp_w_down': exp_w_down}


def reference(x, w_in, dn_conv, dn_a_log, dn_dt_bias, dn_norm_g, att_qn_g, att_kn_g, gla_up,
              gla_up_b, gla_norm_g, w_out, ln1_g, ln1_b, ln2_g, ln2_b, ffn_w_gate, ffn_w_up,
              ffn_w_down, router_w, exp_w_gate, exp_w_up, exp_w_down):
    rope = axial_rope_tables(x.shape[1])
    for layer in range(DEPTH):
        mix = hybrid_mixer(x, w_in[layer], dn_conv[layer], dn_a_log[layer], dn_dt_bias[layer],
                           dn_norm_g[layer], att_qn_g[layer], att_kn_g[layer], gla_up[layer],
                           gla_up_b[layer], gla_norm_g[layer], w_out[layer], rope)
        x = layernorm(ALPHA * x + mix, ln1_g[layer], ln1_b[layer])
        j = layer // 2
        if layer % 2 == 0:
            ffn = swiglu(x, ffn_w_gate[j], ffn_w_up[j], ffn_w_down[j])
        else:
            ffn = moe_swiglu(x, router_w[j], exp_w_gate[j], exp_w_up[j], exp_w_down[j])
        x = layernorm(ALPHA * x + ffn, ln2_g[layer], ln2_b[layer])
    return x
```

```python
import functools
import math

import jax
import jax.numpy as jnp
from jax import lax
from jax.experimental import pallas as pl
from jax.experimental.pallas import tpu as pltpu

F32 = jnp.float32
BF16 = jnp.bfloat16
HIGHEST = lax.Precision.HIGHEST

DN_HEADS, DN_D = 6, 128
ATT_HEADS, ATT_KV_HEADS, ATT_DH = 6, 2, 128
ATT_GROUP = ATT_HEADS // ATT_KV_HEADS
ROPE_SUB, ROPE_THETA, GRID_W = 64, 10000.0, 64
GLA_HEADS, GLA_DK, GLA_DV, GLA_RANK = 4, 64, 128, 16
GLA_NORMALIZER = 16.0
CHUNK = 64
SUB = 16
CONV_W = 5
N_EXPERTS, TOP_K, MOE_BLOCK = 8, 2, 256
EPS = 1e-6
LANES = 128
HALO = 8

DN_W = DN_HEADS * DN_D
ATT_W = ATT_HEADS * ATT_DH
ATT_KV_W = ATT_KV_HEADS * ATT_DH
GLA_QK = GLA_HEADS * GLA_DK
GLA_W = GLA_HEADS * GLA_DV
IN_SIZES = (DN_W, DN_W, DN_W, DN_W, DN_HEADS, DN_HEADS, DN_HEADS, DN_HEADS,
            ATT_W, ATT_KV_W, ATT_KV_W, GLA_QK, GLA_QK, GLA_W, GLA_W, GLA_RANK, GLA_RANK)
OFF_DQ, OFF_DGATE, OFF_AQ, OFF_AK, OFF_AV = 0, 2304, 3072, 3840, 4096
OFF_GQ, OFF_GV, OFF_GGATE, OFF_GK, OFF_SMALL = 4352, 4608, 5120, 5632, 5888
PROJ_COLS = 6144
SM_AF, SM_AB, SM_BF, SM_BB, SM_LRF, SM_LRB = 0, 6, 12, 18, 24, 40

VMEM_LIMIT = 56 * 1024 * 1024


def _cparams(sem, vmem=VMEM_LIMIT):
    return pltpu.CompilerParams(dimension_semantics=sem, vmem_limit_bytes=vmem)


def _tile(n, pref, quantum=LANES):
    if n <= pref:
        return n
    t = pref - pref % quantum
    while n % t:
        t -= quantum
    return t


def _gmm_body(be_ref, x_ref, w_ref, o_ref):
    o_ref[...] = jnp.dot(x_ref[...], w_ref[...], preferred_element_type=F32).astype(o_ref.dtype)


def grouped_matmul(x, w, block_group, bm, tn, out_dtype):
    m, k = x.shape
    n = w.shape[-1]
    tn = _tile(n, tn)
    return pl.pallas_call(
        _gmm_body,
        out_shape=jax.ShapeDtypeStruct((m, n), out_dtype),
        grid_spec=pltpu.PrefetchScalarGridSpec(
            num_scalar_prefetch=1, grid=(n // tn, m // bm),
            in_specs=[pl.BlockSpec((bm, k), lambda j, i, be: (i, 0)),
                      pl.BlockSpec((None, k, tn), lambda j, i, be: (be[i], 0, j))],
            out_specs=pl.BlockSpec((bm, tn), lambda j, i, be: (i, j))),
        compiler_params=_cparams(("arbitrary", "arbitrary")),
        name="grouped_matmul",
    )(block_group, x, w)


def _gswiglu_body(be_ref, x_ref, wg_ref, wu_ref, o_ref):
    x = x_ref[...]
    g = jnp.dot(x, wg_ref[...], preferred_element_type=F32)
    u = jnp.dot(x, wu_ref[...], preferred_element_type=F32)
    o_ref[...] = (g * jax.nn.sigmoid(g) * u).astype(o_ref.dtype)


def grouped_swiglu(x, wg, wu, block_group, bm, tn):
    m, k = x.shape
    n = wg.shape[-1]
    tn = _tile(n, tn)
    return pl.pallas_call(
        _gswiglu_body,
        out_shape=jax.ShapeDtypeStruct((m, n), BF16),
        grid_spec=pltpu.PrefetchScalarGridSpec(
            num_scalar_prefetch=1, grid=(n // tn, m // bm),
            in_specs=[pl.BlockSpec((bm, k), lambda j, i, be: (i, 0)),
                      pl.BlockSpec((None, k, tn), lambda j, i, be: (be[i], 0, j)),
                      pl.BlockSpec((None, k, tn), lambda j, i, be: (be[i], 0, j))],
            out_specs=pl.BlockSpec((bm, tn), lambda j, i, be: (i, j))),
        compiler_params=_cparams(("arbitrary", "arbitrary")),
        name="grouped_swiglu",
    )(block_group, x, wg, wu)


def _res_ln(res, y, g, b, alpha):
    z = alpha * res + y
    mu = jnp.mean(z, axis=-1, keepdims=True)
    zc = z - mu
    var = jnp.mean(zc * zc, axis=-1, keepdims=True)
    return zc * lax.rsqrt(var + EPS) * g + b


def _mm_res_ln_body(x_ref, w_ref, res_ref, g_ref, b_ref, o32_ref, o16_ref, acc_ref, *, nk, alpha):
    kk = pl.program_id(1)

    @pl.when(kk == 0)
    def _():
        acc_ref[...] = jnp.zeros_like(acc_ref)

    acc_ref[...] += jnp.dot(x_ref[...], w_ref[...], preferred_element_type=F32)

    @pl.when(kk == nk - 1)
    def _():
        out = _res_ln(res_ref[...], acc_ref[...], g_ref[...], b_ref[...], alpha)
        o32_ref[...] = out
        o16_ref[...] = out.astype(BF16)


def matmul_res_ln(x, w, res, g, b, alpha, tm=512, tk=512):
    m, k = x.shape
    n = w.shape[-1]
    tm = _tile(m, tm, 8)
    tk = _tile(k, tk)
    nk = k // tk
    return pl.pallas_call(
        functools.partial(_mm_res_ln_body, nk=nk, alpha=alpha),
        out_shape=(jax.ShapeDtypeStruct((m, n), F32), jax.ShapeDtypeStruct((m, n), BF16)),
        grid=(m // tm, nk),
        in_specs=[pl.BlockSpec((tm, tk), lambda i, kk: (i, kk)),
                  pl.BlockSpec((tk, n), lambda i, kk: (kk, 0)),
                  pl.BlockSpec((tm, n), lambda i, kk: (i, 0)),
                  pl.BlockSpec((1, n), lambda i, kk: (0, 0)),
                  pl.BlockSpec((1, n), lambda i, kk: (0, 0))],
        out_specs=(pl.BlockSpec((tm, n), lambda i, kk: (i, 0)),
                   pl.BlockSpec((tm, n), lambda i, kk: (i, 0))),
        scratch_shapes=[pltpu.VMEM((tm, n), F32)],
        compiler_params=_cparams(("arbitrary", "arbitrary")),
        name="matmul_res_ln",
    )(x, w, res, g.reshape(1, n), b.reshape(1, n))


def _add_ln_body(res_ref, y_ref, g_ref, b_ref, o32_ref, o16_ref, *, alpha):
    out = _res_ln(res_ref[...], y_ref[...], g_ref[...], b_ref[...], alpha)
    o32_ref[...] = out
    o16_ref[...] = out.astype(BF16)


def add_ln(res, y, g, b, alpha, tm=512):
    m, n = res.shape
    tm = _tile(m, tm, 8)
    row = pl.BlockSpec((tm, n), lambda i: (i, 0))
    vec = pl.BlockSpec((1, n), lambda i: (0, 0))
    return pl.pallas_call(
        functools.partial(_add_ln_body, alpha=alpha),
        out_shape=(jax.ShapeDtypeStruct((m, n), F32), jax.ShapeDtypeStruct((m, n), BF16)),
        grid=(m // tm,),
        in_specs=[row, row, vec, vec],
        out_specs=(row, row),
        compiler_params=_cparams(("arbitrary",)),
        name="add_ln",
    )(res, y, g.reshape(1, n), b.reshape(1, n))


def _router_body(x_ref, w_ref, o_ref):
    o_ref[...] = jnp.dot(x_ref[...], w_ref[...], preferred_element_type=F32, precision=HIGHEST)


def router_logits(x, w_pad, tm=512):
    m, k = x.shape
    n = w_pad.shape[-1]
    tm = _tile(m, tm, 8)
    return pl.pallas_call(
        _router_body,
        out_shape=jax.ShapeDtypeStruct((m, n), F32),
        grid=(m // tm,),
        in_specs=[pl.BlockSpec((tm, k), lambda i: (i, 0)), pl.BlockSpec((k, n), lambda i: (0, 0))],
        out_specs=pl.BlockSpec((tm, n), lambda i: (i, 0)),
        compiler_params=_cparams(("arbitrary",)),
        name="router_logits",
    )(x, w_pad)


def _dn_prep_body(cur_ref, prev_ref, next_ref, w_ref, q_ref, k_ref, v_ref, buf_ref, *, ts, nblk):
    i = pl.program_id(1)
    buf_ref[0:HALO, :] = jnp.where(i > 0, prev_ref[0], 0.0)
    buf_ref[HALO:HALO + ts, :] = cur_ref[0]
    buf_ref[HALO + ts:2 * HALO + ts, :] = jnp.where(i < nblk - 1, next_ref[0], 0.0)
    base = HALO - CONV_W // 2
    acc = buf_ref[base:base + ts, :] * w_ref[0:1, :]
    for j in range(1, CONV_W):
        acc = acc + buf_ref[base + j:base + j + ts, :] * w_ref[j:j + 1, :]
    y = acc * jax.nn.sigmoid(acc)
    for h in range(DN_HEADS):
        for off, ref in ((0, q_ref), (DN_W, k_ref)):
            t = y[:, off + h * DN_D:off + (h + 1) * DN_D]
            ref[0, :, h * DN_D:(h + 1) * DN_D] = t * lax.rsqrt(jnp.sum(t * t, axis=-1, keepdims=True) + EPS)
    v_ref[0] = y[:, 2 * DN_W:3 * DN_W]


def dn_prep(proj, conv_w, ts=256):
    b, s, _ = proj.shape
    ts = _tile(s, ts, HALO)
    nblk = s // ts
    c = 3 * DN_W
    hb = ts // HALO
    out = jax.ShapeDtypeStruct((b, s, DN_W), F32)
    ospec = pl.BlockSpec((1, ts, DN_W), lambda bb, i: (bb, i, 0))
    return pl.pallas_call(
        functools.partial(_dn_prep_body, ts=ts, nblk=nblk),
        out_shape=(out, out, out),
        grid=(b, nblk),
        in_specs=[pl.BlockSpec((1, ts, c), lambda bb, i: (bb, i, 0)),
                  pl.BlockSpec((1, HALO, c), lambda bb, i: (bb, jnp.maximum(i * hb - 1, 0), 0)),
                  pl.BlockSpec((1, HALO, c), lambda bb, i: (bb, jnp.minimum((i + 1) * hb, nblk * hb - 1), 0)),
                  pl.BlockSpec((CONV_W, c), lambda bb, i: (0, 0))],
        out_specs=(ospec, ospec, ospec),
        scratch_shapes=[pltpu.VMEM((ts + 2 * HALO, c), F32)],
        compiler_params=_cparams(("arbitrary", "arbitrary")),
        name="dn_prep",
    )(proj, proj, proj, conv_w)


def _bdot(a, b):
    return jnp.dot(a.astype(BF16), b.astype(BF16), preferred_element_type=F32)


def _bdot_nt(a, b):
    return lax.dot_general(a.astype(BF16), b.astype(BF16), (((1,), (1,)), ((), ())),
                           preferred_element_type=F32)


def _bdot_tn(a, b):
    return lax.dot_general(a.astype(BF16), b.astype(BF16), (((0,), (0,)), ((), ())),
                           preferred_element_type=F32)


def _hdot(a, b):
    return jnp.dot(a, b, preferred_element_type=F32, precision=HIGHEST)


def _tri_masks(c):
    ri = lax.broadcasted_iota(jnp.int32, (c, c), 0)
    ci = lax.broadcasted_iota(jnp.int32, (c, c), 1)
    return ri >= ci, ri > ci, ri == ci


def _dn_scan_body(q_ref, k_ref, v_ref, ab_ref, par_ref, o_ref, state_ref):
    n = pl.program_id(1)

    @pl.when(n == 0)
    def _():
        state_ref[...] = jnp.zeros_like(state_ref)

    c = CHUNK
    incl, strict, diag = _tri_masks(c)
    ltri = incl.astype(F32)
    eye = diag.astype(F32)
    ab = ab_ref[0]
    par = par_ref[0]
    g_all = -jnp.exp(par[0:1, :]) * jax.nn.softplus(ab + par[1:2, :])
    beta_all = jax.nn.sigmoid(ab)
    gc_all = _hdot(ltri, g_all)
    gc_all_t = gc_all.T
    scale = DN_D ** -0.5
    for h in range(DN_HEADS):
        sl = slice(h * DN_D, (h + 1) * DN_D)
        q = q_ref[0, :, sl] * scale
        k = k_ref[0, :, sl]
        v = v_ref[0, :, sl]
        gcol = gc_all[:, h:h + 1]
        grow = gc_all_t[h:h + 1, :]
        bcol = beta_all[:, DN_HEADS + h:DN_HEADS + h + 1]
        decay = jnp.where(incl, jnp.exp(jnp.where(incl, gcol - grow, 0.0)), 0.0)
        eg = jnp.exp(gcol)
        kb = k * bcol
        m = jnp.where(strict, _bdot_nt(kb, k) * decay, 0.0)
        nm = -m
        tinv = eye + nm
        for _ in range(int(math.log2(c)) - 1):
            nm = _hdot(nm, nm)
            tinv = tinv + _hdot(tinv, nm)
        uw = _bdot(tinv, jnp.concatenate([v * bcol, kb * eg], axis=1))
        u, w = uw[:, :DN_D], uw[:, DN_D:]
        qk = _bdot_nt(q, k) * decay
        gl = gcol[c - 1:c, :]
        kt = k * jnp.exp(gl - gcol)
        state = state_ref[h]
        v_new = u - _bdot(w, state)
        o_ref[0, :, sl] = _bdot(q * eg, state) + _bdot(qk, v_new)
        state_ref[h] = state * jnp.exp(gl) + _bdot_tn(kt, v_new)


def dn_scan(q, k, v, ab, par):
    z, s, w = q.shape
    nchunk = s // CHUNK
    seq = pl.BlockSpec((1, CHUNK, w), lambda zz, n: (zz, n, 0))
    return pl.pallas_call(
        _dn_scan_body,
        out_shape=jax.ShapeDtypeStruct((z, s, w), F32),
        grid=(z, nchunk),
        in_specs=[seq, seq, seq,
                  pl.BlockSpec((1, CHUNK, LANES), lambda zz, n: (zz, n, 0)),
                  pl.BlockSpec((1, 2, LANES), lambda zz, n: (zz, 0, 0))],
        out_specs=seq,
        scratch_shapes=[pltpu.VMEM((DN_HEADS, DN_D, DN_D), F32)],
        compiler_params=_cparams(("arbitrary", "arbitrary")),
        name="dn_scan",
    )(q, k, v, ab, par)


def _rope_head(x, g, cos, sin_signed, first_half):
    xf = x * lax.rsqrt(jnp.mean(x * x, axis=-1, keepdims=True) + EPS) * g
    partner = jnp.where(first_half, pltpu.roll(xf, LANES - ROPE_SUB // 2, axis=1),
                        pltpu.roll(xf, ROPE_SUB // 2, axis=1))
    return xf * cos + partner * sin_signed


def _att_prep_body(q_ref, k_ref, v_ref, cos_ref, sin_ref, qg_ref, kg_ref, qo_ref, ko_ref, vo_ref):
    cos = cos_ref[...]
    sin_signed = sin_ref[...]
    lane = lax.broadcasted_iota(jnp.int32, cos.shape, 1)
    first_half = (lane % ROPE_SUB) < ROPE_SUB // 2
    scale = ATT_DH ** -0.5
    for h in range(ATT_HEADS):
        sl = slice(h * ATT_DH, (h + 1) * ATT_DH)
        qo_ref[0, :, sl] = (_rope_head(q_ref[0, :, sl], qg_ref[...], cos, sin_signed, first_half)
                            * scale).astype(BF16)
    for h in range(ATT_KV_HEADS):
        sl = slice(h * ATT_DH, (h + 1) * ATT_DH)
        ko_ref[0, :, sl] = _rope_head(k_ref[0, :, sl], kg_ref[...], cos, sin_signed, first_half).astype(BF16)
    vo_ref[0] = v_ref[0].astype(BF16)


def att_prep(proj, cos, sin_signed, qg, kg, ts=512):
    b, s, _ = proj.shape
    ts = _tile(s, ts, 16)
    tab = pl.BlockSpec((ts, ATT_DH), lambda bb, i: (i, 0))
    vec = pl.BlockSpec((1, ATT_DH), lambda bb, i: (0, 0))
    return pl.pallas_call(
        _att_prep_body,
        out_shape=(jax.ShapeDtypeStruct((b, s, ATT_W), BF16),
                   jax.ShapeDtypeStruct((b, s, ATT_KV_W), BF16),
                   jax.ShapeDtypeStruct((b, s, ATT_KV_W), BF16)),
        grid=(b, s // ts),
        in_specs=[pl.BlockSpec((1, ts, ATT_W), lambda bb, i: (bb, i, OFF_AQ // ATT_W)),
                  pl.BlockSpec((1, ts, ATT_KV_W), lambda bb, i: (bb, i, OFF_AK // ATT_KV_W)),
                  pl.BlockSpec((1, ts, ATT_KV_W), lambda bb, i: (bb, i, OFF_AV // ATT_KV_W)),
                  tab, tab, vec, vec],
        out_specs=(pl.BlockSpec((1, ts, ATT_W), lambda bb, i: (bb, i, 0)),
                   pl.BlockSpec((1, ts, ATT_KV_W), lambda bb, i: (bb, i, 0)),
                   pl.BlockSpec((1, ts, ATT_KV_W), lambda bb, i: (bb, i, 0))),
        compiler_params=_cparams(("arbitrary", "arbitrary")),
        name="att_prep",
    )(proj, proj, proj, cos, sin_signed, qg.reshape(1, ATT_DH), kg.reshape(1, ATT_DH))


def _flash_body(q_ref, k_ref, v_ref, o_ref, *, tk, nkv):
    tq = q_ref.shape[1]
    for h in range(ATT_GROUP):
        sl = slice(h * ATT_DH, (h + 1) * ATT_DH)
        q = q_ref[0, :, sl]

        def step(t, carry, q=q):
            m, l, acc = carry
            start = pl.multiple_of(t * tk, tk)
            ks = k_ref[0, pl.ds(start, tk), :]
            vs = v_ref[0, pl.ds(start, tk), :]
            sc = lax.dot_general(q, ks, (((1,), (1,)), ((), ())), preferred_element_type=F32)
            m_new = jnp.maximum(m, jnp.max(sc, axis=-1, keepdims=True))
            p = jnp.exp(sc - m_new)
            corr = jnp.exp(m - m_new)
            l = corr * l + jnp.sum(p, axis=-1, keepdims=True)
            acc = corr * acc + jnp.dot(p.astype(BF16), vs, preferred_element_type=F32)
            return m_new, l, acc

        init = (jnp.full((tq, 1), -1e30, F32), jnp.zeros((tq, 1), F32), jnp.zeros((tq, ATT_DH), F32))
        _, l, acc = lax.fori_loop(0, nkv, step, init)
        o_ref[0, :, sl] = (acc / l).astype(o_ref.dtype)


def flash_attention(q, k, v, tq=256, tk=512):
    b, s, _ = q.shape
    tq = _tile(s, tq, 16)
    tk = _tile(s, tk, 16)
    gw = ATT_GROUP * ATT_DH
    return pl.pallas_call(
        functools.partial(_flash_body, tk=tk, nkv=s // tk),
        out_shape=jax.ShapeDtypeStruct((b, s, ATT_W), BF16),
        grid=(b, ATT_KV_HEADS, s // tq),
        in_specs=[pl.BlockSpec((1, tq, gw), lambda bb, g, i: (bb, i, g)),
                  pl.BlockSpec((1, s, ATT_DH), lambda bb, g, i: (bb, 0, g)),
                  pl.BlockSpec((1, s, ATT_DH), lambda bb, g, i: (bb, 0, g))],
        out_specs=pl.BlockSpec((1, tq, gw), lambda bb, g, i: (bb, i, g)),
        compiler_params=_cparams(("arbitrary", "arbitrary", "arbitrary")),
        name="flash_attention",
    )(q, k, v)


def _gla_scan_body(q_ref, k_ref, v_ref, lr_ref, up_ref, upb_ref, o_ref, state_ref):
    n = pl.program_id(1)

    @pl.when(n == 0)
    def _():
        state_ref[...] = jnp.zeros_like(state_ref)

    c = CHUNK
    nsub = c // SUB
    incl, _, _ = _tri_masks(c)
    ltri = incl.astype(F32)
    qs = q_ref[0] * (GLA_DK ** -0.5)
    k = k_ref[0]
    v = v_ref[0]
    gk = jax.nn.log_sigmoid(_hdot(lr_ref[0], up_ref[0]) + upb_ref[0]) * (1.0 / GLA_NORMALIZER)
    gc = _hdot(ltri, gk)
    gl = gc[c - 1:c, :]
    qd = qs * jnp.exp(gc)
    kt = k * jnp.exp(gl - gc)
    egl = jnp.exp(gl)
    row = lax.broadcasted_iota(jnp.int32, (c, GLA_QK), 0)

    a_off = [[jnp.zeros((SUB, c), F32)] for _ in range(GLA_HEADS)]
    for i in range(1, nsub):
        ref = gc[i * SUB:i * SUB + 1, :]
        qi = qs[i * SUB:(i + 1) * SUB, :] * jnp.exp(gc[i * SUB:(i + 1) * SUB, :] - ref)
        early = row < i * SUB
        kf = k * jnp.where(early, jnp.exp(jnp.where(early, ref - gc, 0.0)), 0.0)
        for h in range(GLA_HEADS):
            sl = slice(h * GLA_DK, (h + 1) * GLA_DK)
            a_off[h].append(_bdot_nt(qi[:, sl], kf[:, sl]))

    lane = lax.broadcasted_iota(jnp.int32, (LANES, 2 * GLA_DV), 0)
    col = lax.broadcasted_iota(jnp.int32, (LANES, 2 * GLA_DV), 1)
    head_sum = ((lane // GLA_DK) == (col // GLA_DV)).astype(BF16)
    sub_row = lax.broadcasted_iota(jnp.int32, (SUB, LANES), 0)
    o_diag = []
    for p in range(GLA_HEADS // 2):
        ls = slice(p * LANES, (p + 1) * LANES)
        vs = slice(p * 2 * GLA_DV, (p + 1) * 2 * GLA_DV)
        blocks = []
        for i in range(nsub):
            rs = slice(i * SUB, (i + 1) * SUB)
            gci, qsi = gc[rs, ls], qs[rs, ls]
            terms = []
            for jj in range(SUB):
                j = i * SUB + jj
                later = sub_row >= jj
                e = jnp.where(later, jnp.exp(jnp.where(later, gci - gc[j:j + 1, ls], 0.0)), 0.0)
                terms.append(qsi * k[j:j + 1, ls] * e)
            r = jnp.dot(jnp.concatenate(terms, axis=0).astype(BF16), head_sum,
                        preferred_element_type=F32)
            acc = r[0:SUB, :] * v[i * SUB:i * SUB + 1, vs]
            for jj in range(1, SUB):
                acc = acc + r[jj * SUB:(jj + 1) * SUB, :] * v[i * SUB + jj:i * SUB + jj + 1, vs]
            blocks.append(acc)
        o_diag.append(jnp.concatenate(blocks, axis=0))

    for h in range(GLA_HEADS):
        sl = slice(h * GLA_DK, (h + 1) * GLA_DK)
        vsl = slice(h * GLA_DV, (h + 1) * GLA_DV)
        st = state_ref[h]
        vh = v[:, vsl]
        a = jnp.concatenate(a_off[h], axis=0)
        od = o_diag[h // 2][:, (h % 2) * GLA_DV:(h % 2 + 1) * GLA_DV]
        o_ref[0, :, vsl] = _bdot_nt(qd[:, sl], st) + _bdot(a, vh) + od
        state_ref[h] = st * egl[:, sl] + _bdot_tn(vh, kt[:, sl])


def gla_scan(q, k, v, lr, up, upb):
    z, s, _ = q.shape
    nchunk = s // CHUNK
    qk_spec = pl.BlockSpec((1, CHUNK, GLA_QK), lambda zz, n: (zz, n, 0))
    v_spec = pl.BlockSpec((1, CHUNK, GLA_W), lambda zz, n: (zz, n, 0))
    return pl.pallas_call(
        _gla_scan_body,
        out_shape=jax.ShapeDtypeStruct((z, s, GLA_W), F32),
        grid=(z, nchunk),
        in_specs=[qk_spec, qk_spec, v_spec,
                  pl.BlockSpec((1, CHUNK, GLA_RANK), lambda zz, n: (zz, n, 0)),
                  pl.BlockSpec((1, GLA_RANK, GLA_QK), lambda zz, n: (zz, 0, 0)),
                  pl.BlockSpec((1, 1, GLA_QK), lambda zz, n: (zz, 0, 0))],
        out_specs=v_spec,
        scratch_shapes=[pltpu.VMEM((GLA_HEADS, GLA_DV, GLA_DK), F32)],
        compiler_params=_cparams(("arbitrary", "arbitrary")),
        name="gla_scan",
    )(q, k, v, lr, up, upb)


def _gated_norm(o, gate, g, heads, d):
    outs = []
    for h in range(heads):
        sl = slice(h * d, (h + 1) * d)
        t = o[:, sl]
        y = t * lax.rsqrt(jnp.mean(t * t, axis=-1, keepdims=True) + EPS) * g
        gt = gate[:, sl]
        outs.append(y * (gt * jax.nn.sigmoid(gt)))
    return jnp.concatenate(outs, axis=1)


def _merge_body(dnf_ref, dnb_ref, dgate_ref, dng_ref, att_ref, glf_ref, glb_ref, ggate_ref, glg_ref, o_ref):
    o_dn = _gated_norm(dnf_ref[0] + dnb_ref[0], dgate_ref[0], dng_ref[...], DN_HEADS, DN_D)
    o_gla = _gated_norm(glf_ref[0] + glb_ref[0], ggate_ref[0], glg_ref[...], GLA_HEADS, GLA_DV)
    o_ref[0, :, 0:DN_W] = o_dn.astype(BF16)
    o_ref[0, :, DN_W:DN_W + ATT_W] = att_ref[0]
    o_ref[0, :, DN_W + ATT_W:] = o_gla.astype(BF16)


def merge_heads(dn_f, dn_b, proj, dn_g, att, gl_f, gl_b, gla_g, ts=512):
    b, s, _ = proj.shape
    ts = _tile(s, ts, 16)
    mix_w = DN_W + ATT_W + GLA_W

    def spec(w, blk=0):
        return pl.BlockSpec((1, ts, w), lambda bb, i: (bb, i, blk))

    vec = pl.BlockSpec((1, LANES), lambda bb, i: (0, 0))
    return pl.pallas_call(
        _merge_body,
        out_shape=jax.ShapeDtypeStruct((b, s, mix_w), BF16),
        grid=(b, s // ts),
        in_specs=[spec(DN_W), spec(DN_W), spec(DN_W, OFF_DGATE // DN_W), vec,
                  spec(ATT_W), spec(GLA_W), spec(GLA_W), spec(GLA_W, OFF_GGATE // GLA_W), vec],
        out_specs=spec(mix_w),
        compiler_params=_cparams(("arbitrary", "arbitrary")),
        name="merge_heads",
    )(dn_f, dn_b, proj, dn_g.reshape(1, DN_D), att, gl_f, gl_b, proj, gla_g.reshape(1, GLA_DV))


def _relayout_w_in(w):
    d = w.shape[0]
    (dq, dk, dv, dgate, a_f, a_b, b_f, b_b, aq, ak, av, gq, gkk, gv, ggate, lr_f, lr_b) = jnp.split(
        w, _split_points(), axis=1)
    small = jnp.concatenate([a_f, a_b, b_f, b_b, lr_f, lr_b], axis=1)
    small = jnp.pad(small, ((0, 0), (0, LANES - small.shape[1])))
    cols = jnp.concatenate([dq, dk, dv, dgate, aq, ak, av, gq, gv, ggate, gkk, small,
                            jnp.zeros((d, PROJ_COLS - OFF_SMALL - LANES), w.dtype)], axis=1)
    return cols.astype(BF16)[None]


def _split_points():
    pts, acc = [], 0
    for sz in IN_SIZES[:-1]:
        acc += sz
        pts.append(acc)
    return pts


def _rope_tables(s):
    rows = s // GRID_W
    row = jnp.repeat(jnp.arange(rows, dtype=jnp.int32), GRID_W).astype(F32)
    col = jnp.tile(jnp.arange(GRID_W, dtype=jnp.int32), rows).astype(F32)
    inv_freq = ROPE_THETA ** (-jnp.arange(0, ROPE_SUB, 2, dtype=F32) / ROPE_SUB)
    ang_r = row[:, None] * inv_freq[None, :]
    ang_c = col[:, None] * inv_freq[None, :]
    cos = jnp.concatenate([jnp.cos(ang_r), jnp.cos(ang_r), jnp.cos(ang_c), jnp.cos(ang_c)], axis=1)
    sin = jnp.concatenate([-jnp.sin(ang_r), jnp.sin(ang_r), -jnp.sin(ang_c), jnp.sin(ang_c)], axis=1)
    return cos, sin


def _both_directions(t):
    return jnp.concatenate([t, jnp.flip(t, axis=1)], axis=0)


def _mixer(x16, bsz, s, w_in, dn_conv, dn_a_log, dn_dt_bias, dn_norm_g, att_qn_g, att_kn_g,
           gla_up, gla_up_b, gla_norm_g, rope):
    t = bsz * s
    bm = _tile(t, 512, 16)
    proj = grouped_matmul(x16, _relayout_w_in(w_in), jnp.zeros((t // bm,), jnp.int32), bm, 1024, F32)
    proj = proj.reshape(bsz, s, PROJ_COLS)

    q, k, v = dn_prep(proj, dn_conv)
    small = proj[:, :, OFF_SMALL:OFF_SMALL + LANES]
    ab_f = jnp.concatenate([small[..., SM_AF:SM_AF + DN_HEADS], small[..., SM_BF:SM_BF + DN_HEADS]], axis=-1)
    ab_b = jnp.concatenate([small[..., SM_AB:SM_AB + DN_HEADS], small[..., SM_BB:SM_BB + DN_HEADS]], axis=-1)
    ab = jnp.concatenate([ab_f, jnp.flip(ab_b, axis=1)], axis=0)
    ab = jnp.pad(ab, ((0, 0), (0, 0), (0, LANES - 2 * DN_HEADS)))
    par = jnp.stack([dn_a_log, dn_dt_bias], axis=1)
    par = jnp.pad(par, ((0, 0), (0, 0), (0, LANES - DN_HEADS)))
    par = jnp.repeat(par, bsz, axis=0)
    o2 = dn_scan(_both_directions(q), _both_directions(k), _both_directions(v), ab, par)
    dn_f, dn_b = o2[:bsz], jnp.flip(o2[bsz:], axis=1)

    cos, sin_signed = rope
    aq, ak, av = att_prep(proj, cos, sin_signed, att_qn_g, att_kn_g)
    o_att = flash_attention(aq, ak, av)

    gq = proj[:, :, OFF_GQ:OFF_GQ + GLA_QK]
    gkk = proj[:, :, OFF_GK:OFF_GK + GLA_QK]
    gv = proj[:, :, OFF_GV:OFF_GV + GLA_W]
    lr = jnp.concatenate([small[..., SM_LRF:SM_LRF + GLA_RANK],
                          jnp.flip(small[..., SM_LRB:SM_LRB + GLA_RANK], axis=1)], axis=0)
    up = jnp.repeat(gla_up, bsz, axis=0)
    upb = jnp.repeat(gla_up_b, bsz, axis=0)[:, None, :]
    g2 = gla_scan(_both_directions(gq), _both_directions(gkk), _both_directions(gv), lr, up, upb)
    gl_f, gl_b = g2[:bsz], jnp.flip(g2[bsz:], axis=1)

    return merge_heads(dn_f, dn_b, proj, dn_norm_g, o_att, gl_f, gl_b, gla_norm_g).reshape(t, -1)


def _moe(x32, x16, router_w, w_gate, w_up, w_down, ln_g, ln_b, alpha):
    t, d = x32.shape
    rw = jnp.pad(router_w, ((0, 0), (0, LANES - N_EXPERTS)))
    logits = router_logits(x32, rw)[:, :N_EXPERTS]
    top_val, top_idx = lax.top_k(logits, TOP_K)
    gates = jax.nn.softmax(top_val, axis=-1)
    e_flat = top_idx.reshape(-1).astype(jnp.int32)
    tok_flat = jnp.repeat(jnp.arange(t, dtype=jnp.int32), TOP_K)
    onehot = (e_flat[:, None] == jnp.arange(N_EXPERTS, dtype=jnp.int32)[None, :]).astype(jnp.int32)
    csum = jnp.cumsum(onehot, axis=0)
    counts = csum[-1]
    rank = jnp.take_along_axis(csum, e_flat[:, None], axis=1)[:, 0] - 1
    padded = (counts + MOE_BLOCK - 1) // MOE_BLOCK * MOE_BLOCK
    pstart = jnp.cumsum(padded) - padded
    pend = pstart + padded
    dest = pstart[e_flat] + rank
    nb = -(-(TOP_K * t) // MOE_BLOCK) + N_EXPERTS
    cap = nb * MOE_BLOCK
    buf_tok = jnp.zeros((cap,), jnp.int32).at[dest].set(tok_flat)
    block_start = jnp.arange(nb, dtype=jnp.int32) * MOE_BLOCK
    block_exp = jnp.minimum(jnp.sum(block_start[:, None] >= pend[None, :], axis=-1),
                            N_EXPERTS - 1).astype(jnp.int32)
    xb = x16[buf_tok]
    h = grouped_swiglu(xb, w_gate.astype(BF16), w_up.astype(BF16), block_exp, MOE_BLOCK, 1024)
    yb = grouped_matmul(h, w_down.astype(BF16), block_exp, MOE_BLOCK, 1024, F32)
    y = (yb[dest] * gates.reshape(-1)[:, None]).reshape(t, TOP_K, d).sum(axis=1)
    return add_ln(x32, y, ln_g, ln_b, alpha)


def _dense_ffn(x32, x16, w_gate, w_up, w_down, ln_g, ln_b, alpha):
    t = x32.shape[0]
    bm = _tile(t, 512, 16)
    grp = jnp.zeros((t // bm,), jnp.int32)
    h = grouped_swiglu(x16, w_gate.astype(BF16)[None], w_up.astype(BF16)[None], grp, bm, 1024)
    return matmul_res_ln(h, w_down.astype(BF16), x32, ln_g, ln_b, alpha)


def kernel(x, w_in, dn_conv, dn_a_log, dn_dt_bias, dn_norm_g, att_qn_g, att_kn_g, gla_up, gla_up_b,
           gla_norm_g, w_out, ln1_g, ln1_b, ln2_g, ln2_b, ffn_w_gate, ffn_w_up, ffn_w_down, router_w,
           exp_w_gate, exp_w_up, exp_w_down):
    bsz, s, d = x.shape
    depth = w_in.shape[0]
    alpha = (2.0 * depth) ** 0.25
    t = bsz * s
    rope = _rope_tables(s)
    x32 = x.reshape(t, d)
    x16 = x32.astype(BF16)
    for layer in range(depth):
        mix = _mixer(x16, bsz, s, w_in[layer], dn_conv[layer], dn_a_log[layer], dn_dt_bias[layer],
                     dn_norm_g[layer], att_qn_g[layer], att_kn_g[layer], gla_up[layer],
                     gla_up_b[layer], gla_norm_g[layer], rope)
        x32, x16 = matmul_res_ln(mix, w_out[layer].astype(BF16), x32, ln1_g[layer], ln1_b[layer], alpha)
        j = layer // 2
        if layer % 2 == 0:
            x32, x16 = _dense_ffn(x32, x16, ffn_w_gate[j], ffn_w_up[j], ffn_w_down[j],
                                  ln2_g[layer], ln2_b[layer], alpha)
        else:
            x32, x16 = _moe(x32, x16, router_w[j], exp_w_gate[j], exp_w_up[j], exp_w_down[j],
                            ln2_g[layer], ln2_b[layer], alpha)
    return x32.reshape(bsz, s, d)
```

```python
import functools
import math

import jax
import jax.numpy as jnp
from jax import lax
from jax.experimental import pallas as pl
from jax.experimental.pallas import tpu as pltpu

F32 = jnp.float32
BF16 = jnp.bfloat16
HIGHEST = lax.Precision.HIGHEST

DN_HEADS, DN_D = 6, 128
ATT_HEADS, ATT_KV_HEADS, ATT_DH = 6, 2, 128
ATT_GROUP = ATT_HEADS // ATT_KV_HEADS
ROPE_SUB, ROPE_THETA, GRID_W = 64, 10000.0, 64
GLA_HEADS, GLA_DK, GLA_DV, GLA_RANK = 4, 64, 128, 16
GLA_NORMALIZER = 16.0
CHUNK = 64
SUB = 16
WY_ROWS = 2 * CHUNK
CONV_W = 5
N_EXPERTS, TOP_K, MOE_BLOCK = 8, 2, 256
EPS = 1e-6
LOG2E = 1.4426950408889634
LANES = 128
HALO = 8

DN_W = DN_HEADS * DN_D
ATT_W = ATT_HEADS * ATT_DH
ATT_KV_W = ATT_KV_HEADS * ATT_DH
GLA_QK = GLA_HEADS * GLA_DK
GLA_W = GLA_HEADS * GLA_DV
IN_SIZES = (DN_W, DN_W, DN_W, DN_W, DN_HEADS, DN_HEADS, DN_HEADS, DN_HEADS,
            ATT_W, ATT_KV_W, ATT_KV_W, GLA_QK, GLA_QK, GLA_W, GLA_W, GLA_RANK, GLA_RANK)
OFF_DQ, OFF_DGATE, OFF_AQ, OFF_AK, OFF_AV = 0, 2304, 3072, 3840, 4096
OFF_GQ, OFF_GV, OFF_GGATE, OFF_GK, OFF_SMALL = 4352, 4608, 5120, 5632, 5888
PROJ_COLS = 6144
SM_A, SM_B, SM_LR = 0, 2 * DN_HEADS, 4 * DN_HEADS

VMEM_LIMIT = 56 * 1024 * 1024


def _cparams(sem, vmem=VMEM_LIMIT):
    return pltpu.CompilerParams(dimension_semantics=sem, vmem_limit_bytes=vmem)


def _tile(n, pref, quantum=LANES):
    if n <= pref:
        return n
    t = pref - pref % quantum
    while n % t:
        t -= quantum
    return t


def _bdot(a, b):
    return jnp.dot(a.astype(BF16), b.astype(BF16), preferred_element_type=F32)


def _bdot_nt(a, b):
    return lax.dot_general(a.astype(BF16), b.astype(BF16), (((1,), (1,)), ((), ())),
                           preferred_element_type=F32)


def _hdot(a, b):
    return jnp.dot(a, b, preferred_element_type=F32, precision=HIGHEST)


def _group_changed(be_ref):
    i = pl.program_id(1)
    return (i == 0) | (be_ref[i] != be_ref[jnp.maximum(i - 1, 0)])


def _gmm_body(be_ref, x_ref, w_ref, o_ref, *w16):
    if w16:
        @pl.when(_group_changed(be_ref))
        def _():
            w16[0][...] = w_ref[...].astype(BF16)
        w = w16[0][...]
    else:
        w = w_ref[...]
    o_ref[...] = jnp.dot(x_ref[...], w, preferred_element_type=F32).astype(o_ref.dtype)


def grouped_matmul(x, w, layer, block_group, bm, tn, out_dtype):
    m, k = x.shape
    n = w.shape[-1]
    tn = _tile(n, tn)
    scratch = [pltpu.VMEM((k, tn), BF16)] if w.dtype != BF16 else []
    return pl.pallas_call(
        _gmm_body,
        out_shape=jax.ShapeDtypeStruct((m, n), out_dtype),
        grid_spec=pltpu.PrefetchScalarGridSpec(
            num_scalar_prefetch=1, grid=(n // tn, m // bm),
            in_specs=[pl.BlockSpec((bm, k), lambda j, i, be: (i, 0)),
                      pl.BlockSpec((None, None, k, tn), lambda j, i, be: (layer, be[i], 0, j))],
            out_specs=pl.BlockSpec((bm, tn), lambda j, i, be: (i, j)),
            scratch_shapes=scratch),
        compiler_params=_cparams(("arbitrary", "arbitrary")),
        name="grouped_matmul",
    )(block_group, x, w)


def _gswiglu_body(be_ref, x_ref, wg_ref, wu_ref, o_ref, wg16, wu16):
    @pl.when(_group_changed(be_ref))
    def _():
        wg16[...] = wg_ref[...].astype(BF16)
        wu16[...] = wu_ref[...].astype(BF16)

    x = x_ref[...]
    g = jnp.dot(x, wg16[...], preferred_element_type=F32)
    u = jnp.dot(x, wu16[...], preferred_element_type=F32)
    o_ref[...] = (g * jax.nn.sigmoid(g) * u).astype(o_ref.dtype)


def grouped_swiglu(x, wg, wu, layer, block_group, bm, tn):
    m, k = x.shape
    n = wg.shape[-1]
    tn = _tile(n, tn)
    return pl.pallas_call(
        _gswiglu_body,
        out_shape=jax.ShapeDtypeStruct((m, n), BF16),
        grid_spec=pltpu.PrefetchScalarGridSpec(
            num_scalar_prefetch=1, grid=(n // tn, m // bm),
            in_specs=[pl.BlockSpec((bm, k), lambda j, i, be: (i, 0)),
                      pl.BlockSpec((None, None, k, tn), lambda j, i, be: (layer, be[i], 0, j)),
                      pl.BlockSpec((None, None, k, tn), lambda j, i, be: (layer, be[i], 0, j))],
            out_specs=pl.BlockSpec((bm, tn), lambda j, i, be: (i, j)),
            scratch_shapes=[pltpu.VMEM((k, tn), BF16), pltpu.VMEM((k, tn), BF16)]),
        compiler_params=_cparams(("arbitrary", "arbitrary")),
        name="grouped_swiglu",
    )(block_group, x, wg, wu)


def _res_ln(res, y, g, b, alpha):
    z = alpha * res + y
    mu = jnp.mean(z, axis=-1, keepdims=True)
    zc = z - mu
    var = jnp.mean(zc * zc, axis=-1, keepdims=True)
    return zc * lax.rsqrt(var + EPS) * g + b


def _mm_res_ln_body(x_ref, w_ref, res_ref, g_ref, b_ref, o32_ref, o16_ref, acc_ref, *, nk, alpha):
    kk = pl.program_id(1)

    @pl.when(kk == 0)
    def _():
        acc_ref[...] = jnp.zeros_like(acc_ref)

    acc_ref[...] += jnp.dot(x_ref[...], w_ref[...], preferred_element_type=F32)

    @pl.when(kk == nk - 1)
    def _():
        out = _res_ln(res_ref[...], acc_ref[...], g_ref[...], b_ref[...], alpha)
        o32_ref[...] = out
        o16_ref[...] = out.astype(BF16)


def matmul_res_ln(x, w, res, g, b, alpha, tm=512, tk=512):
    m, k = x.shape
    n = w.shape[-1]
    tm = _tile(m, tm, 8)
    tk = _tile(k, tk)
    nk = k // tk
    return pl.pallas_call(
        functools.partial(_mm_res_ln_body, nk=nk, alpha=alpha),
        out_shape=(jax.ShapeDtypeStruct((m, n), F32), jax.ShapeDtypeStruct((m, n), BF16)),
        grid=(m // tm, nk),
        in_specs=[pl.BlockSpec((tm, tk), lambda i, kk: (i, kk)),
                  pl.BlockSpec((tk, n), lambda i, kk: (kk, 0)),
                  pl.BlockSpec((tm, n), lambda i, kk: (i, 0)),
                  pl.BlockSpec((1, n), lambda i, kk: (0, 0)),
                  pl.BlockSpec((1, n), lambda i, kk: (0, 0))],
        out_specs=(pl.BlockSpec((tm, n), lambda i, kk: (i, 0)),
                   pl.BlockSpec((tm, n), lambda i, kk: (i, 0))),
        scratch_shapes=[pltpu.VMEM((tm, n), F32)],
        compiler_params=_cparams(("arbitrary", "arbitrary")),
        name="matmul_res_ln",
    )(x, w, res, g.reshape(1, n), b.reshape(1, n))


def _add_ln_body(res_ref, y_ref, g_ref, b_ref, o32_ref, o16_ref, *, alpha):
    out = _res_ln(res_ref[...], y_ref[...], g_ref[...], b_ref[...], alpha)
    o32_ref[...] = out
    o16_ref[...] = out.astype(BF16)


def add_ln(res, y, g, b, alpha, tm=512):
    m, n = res.shape
    tm = _tile(m, tm, 8)
    row = pl.BlockSpec((tm, n), lambda i: (i, 0))
    vec = pl.BlockSpec((1, n), lambda i: (0, 0))
    return pl.pallas_call(
        functools.partial(_add_ln_body, alpha=alpha),
        out_shape=(jax.ShapeDtypeStruct((m, n), F32), jax.ShapeDtypeStruct((m, n), BF16)),
        grid=(m // tm,),
        in_specs=[row, row, vec, vec],
        out_specs=(row, row),
        compiler_params=_cparams(("arbitrary",)),
        name="add_ln",
    )(res, y, g.reshape(1, n), b.reshape(1, n))


def _router_body(x_ref, w_ref, o_ref):
    o_ref[...] = _hdot(x_ref[...], w_ref[...])


def router_logits(x, w_pad, tm=512):
    m, k = x.shape
    n = w_pad.shape[-1]
    tm = _tile(m, tm, 8)
    return pl.pallas_call(
        _router_body,
        out_shape=jax.ShapeDtypeStruct((m, n), F32),
        grid=(m // tm,),
        in_specs=[pl.BlockSpec((tm, k), lambda i: (i, 0)), pl.BlockSpec((k, n), lambda i: (0, 0))],
        out_specs=pl.BlockSpec((tm, n), lambda i: (i, 0)),
        compiler_params=_cparams(("arbitrary",)),
        name="router_logits",
    )(x, w_pad)


def _dn_prep_body(cur_ref, prev_ref, next_ref, w_ref, q_ref, k_ref, v_ref, buf_ref, *, ts, nblk):
    i = pl.program_id(1)
    buf_ref[0:HALO, :] = jnp.where(i > 0, prev_ref[0], 0.0)
    buf_ref[HALO:HALO + ts, :] = cur_ref[0]
    buf_ref[HALO + ts:2 * HALO + ts, :] = jnp.where(i < nblk - 1, next_ref[0], 0.0)
    base = HALO - CONV_W // 2
    acc = buf_ref[base:base + ts, :] * w_ref[0:1, :]
    for j in range(1, CONV_W):
        acc = acc + buf_ref[base + j:base + j + ts, :] * w_ref[j:j + 1, :]
    y = acc * jax.nn.sigmoid(acc)
    for h in range(DN_HEADS):
        for off, ref in ((0, q_ref), (DN_W, k_ref)):
            t = y[:, off + h * DN_D:off + (h + 1) * DN_D]
            ref[0, :, h * DN_D:(h + 1) * DN_D] = t * lax.rsqrt(jnp.sum(t * t, axis=-1, keepdims=True) + EPS)
    v_ref[0] = y[:, 2 * DN_W:3 * DN_W]


def dn_prep(proj, conv_w, ts=256):
    b, s, _ = proj.shape
    ts = _tile(s, ts, HALO)
    nblk = s // ts
    c = 3 * DN_W
    hb = ts // HALO
    out = jax.ShapeDtypeStruct((b, s, DN_W), F32)
    ospec = pl.BlockSpec((1, ts, DN_W), lambda bb, i: (bb, i, 0))
    return pl.pallas_call(
        functools.partial(_dn_prep_body, ts=ts, nblk=nblk),
        out_shape=(out, out, out),
        grid=(b, nblk),
        in_specs=[pl.BlockSpec((1, ts, c), lambda bb, i: (bb, i, 0)),
                  pl.BlockSpec((1, HALO, c), lambda bb, i: (bb, jnp.maximum(i * hb - 1, 0), 0)),
                  pl.BlockSpec((1, HALO, c), lambda bb, i: (bb, jnp.minimum((i + 1) * hb, nblk * hb - 1), 0)),
                  pl.BlockSpec((CONV_W, c), lambda bb, i: (0, 0))],
        out_specs=(ospec, ospec, ospec),
        scratch_shapes=[pltpu.VMEM((ts + 2 * HALO, c), F32)],
        compiler_params=_cparams(("arbitrary", "arbitrary")),
        name="dn_prep",
    )(proj, proj, proj, conv_w)


def _dn_wy_body(q_ref, k_ref, v_ref, sm_ref, par_ref, u_ref, wq_ref, ktt_ref, qk_ref, egl_ref):
    r = WY_ROWS
    nck = r // CHUNK
    sm = sm_ref[0]
    par = par_ref[...]
    g_all = -jnp.exp(par[0:1, :]) * jax.nn.softplus(sm + par[1:2, :])
    beta_all = jax.nn.sigmoid(sm)
    ri = lax.broadcasted_iota(jnp.int32, (r, r), 0)
    ci = lax.broadcasted_iota(jnp.int32, (r, r), 1)
    same = (ri // CHUNK) == (ci // CHUNK)
    eye = (ri == ci).astype(F32)
    gtot = _hdot(same.astype(F32), g_all)
    masks, gcs = [], []
    for d in range(2):
        delta = ri - ci if d == 0 else ci - ri
        incl = same & (delta >= 0)
        strict = same & (delta > 0)
        gc = _hdot(incl.astype(F32), g_all)
        masks.append((incl, strict))
        gcs.append((gc, gc.T))
    scale = DN_D ** -0.5
    chains = [(h, d) for h in range(DN_HEADS) for d in range(2)]
    qk_cols = ([], [])
    nm, tinv, rhs = {}, {}, {}
    for h in range(DN_HEADS):
        sl = slice(h * DN_D, (h + 1) * DN_D)
        q = q_ref[0, :, sl] * scale
        k = k_ref[0, :, sl]
        v = v_ref[0, :, sl]
        kq = _bdot_nt(jnp.concatenate([k, q], axis=0), k)
        kk, qk = kq[:r], kq[r:]
        for d in range(2):
            lane = SM_A + d * DN_HEADS + h
            incl, strict = masks[d]
            gc, gct = gcs[d]
            gcol = gc[:, lane:lane + 1]
            grow = gct[lane:lane + 1, :]
            gt = gtot[:, lane:lane + 1]
            bcol = beta_all[:, SM_B + d * DN_HEADS + h:SM_B + d * DN_HEADS + h + 1]
            decay = jnp.where(incl, jnp.exp(jnp.where(incl, gcol - grow, 0.0)), 0.0)
            nm[h, d] = jnp.where(strict, -(bcol * kk) * decay, 0.0)
            tinv[h, d] = eye + nm[h, d]
            eg = jnp.exp(gcol)
            rhs[h, d] = jnp.concatenate([v * bcol, k * (bcol * eg)], axis=1).astype(BF16)
            qd16 = (q * eg).astype(BF16)
            ktt = (k * jnp.exp(gt - gcol)).T.astype(BF16)
            for c in range(nck):
                rows = slice(c * CHUNK, (c + 1) * CHUNK)
                wq_ref[d, 0, c, CHUNK:2 * CHUNK, sl] = qd16[rows]
                ktt_ref[d, 0, c, :, h * CHUNK:(h + 1) * CHUNK] = ktt[:, rows]
            qkd = qk * decay
            qk_cols[d].append(qkd[:, :CHUNK] + qkd[:, CHUNK:])
            egl_ref[d, 0, :, sl] = jnp.broadcast_to(jnp.exp(gt), (r, DN_D))
    for d in range(2):
        qk_ref[d, 0] = jnp.concatenate(qk_cols[d], axis=1).astype(BF16)
    for _ in range(int(math.log2(CHUNK)) - 1):
        for hd in chains:
            nm[hd] = _bdot(nm[hd], nm[hd])
        for hd in chains:
            tinv[hd] = tinv[hd] + _bdot(tinv[hd], nm[hd])
    for h, d in chains:
        sl = slice(h * DN_D, (h + 1) * DN_D)
        uw = _bdot(tinv[h, d], rhs[h, d])
        u_ref[d, 0, :, sl] = uw[:, :DN_D]
        w16 = uw[:, DN_D:].astype(BF16)
        for c in range(nck):
            wq_ref[d, 0, c, 0:CHUNK, sl] = w16[c * CHUNK:(c + 1) * CHUNK]


def dn_wy(q, k, v, proj, par):
    b, s, w = q.shape
    r = WY_ROWS
    nck = r // CHUNK
    nchunk = s // CHUNK
    seq = pl.BlockSpec((1, r, w), lambda bb, i: (bb, i, 0))
    dseq = pl.BlockSpec((2, 1, r, w), lambda bb, i: (0, bb, i, 0))
    return pl.pallas_call(
        _dn_wy_body,
        out_shape=(jax.ShapeDtypeStruct((2, b, s, w), F32),
                   jax.ShapeDtypeStruct((2, b, nchunk, 2 * CHUNK, w), BF16),
                   jax.ShapeDtypeStruct((2, b, nchunk, DN_D, DN_HEADS * CHUNK), BF16),
                   jax.ShapeDtypeStruct((2, b, s, DN_HEADS * CHUNK), BF16),
                   jax.ShapeDtypeStruct((2, b, s, w), F32)),
        grid=(b, s // r),
        in_specs=[seq, seq, seq,
                  pl.BlockSpec((1, r, LANES), lambda bb, i: (bb, i, OFF_SMALL // LANES)),
                  pl.BlockSpec((2, LANES), lambda bb, i: (0, 0))],
        out_specs=(dseq,
                   pl.BlockSpec((2, 1, nck, 2 * CHUNK, w), lambda bb, i: (0, bb, i, 0, 0)),
                   pl.BlockSpec((2, 1, nck, DN_D, DN_HEADS * CHUNK), lambda bb, i: (0, bb, i, 0, 0)),
                   pl.BlockSpec((2, 1, r, DN_HEADS * CHUNK), lambda bb, i: (0, bb, i, 0)),
                   dseq),
        compiler_params=_cparams(("arbitrary", "arbitrary")),
        name="dn_wy",
    )(q, k, v, proj, par)


def _dn_rec_body(*refs):
    ins = (refs[0:5], refs[5:10])
    outs = refs[10:12]
    state_ref = refs[12]
    n = pl.program_id(1)

    @pl.when(n == 0)
    def _():
        state_ref[...] = jnp.zeros_like(state_ref)

    chains = [(d, h) for d in range(2) for h in range(DN_HEADS)]
    state, wq, v_new = {}, {}, {}
    for d, h in chains:
        sl = slice(h * DN_D, (h + 1) * DN_D)
        state[d, h] = state_ref[d, h]
        wq[d, h] = jnp.dot(ins[d][1][0, 0, 0, :, sl], state[d, h].astype(BF16), preferred_element_type=F32)
    for d, h in chains:
        sl = slice(h * DN_D, (h + 1) * DN_D)
        v_new[d, h] = (ins[d][0][0, 0, :, sl] - wq[d, h][:CHUNK]).astype(BF16)
    for d, h in chains:
        sl = slice(h * DN_D, (h + 1) * DN_D)
        qk = ins[d][3][0, 0, :, h * CHUNK:(h + 1) * CHUNK]
        outs[d][0, :, sl] = wq[d, h][CHUNK:] + jnp.dot(qk, v_new[d, h], preferred_element_type=F32)
    for d, h in chains:
        sl = slice(h * DN_D, (h + 1) * DN_D)
        ktt = ins[d][2][0, 0, 0, :, h * CHUNK:(h + 1) * CHUNK]
        state_ref[d, h] = (state[d, h] * ins[d][4][0, 0, 0:1, sl]
                           + jnp.dot(ktt, v_new[d, h], preferred_element_type=F32))


def dn_rec(u, wq, ktt, qk, egl):
    _, b, s, w = u.shape
    nchunk = s // CHUNK

    def specs(d):
        def cidx(n):
            return n if d == 0 else nchunk - 1 - n
        return [pl.BlockSpec((1, 1, CHUNK, w), lambda bb, n: (d, bb, cidx(n), 0)),
                pl.BlockSpec((1, 1, 1, 2 * CHUNK, w), lambda bb, n: (d, bb, cidx(n), 0, 0)),
                pl.BlockSpec((1, 1, 1, DN_D, DN_HEADS * CHUNK), lambda bb, n: (d, bb, cidx(n), 0, 0)),
                pl.BlockSpec((1, 1, CHUNK, DN_HEADS * CHUNK), lambda bb, n: (d, bb, cidx(n), 0)),
                pl.BlockSpec((1, 1, CHUNK, w), lambda bb, n: (d, bb, cidx(n), 0))]

    out = jax.ShapeDtypeStruct((b, s, w), F32)
    return pl.pallas_call(
        _dn_rec_body,
        out_shape=(out, out),
        grid=(b, nchunk),
        in_specs=specs(0) + specs(1),
        out_specs=(pl.BlockSpec((1, CHUNK, w), lambda bb, n: (bb, n, 0)),
                   pl.BlockSpec((1, CHUNK, w), lambda bb, n: (bb, nchunk - 1 - n, 0))),
        scratch_shapes=[pltpu.VMEM((2, DN_HEADS, DN_D, DN_D), F32)],
        compiler_params=_cparams(("arbitrary", "arbitrary")),
        name="dn_rec",
    )(u, wq, ktt, qk, egl, u, wq, ktt, qk, egl)


def _rope_head(x, g, cos, sin_signed, first_half):
    xf = x * lax.rsqrt(jnp.mean(x * x, axis=-1, keepdims=True) + EPS) * g
    partner = jnp.where(first_half, pltpu.roll(xf, LANES - ROPE_SUB // 2, axis=1),
                        pltpu.roll(xf, ROPE_SUB // 2, axis=1))
    return xf * cos + partner * sin_signed


def _att_prep_body(q_ref, k_ref, v_ref, cos_ref, sin_ref, qg_ref, kg_ref, qo_ref, ko_ref, vo_ref):
    cos = cos_ref[...]
    sin_signed = sin_ref[...]
    lane = lax.broadcasted_iota(jnp.int32, cos.shape, 1)
    first_half = (lane % ROPE_SUB) < ROPE_SUB // 2
    scale = ATT_DH ** -0.5 * LOG2E
    for h in range(ATT_HEADS):
        sl = slice(h * ATT_DH, (h + 1) * ATT_DH)
        qo_ref[0, :, sl] = (_rope_head(q_ref[0, :, sl], qg_ref[...], cos, sin_signed, first_half)
                            * scale).astype(BF16)
    for h in range(ATT_KV_HEADS):
        sl = slice(h * ATT_DH, (h + 1) * ATT_DH)
        ko_ref[0, :, sl] = _rope_head(k_ref[0, :, sl], kg_ref[...], cos, sin_signed, first_half).astype(BF16)
    vo_ref[0] = v_ref[0].astype(BF16)


def att_prep(proj, cos, sin_signed, qg, kg, ts=512):
    b, s, _ = proj.shape
    ts = _tile(s, ts, 16)
    tab = pl.BlockSpec((ts, ATT_DH), lambda bb, i: (i, 0))
    vec = pl.BlockSpec((1, ATT_DH), lambda bb, i: (0, 0))
    return pl.pallas_call(
        _att_prep_body,
        out_shape=(jax.ShapeDtypeStruct((b, s, ATT_W), BF16),
                   jax.ShapeDtypeStruct((b, s, ATT_KV_W), BF16),
                   jax.ShapeDtypeStruct((b, s, ATT_KV_W), BF16)),
        grid=(b, s // ts),
        in_specs=[pl.BlockSpec((1, ts, ATT_W), lambda bb, i: (bb, i, OFF_AQ // ATT_W)),
                  pl.BlockSpec((1, ts, ATT_KV_W), lambda bb, i: (bb, i, OFF_AK // ATT_KV_W)),
                  pl.BlockSpec((1, ts, ATT_KV_W), lambda bb, i: (bb, i, OFF_AV // ATT_KV_W)),
                  tab, tab, vec, vec],
        out_specs=(pl.BlockSpec((1, ts, ATT_W), lambda bb, i: (bb, i, 0)),
                   pl.BlockSpec((1, ts, ATT_KV_W), lambda bb, i: (bb, i, 0)),
                   pl.BlockSpec((1, ts, ATT_KV_W), lambda bb, i: (bb, i, 0))),
        compiler_params=_cparams(("arbitrary", "arbitrary")),
        name="att_prep",
    )(proj, proj, proj, cos, sin_signed, qg.reshape(1, ATT_DH), kg.reshape(1, ATT_DH))


def _flash_body(q_ref, k_ref, v_ref, o_ref, *, tk, nkv):
    tq = q_ref.shape[1]
    heads = range(ATT_GROUP)
    qs = [q_ref[0, :, h * ATT_DH:(h + 1) * ATT_DH] for h in heads]

    def step(t, carry):
        start = pl.multiple_of(t * tk, tk)
        ks = k_ref[0, pl.ds(start, tk), :]
        vs = v_ref[0, pl.ds(start, tk), :]
        new = []
        for h in heads:
            m, l, acc = carry[h]
            sc = lax.dot_general(qs[h], ks, (((1,), (1,)), ((), ())), preferred_element_type=F32)
            m_new = jnp.maximum(m, jnp.max(sc, axis=-1, keepdims=True))
            p = jnp.exp2(sc - m_new)
            corr = jnp.exp2(m - m_new)
            l = corr * l + jnp.sum(p, axis=-1, keepdims=True)
            acc = corr * acc + jnp.dot(p.astype(BF16), vs, preferred_element_type=F32)
            new.append((m_new, l, acc))
        return tuple(new)

    init = tuple((jnp.full((tq, 1), -1e30, F32), jnp.zeros((tq, 1), F32), jnp.zeros((tq, ATT_DH), F32))
                 for _ in heads)
    res = lax.fori_loop(0, nkv, step, init, unroll=2 if nkv % 2 == 0 else 1)
    for h in heads:
        _, l, acc = res[h]
        o_ref[0, :, h * ATT_DH:(h + 1) * ATT_DH] = (acc / l).astype(o_ref.dtype)


def flash_attention(q, k, v, tq=512, tk=512):
    b, s, _ = q.shape
    tq = _tile(s, tq, 16)
    tk = _tile(s, tk, 16)
    gw = ATT_GROUP * ATT_DH
    return pl.pallas_call(
        functools.partial(_flash_body, tk=tk, nkv=s // tk),
        out_shape=jax.ShapeDtypeStruct((b, s, ATT_W), BF16),
        grid=(b, ATT_KV_HEADS, s // tq),
        in_specs=[pl.BlockSpec((1, tq, gw), lambda bb, g, i: (bb, i, g)),
                  pl.BlockSpec((1, s, ATT_DH), lambda bb, g, i: (bb, 0, g)),
                  pl.BlockSpec((1, s, ATT_DH), lambda bb, g, i: (bb, 0, g))],
        out_specs=pl.BlockSpec((1, tq, gw), lambda bb, g, i: (bb, i, g)),
        compiler_params=_cparams(("arbitrary", "arbitrary", "arbitrary")),
        name="flash_attention",
    )(q, k, v)


def _gla_chunk(qs, k, v, gk, state_ref, d, reverse):
    c = CHUNK
    nsub = c // SUB
    ri = lax.broadcasted_iota(jnp.int32, (c, c), 0)
    ci = lax.broadcasted_iota(jnp.int32, (c, c), 1)
    incl = (ci >= ri) if reverse else (ri >= ci)
    gc = _hdot(incl.astype(F32), gk)
    last = 0 if reverse else c - 1
    gl = gc[last:last + 1, :]
    qd = qs * jnp.exp(gc)
    kt = k * jnp.exp(gl - gc)
    egl = jnp.exp(gl)
    row = lax.broadcasted_iota(jnp.int32, (c, GLA_QK), 0)

    a_off = [[] for _ in range(GLA_HEADS)]
    for i in range(nsub):
        if i == (nsub - 1 if reverse else 0):
            for h in range(GLA_HEADS):
                a_off[h].append(jnp.zeros((SUB, c), F32))
            continue
        first = (i + 1) * SUB - 1 if reverse else i * SUB
        rs = slice(i * SUB, (i + 1) * SUB)
        ref = gc[first:first + 1, :]
        qi = qs[rs, :] * jnp.exp(gc[rs, :] - ref)
        early = (row > first) if reverse else (row < first)
        kf = k * jnp.where(early, jnp.exp(jnp.where(early, ref - gc, 0.0)), 0.0)
        for h in range(GLA_HEADS):
            sl = slice(h * GLA_DK, (h + 1) * GLA_DK)
            a_off[h].append(_bdot_nt(qi[:, sl], kf[:, sl]))

    lane = lax.broadcasted_iota(jnp.int32, (LANES, 2 * GLA_DV), 0)
    col = lax.broadcasted_iota(jnp.int32, (LANES, 2 * GLA_DV), 1)
    head_sum = ((lane // GLA_DK) == (col // GLA_DV)).astype(BF16)
    sub_row = lax.broadcasted_iota(jnp.int32, (SUB, LANES), 0)
    o_diag = []
    for p in range(GLA_HEADS // 2):
        ls = slice(p * LANES, (p + 1) * LANES)
        vs = slice(p * 2 * GLA_DV, (p + 1) * 2 * GLA_DV)
        blocks = []
        for i in range(nsub):
            rs = slice(i * SUB, (i + 1) * SUB)
            gci, qsi = gc[rs, ls], qs[rs, ls]
            terms = []
            for jj in range(SUB):
                j = i * SUB + jj
                later = (sub_row <= jj) if reverse else (sub_row >= jj)
                e = jnp.where(later, jnp.exp(jnp.where(later, gci - gc[j:j + 1, ls], 0.0)), 0.0)
                terms.append(qsi * k[j:j + 1, ls] * e)
            r = jnp.dot(jnp.concatenate(terms, axis=0).astype(BF16), head_sum,
                        preferred_element_type=F32)
            acc = r[0:SUB, :] * v[i * SUB:i * SUB + 1, vs]
            for jj in range(1, SUB):
                acc = acc + r[jj * SUB:(jj + 1) * SUB, :] * v[i * SUB + jj:i * SUB + jj + 1, vs]
            blocks.append(acc)
        o_diag.append(jnp.concatenate(blocks, axis=0))

    outs = []
    for h in range(GLA_HEADS):
        sl = slice(h * GLA_DK, (h + 1) * GLA_DK)
        vh = v[:, h * GLA_DV:(h + 1) * GLA_DV]
        st = state_ref[d, h]
        a = jnp.concatenate(a_off[h], axis=0)
        od = o_diag[h // 2][:, (h % 2) * GLA_DV:(h % 2 + 1) * GLA_DV]
        outs.append(_bdot_nt(qd[:, sl], st) + _bdot(a, vh) + od)
        state_ref[d, h] = st * egl[:, sl] + lax.dot_general(
            vh.astype(BF16), kt[:, sl].astype(BF16), (((0,), (0,)), ((), ())), preferred_element_type=F32)
    return jnp.concatenate(outs, axis=1)


def _gla_scan_body(qf_ref, kf_ref, vf_ref, smf_ref, qb_ref, kb_ref, vb_ref, smb_ref, up_ref, upb_ref,
                   of_ref, ob_ref, state_ref):
    n = pl.program_id(1)

    @pl.when(n == 0)
    def _():
        state_ref[...] = jnp.zeros_like(state_ref)

    ins = ((qf_ref, kf_ref, vf_ref, smf_ref, of_ref), (qb_ref, kb_ref, vb_ref, smb_ref, ob_ref))
    for d, (q_ref, k_ref, v_ref, sm_ref, o_ref) in enumerate(ins):
        gk = jax.nn.log_sigmoid(_hdot(sm_ref[0], up_ref[d]) + upb_ref[d]) * (1.0 / GLA_NORMALIZER)
        qs = q_ref[0] * (GLA_DK ** -0.5)
        o_ref[0] = _gla_chunk(qs, k_ref[0], v_ref[0], gk, state_ref, d, reverse=(d == 1))


def gla_scan(proj, up_pad, upb):
    b, s, _ = proj.shape
    nchunk = s // CHUNK

    def specs(d):
        def cidx(n):
            return n if d == 0 else nchunk - 1 - n
        return [pl.BlockSpec((1, CHUNK, GLA_QK), lambda bb, n: (bb, cidx(n), OFF_GQ // GLA_QK)),
                pl.BlockSpec((1, CHUNK, GLA_QK), lambda bb, n: (bb, cidx(n), OFF_GK // GLA_QK)),
                pl.BlockSpec((1, CHUNK, GLA_W), lambda bb, n: (bb, cidx(n), OFF_GV // GLA_W)),
                pl.BlockSpec((1, CHUNK, LANES), lambda bb, n: (bb, cidx(n), OFF_SMALL // LANES))]

    out = jax.ShapeDtypeStruct((b, s, GLA_W), F32)
    return pl.pallas_call(
        _gla_scan_body,
        out_shape=(out, out),
        grid=(b, nchunk),
        in_specs=specs(0) + specs(1) + [pl.BlockSpec((2, LANES, GLA_QK), lambda bb, n: (0, 0, 0)),
                                        pl.BlockSpec((2, 1, GLA_QK), lambda bb, n: (0, 0, 0))],
        out_specs=(pl.BlockSpec((1, CHUNK, GLA_W), lambda bb, n: (bb, n, 0)),
                   pl.BlockSpec((1, CHUNK, GLA_W), lambda bb, n: (bb, nchunk - 1 - n, 0))),
        scratch_shapes=[pltpu.VMEM((2, GLA_HEADS, GLA_DV, GLA_DK), F32)],
        compiler_params=_cparams(("arbitrary", "arbitrary")),
        name="gla_scan",
    )(proj, proj, proj, proj, proj, proj, proj, proj, up_pad, upb)


def _gated_norm(o, gate, g, heads, d):
    outs = []
    for h in range(heads):
        sl = slice(h * d, (h + 1) * d)
        t = o[:, sl]
        y = t * lax.rsqrt(jnp.mean(t * t, axis=-1, keepdims=True) + EPS) * g
        gt = gate[:, sl]
        outs.append(y * (gt * jax.nn.sigmoid(gt)))
    return jnp.concatenate(outs, axis=1)


def _merge_body(dnf_ref, dnb_ref, dgate_ref, dng_ref, att_ref, glf_ref, glb_ref, ggate_ref, glg_ref, o_ref):
    o_dn = _gated_norm(dnf_ref[0] + dnb_ref[0], dgate_ref[0], dng_ref[...], DN_HEADS, DN_D)
    o_gla = _gated_norm(glf_ref[0] + glb_ref[0], ggate_ref[0], glg_ref[...], GLA_HEADS, GLA_DV)
    o_ref[0, :, 0:DN_W] = o_dn.astype(BF16)
    o_ref[0, :, DN_W:DN_W + ATT_W] = att_ref[0]
    o_ref[0, :, DN_W + ATT_W:] = o_gla.astype(BF16)


def merge_heads(dn_f, dn_b, proj, dn_g, att, gl_f, gl_b, gla_g, ts=512):
    b, s, _ = proj.shape
    ts = _tile(s, ts, 16)
    mix_w = DN_W + ATT_W + GLA_W

    def spec(w, blk=0):
        return pl.BlockSpec((1, ts, w), lambda bb, i: (bb, i, blk))

    vec = pl.BlockSpec((1, LANES), lambda bb, i: (0, 0))
    return pl.pallas_call(
        _merge_body,
        out_shape=jax.ShapeDtypeStruct((b, s, mix_w), BF16),
        grid=(b, s // ts),
        in_specs=[spec(DN_W), spec(DN_W), spec(DN_W, OFF_DGATE // DN_W), vec,
                  spec(ATT_W), spec(GLA_W), spec(GLA_W), spec(GLA_W, OFF_GGATE // GLA_W), vec],
        out_specs=spec(mix_w),
        compiler_params=_cparams(("arbitrary", "arbitrary")),
        name="merge_heads",
    )(dn_f, dn_b, proj, dn_g.reshape(1, DN_D), att, gl_f, gl_b, proj, gla_g.reshape(1, GLA_DV))


def _split_points():
    pts, acc = [], 0
    for sz in IN_SIZES[:-1]:
        acc += sz
        pts.append(acc)
    return pts


def _relayout_w_in(w):
    d = w.shape[0]
    (dq, dk, dv, dgate, a_f, a_b, b_f, b_b, aq, ak, av, gq, gkk, gv, ggate, lr_f, lr_b) = jnp.split(
        w, _split_points(), axis=1)
    small = jnp.concatenate([a_f, a_b, b_f, b_b, lr_f, lr_b], axis=1)
    small = jnp.pad(small, ((0, 0), (0, LANES - small.shape[1])))
    cols = jnp.concatenate([dq, dk, dv, dgate, aq, ak, av, gq, gv, ggate, gkk, small,
                            jnp.zeros((d, PROJ_COLS - OFF_SMALL - LANES), w.dtype)], axis=1)
    return cols.astype(BF16)[None, None]


def _rope_tables(s):
    rows = s // GRID_W
    row = jnp.repeat(jnp.arange(rows, dtype=jnp.int32), GRID_W).astype(F32)
    col = jnp.tile(jnp.arange(GRID_W, dtype=jnp.int32), rows).astype(F32)
    inv_freq = ROPE_THETA ** (-jnp.arange(0, ROPE_SUB, 2, dtype=F32) / ROPE_SUB)
    ang_r = row[:, None] * inv_freq[None, :]
    ang_c = col[:, None] * inv_freq[None, :]
    cos = jnp.concatenate([jnp.cos(ang_r), jnp.cos(ang_r), jnp.cos(ang_c), jnp.cos(ang_c)], axis=1)
    sin = jnp.concatenate([-jnp.sin(ang_r), jnp.sin(ang_r), -jnp.sin(ang_c), jnp.sin(ang_c)], axis=1)
    return cos, sin


def _dn_params(dn_a_log, dn_dt_bias):
    rows = jnp.stack([dn_a_log.reshape(-1), dn_dt_bias.reshape(-1)], axis=0)
    return jnp.pad(rows, ((0, 0), (SM_A, LANES - SM_A - 2 * DN_HEADS)))


def _gla_params(gla_up, gla_up_b):
    ups = []
    for d in range(2):
        lo = SM_LR + d * GLA_RANK
        ups.append(jnp.pad(gla_up[d], ((lo, LANES - lo - GLA_RANK), (0, 0))))
    return jnp.stack(ups, axis=0), gla_up_b[:, None, :]


def _mixer(x16, bsz, s, w_in, dn_conv, dn_a_log, dn_dt_bias, dn_norm_g, att_qn_g, att_kn_g,
           gla_up, gla_up_b, gla_norm_g, rope):
    t = bsz * s
    bm = _tile(t, 512, 16)
    proj = grouped_matmul(x16, _relayout_w_in(w_in), 0, jnp.zeros((t // bm,), jnp.int32), bm, 1024, F32)
    proj = proj.reshape(bsz, s, PROJ_COLS)

    q, k, v = dn_prep(proj, dn_conv)
    u, wq, ktt, qk, egl = dn_wy(q, k, v, proj, _dn_params(dn_a_log, dn_dt_bias))
    dn_f, dn_b = dn_rec(u, wq, ktt, qk, egl)

    cos, sin_signed = rope
    aq, ak, av = att_prep(proj, cos, sin_signed, att_qn_g, att_kn_g)
    o_att = flash_attention(aq, ak, av)

    gl_f, gl_b = gla_scan(proj, *_gla_params(gla_up, gla_up_b))

    return merge_heads(dn_f, dn_b, proj, dn_norm_g, o_att, gl_f, gl_b, gla_norm_g).reshape(t, -1)


def _moe(x32, x16, router_w, w_gate, w_up, w_down, layer, ln_g, ln_b, alpha):
    t, d = x32.shape
    rw = jnp.pad(router_w, ((0, 0), (0, LANES - N_EXPERTS)))
    logits = router_logits(x32, rw)[:, :N_EXPERTS]
    top_val, top_idx = lax.top_k(logits, TOP_K)
    gates = jax.nn.softmax(top_val, axis=-1)
    e_flat = top_idx.reshape(-1).astype(jnp.int32)
    tok_flat = jnp.repeat(jnp.arange(t, dtype=jnp.int32), TOP_K)
    onehot = (e_flat[:, None] == jnp.arange(N_EXPERTS, dtype=jnp.int32)[None, :]).astype(jnp.int32)
    csum = jnp.cumsum(onehot, axis=0)
    counts = csum[-1]
    rank = jnp.take_along_axis(csum, e_flat[:, None], axis=1)[:, 0] - 1
    padded = (counts + MOE_BLOCK - 1) // MOE_BLOCK * MOE_BLOCK
    pstart = jnp.cumsum(padded) - padded
    pend = pstart + padded
    dest = pstart[e_flat] + rank
    nb = -(-(TOP_K * t) // MOE_BLOCK) + N_EXPERTS
    cap = nb * MOE_BLOCK
    buf_tok = jnp.zeros((cap,), jnp.int32).at[dest].set(tok_flat)
    block_start = jnp.arange(nb, dtype=jnp.int32) * MOE_BLOCK
    block_exp = jnp.minimum(jnp.sum(block_start[:, None] >= pend[None, :], axis=-1),
                            N_EXPERTS - 1).astype(jnp.int32)
    xb = x16[buf_tok]
    h = grouped_swiglu(xb, w_gate, w_up, layer, block_exp, MOE_BLOCK, 512)
    yb = grouped_matmul(h, w_down, layer, block_exp, MOE_BLOCK, 512, F32)
    y = (yb[dest] * gates.reshape(-1)[:, None]).reshape(t, TOP_K, d).sum(axis=1)
    return add_ln(x32, y, ln_g, ln_b, alpha)


def _dense_ffn(x32, x16, w_gate, w_up, w_down, layer, ln_g, ln_b, alpha):
    t = x32.shape[0]
    bm = _tile(t, 512, 16)
    grp = jnp.zeros((t // bm,), jnp.int32)
    h = grouped_swiglu(x16, w_gate[:, None], w_up[:, None], layer, grp, bm, 512)
    return matmul_res_ln(h, w_down[layer].astype(BF16), x32, ln_g, ln_b, alpha)


def kernel(x, w_in, dn_conv, dn_a_log, dn_dt_bias, dn_norm_g, att_qn_g, att_kn_g, gla_up, gla_up_b,
           gla_norm_g, w_out, ln1_g, ln1_b, ln2_g, ln2_b, ffn_w_gate, ffn_w_up, ffn_w_down, router_w,
           exp_w_gate, exp_w_up, exp_w_down):
    bsz, s, d = x.shape
    depth = w_in.shape[0]
    alpha = (2.0 * depth) ** 0.25
    t = bsz * s
    rope = _rope_tables(s)
    x32 = x.reshape(t, d)
    x16 = x32.astype(BF16)
    for layer in range(depth):
        mix = _mixer(x16, bsz, s, w_in[layer], dn_conv[layer], dn_a_log[layer], dn_dt_bias[layer],
                     dn_norm_g[layer], att_qn_g[layer], att_kn_g[layer], gla_up[layer],
                     gla_up_b[layer], gla_norm_g[layer], rope)
        x32, x16 = matmul_res_ln(mix, w_out[layer].astype(BF16), x32, ln1_g[layer], ln1_b[layer], alpha)
        j = layer // 2
        if layer % 2 == 0:
            x32, x16 = _dense_ffn(x32, x16, ffn_w_gate, ffn_w_up, ffn_w_down, j,
                                  ln2_g[layer], ln2_b[layer], alpha)
        else:
            x32, x16 = _moe(x32, x16, router_w[j], exp_w_gate, exp_w_up, exp_w_down, j,
                            ln2_g[layer], ln2_b[layer], alpha)
    return x32.reshape(bsz, s, d)
```

```python
import functools
import math

import jax
import jax.numpy as jnp
from jax import lax
from jax.experimental import pallas as pl
from jax.experimental.pallas import tpu as pltpu

F32 = jnp.float32
BF16 = jnp.bfloat16
HIGHEST = lax.Precision.HIGHEST

DN_HEADS, DN_D = 6, 128
ATT_HEADS, ATT_KV_HEADS, ATT_DH = 6, 2, 128
ATT_GROUP = ATT_HEADS // ATT_KV_HEADS
ROPE_SUB, ROPE_THETA, GRID_W = 64, 10000.0, 64
GLA_HEADS, GLA_DK, GLA_DV, GLA_RANK = 4, 64, 128, 16
GLA_NORMALIZER = 16.0
CHUNK = 64
SUB = 16
WY_ROWS = 2 * CHUNK
CONV_W = 5
N_EXPERTS, TOP_K, MOE_BLOCK = 8, 2, 256
EPS = 1e-6
LOG2E = 1.4426950408889634
LANES = 128
HALO = 8

DN_W = DN_HEADS * DN_D
ATT_W = ATT_HEADS * ATT_DH
ATT_KV_W = ATT_KV_HEADS * ATT_DH
GLA_QK = GLA_HEADS * GLA_DK
GLA_W = GLA_HEADS * GLA_DV
IN_SIZES = (DN_W, DN_W, DN_W, DN_W, DN_HEADS, DN_HEADS, DN_HEADS, DN_HEADS,
            ATT_W, ATT_KV_W, ATT_KV_W, GLA_QK, GLA_QK, GLA_W, GLA_W, GLA_RANK, GLA_RANK)
OFF_DQ, OFF_DGATE, OFF_AQ, OFF_AK, OFF_AV = 0, 2304, 3072, 3840, 4096
OFF_GQ, OFF_GV, OFF_GGATE, OFF_GK, OFF_SMALL = 4352, 4608, 5120, 5632, 5888
PROJ_COLS = 6144
SM_A, SM_B, SM_LR = 0, 2 * DN_HEADS, 4 * DN_HEADS

VMEM_LIMIT = 56 * 1024 * 1024


def _cparams(sem, vmem=VMEM_LIMIT):
    return pltpu.CompilerParams(dimension_semantics=sem, vmem_limit_bytes=vmem)


def _tile(n, pref, quantum=LANES):
    if n <= pref:
        return n
    t = pref - pref % quantum
    while n % t:
        t -= quantum
    return t


def _bdot(a, b):
    return jnp.dot(a.astype(BF16), b.astype(BF16), preferred_element_type=F32)


def _bdot_nt(a, b):
    return lax.dot_general(a.astype(BF16), b.astype(BF16), (((1,), (1,)), ((), ())),
                           preferred_element_type=F32)


def _hdot(a, b):
    return jnp.dot(a, b, preferred_element_type=F32, precision=HIGHEST)


def _group_changed(be_ref):
    i = pl.program_id(1)
    return (i == 0) | (be_ref[i] != be_ref[jnp.maximum(i - 1, 0)])


def _gmm_body(be_ref, x_ref, w_ref, o_ref, *w16):
    if w16:
        @pl.when(_group_changed(be_ref))
        def _():
            w16[0][...] = w_ref[...].astype(BF16)
        w = w16[0][...]
    else:
        w = w_ref[...]
    o_ref[...] = jnp.dot(x_ref[...], w, preferred_element_type=F32).astype(o_ref.dtype)


def _weight_spec(w, layer, k, tn):
    if w.ndim == 3:
        return pl.BlockSpec((None, k, tn), lambda j, i, be: (layer, 0, j))
    return pl.BlockSpec((None, None, k, tn), lambda j, i, be: (layer, be[i], 0, j))


def grouped_matmul(x, w, layer, block_group, bm, tn, out_dtype):
    m, k = x.shape
    n = w.shape[-1]
    tn = _tile(n, tn)
    scratch = [pltpu.VMEM((k, tn), BF16)] if w.dtype != BF16 else []
    return pl.pallas_call(
        _gmm_body,
        out_shape=jax.ShapeDtypeStruct((m, n), out_dtype),
        grid_spec=pltpu.PrefetchScalarGridSpec(
            num_scalar_prefetch=1, grid=(n // tn, m // bm),
            in_specs=[pl.BlockSpec((bm, k), lambda j, i, be: (i, 0)), _weight_spec(w, layer, k, tn)],
            out_specs=pl.BlockSpec((bm, tn), lambda j, i, be: (i, j)),
            scratch_shapes=scratch),
        compiler_params=_cparams(("arbitrary", "arbitrary")),
        name="grouped_matmul",
    )(block_group, x, w)


def _gswiglu_body(be_ref, x_ref, wg_ref, wu_ref, o_ref, wg16, wu16):
    @pl.when(_group_changed(be_ref))
    def _():
        wg16[...] = wg_ref[...].astype(BF16)
        wu16[...] = wu_ref[...].astype(BF16)

    x = x_ref[...]
    g = jnp.dot(x, wg16[...], preferred_element_type=F32)
    u = jnp.dot(x, wu16[...], preferred_element_type=F32)
    o_ref[...] = (g * jax.nn.sigmoid(g) * u).astype(o_ref.dtype)


def grouped_swiglu(x, wg, wu, layer, block_group, bm, tn):
    m, k = x.shape
    n = wg.shape[-1]
    tn = _tile(n, tn)
    return pl.pallas_call(
        _gswiglu_body,
        out_shape=jax.ShapeDtypeStruct((m, n), BF16),
        grid_spec=pltpu.PrefetchScalarGridSpec(
            num_scalar_prefetch=1, grid=(n // tn, m // bm),
            in_specs=[pl.BlockSpec((bm, k), lambda j, i, be: (i, 0)),
                      _weight_spec(wg, layer, k, tn), _weight_spec(wu, layer, k, tn)],
            out_specs=pl.BlockSpec((bm, tn), lambda j, i, be: (i, j)),
            scratch_shapes=[pltpu.VMEM((k, tn), BF16), pltpu.VMEM((k, tn), BF16)]),
        compiler_params=_cparams(("arbitrary", "arbitrary")),
        name="grouped_swiglu",
    )(block_group, x, wg, wu)


def _res_ln(res, y, g, b, alpha):
    z = alpha * res + y
    mu = jnp.mean(z, axis=-1, keepdims=True)
    zc = z - mu
    var = jnp.mean(zc * zc, axis=-1, keepdims=True)
    return zc * lax.rsqrt(var + EPS) * g + b


def _mm_res_ln_body(x_ref, w_ref, res_ref, g_ref, b_ref, o32_ref, o16_ref, *acc, nk, alpha):
    kk = pl.program_id(1)
    part = jnp.dot(x_ref[...], w_ref[...], preferred_element_type=F32)

    def finish(y):
        out = _res_ln(res_ref[...], y, g_ref[...], b_ref[...], alpha)
        o32_ref[...] = out
        o16_ref[...] = out.astype(BF16)

    if nk == 1:
        finish(part)
        return
    acc_ref = acc[0]

    @pl.when(kk == 0)
    def _():
        acc_ref[...] = part

    @pl.when((kk > 0) & (kk < nk - 1))
    def _():
        acc_ref[...] += part

    @pl.when(kk == nk - 1)
    def _():
        finish(acc_ref[...] + part)


def matmul_res_ln(x, w, res, g, b, alpha, tm=512, tk=2048):
    m, k = x.shape
    n = w.shape[-1]
    tm = _tile(m, tm, 8)
    tk = _tile(k, tk)
    nk = k // tk
    return pl.pallas_call(
        functools.partial(_mm_res_ln_body, nk=nk, alpha=alpha),
        out_shape=(jax.ShapeDtypeStruct((m, n), F32), jax.ShapeDtypeStruct((m, n), BF16)),
        grid=(m // tm, nk),
        in_specs=[pl.BlockSpec((tm, tk), lambda i, kk: (i, kk)),
                  pl.BlockSpec((tk, n), lambda i, kk: (kk, 0)),
                  pl.BlockSpec((tm, n), lambda i, kk: (i, 0)),
                  pl.BlockSpec((1, n), lambda i, kk: (0, 0)),
                  pl.BlockSpec((1, n), lambda i, kk: (0, 0))],
        out_specs=(pl.BlockSpec((tm, n), lambda i, kk: (i, 0)),
                   pl.BlockSpec((tm, n), lambda i, kk: (i, 0))),
        scratch_shapes=[pltpu.VMEM((tm, n), F32)] if nk > 1 else [],
        compiler_params=_cparams(("arbitrary", "arbitrary")),
        name="matmul_res_ln",
    )(x, w, res, g.reshape(1, n), b.reshape(1, n))


def _combine_ln_body(res_ref, y_ref, gate_ref, g_ref, b_ref, o32_ref, o16_ref, *, alpha):
    n = res_ref.shape[-1]
    gate = gate_ref[...]
    y = y_ref[:, 0:n].astype(F32) * gate[:, 0:1]
    for kk in range(1, TOP_K):
        y = y + y_ref[:, kk * n:(kk + 1) * n].astype(F32) * gate[:, kk:kk + 1]
    out = _res_ln(res_ref[...], y, g_ref[...], b_ref[...], alpha)
    o32_ref[...] = out
    o16_ref[...] = out.astype(BF16)


def combine_ln(res, y, gates, g, b, alpha, tm=512):
    m, n = res.shape
    tm = _tile(m, tm, 16)
    row = pl.BlockSpec((tm, n), lambda i: (i, 0))
    vec = pl.BlockSpec((1, n), lambda i: (0, 0))
    return pl.pallas_call(
        functools.partial(_combine_ln_body, alpha=alpha),
        out_shape=(jax.ShapeDtypeStruct((m, n), F32), jax.ShapeDtypeStruct((m, n), BF16)),
        grid=(m // tm,),
        in_specs=[row, pl.BlockSpec((tm, TOP_K * n), lambda i: (i, 0)),
                  pl.BlockSpec((tm, TOP_K), lambda i: (i, 0)), vec, vec],
        out_specs=(row, row),
        compiler_params=_cparams(("arbitrary",)),
        name="combine_ln",
    )(res, y, gates, g.reshape(1, n), b.reshape(1, n))


def _router_body(x_ref, w_ref, o_ref):
    o_ref[...] = _hdot(x_ref[...], w_ref[...])


def router_logits(x, w_pad, tm=512):
    m, k = x.shape
    n = w_pad.shape[-1]
    tm = _tile(m, tm, 8)
    return pl.pallas_call(
        _router_body,
        out_shape=jax.ShapeDtypeStruct((m, n), F32),
        grid=(m // tm,),
        in_specs=[pl.BlockSpec((tm, k), lambda i: (i, 0)), pl.BlockSpec((k, n), lambda i: (0, 0))],
        out_specs=pl.BlockSpec((tm, n), lambda i: (i, 0)),
        compiler_params=_cparams(("arbitrary",)),
        name="router_logits",
    )(x, w_pad)


def _dn_prep_body(cur_ref, prev_ref, next_ref, w_ref, q_ref, k_ref, v_ref, buf_ref, *, ts, nblk):
    i = pl.program_id(1)
    buf_ref[0:HALO, :] = jnp.where(i > 0, prev_ref[0], 0.0)
    buf_ref[HALO:HALO + ts, :] = cur_ref[0]
    buf_ref[HALO + ts:2 * HALO + ts, :] = jnp.where(i < nblk - 1, next_ref[0], 0.0)
    base = HALO - CONV_W // 2
    acc = buf_ref[base:base + ts, :] * w_ref[0:1, :]
    for j in range(1, CONV_W):
        acc = acc + buf_ref[base + j:base + j + ts, :] * w_ref[j:j + 1, :]
    y = acc * jax.nn.sigmoid(acc)
    for h in range(DN_HEADS):
        for off, ref in ((0, q_ref), (DN_W, k_ref)):
            t = y[:, off + h * DN_D:off + (h + 1) * DN_D]
            ref[0, :, h * DN_D:(h + 1) * DN_D] = t * lax.rsqrt(jnp.sum(t * t, axis=-1, keepdims=True) + EPS)
    v_ref[0] = y[:, 2 * DN_W:3 * DN_W]


def dn_prep(proj, conv_w, ts=256):
    b, s, _ = proj.shape
    ts = _tile(s, ts, HALO)
    nblk = s // ts
    c = 3 * DN_W
    hb = ts // HALO
    out = jax.ShapeDtypeStruct((b, s, DN_W), F32)
    ospec = pl.BlockSpec((1, ts, DN_W), lambda bb, i: (bb, i, 0))
    return pl.pallas_call(
        functools.partial(_dn_prep_body, ts=ts, nblk=nblk),
        out_shape=(out, out, out),
        grid=(b, nblk),
        in_specs=[pl.BlockSpec((1, ts, c), lambda bb, i: (bb, i, 0)),
                  pl.BlockSpec((1, HALO, c), lambda bb, i: (bb, jnp.maximum(i * hb - 1, 0), 0)),
                  pl.BlockSpec((1, HALO, c), lambda bb, i: (bb, jnp.minimum((i + 1) * hb, nblk * hb - 1), 0)),
                  pl.BlockSpec((CONV_W, c), lambda bb, i: (0, 0))],
        out_specs=(ospec, ospec, ospec),
        scratch_shapes=[pltpu.VMEM((ts + 2 * HALO, c), F32)],
        compiler_params=_cparams(("arbitrary", "arbitrary")),
        name="dn_prep",
    )(proj, proj, proj, conv_w)


def _dn_wy_body(q_ref, k_ref, v_ref, sm_ref, par_ref, u_ref, wq_ref, ktt_ref, qk_ref, egl_ref):
    r = WY_ROWS
    nck = r // CHUNK
    sm = sm_ref[0]
    par = par_ref[...]
    g_all = -jnp.exp(par[0:1, :]) * jax.nn.softplus(sm + par[1:2, :])
    beta_all = jax.nn.sigmoid(sm)
    ri = lax.broadcasted_iota(jnp.int32, (r, r), 0)
    ci = lax.broadcasted_iota(jnp.int32, (r, r), 1)
    same = (ri // CHUNK) == (ci // CHUNK)
    eye = (ri == ci).astype(F32)
    gtot = _hdot(same.astype(F32), g_all)
    masks, gcs = [], []
    for d in range(2):
        delta = ri - ci if d == 0 else ci - ri
        incl = same & (delta >= 0)
        strict = same & (delta > 0)
        gc = _hdot(incl.astype(F32), g_all)
        masks.append((incl, strict))
        gcs.append((gc, gc.T))
    scale = DN_D ** -0.5
    chains = [(h, d) for h in range(DN_HEADS) for d in range(2)]
    qk_cols = ([], [])
    nm, tinv, rhs = {}, {}, {}
    for h in range(DN_HEADS):
        sl = slice(h * DN_D, (h + 1) * DN_D)
        q = q_ref[0, :, sl] * scale
        k = k_ref[0, :, sl]
        v = v_ref[0, :, sl]
        kq = _bdot_nt(jnp.concatenate([k, q], axis=0), k)
        kk, qk = kq[:r], kq[r:]
        for d in range(2):
            lane = SM_A + d * DN_HEADS + h
            incl, strict = masks[d]
            gc, gct = gcs[d]
            gcol = gc[:, lane:lane + 1]
            grow = gct[lane:lane + 1, :]
            gt = gtot[:, lane:lane + 1]
            bcol = beta_all[:, SM_B + d * DN_HEADS + h:SM_B + d * DN_HEADS + h + 1]
            decay = jnp.where(incl, jnp.exp(jnp.where(incl, gcol - grow, 0.0)), 0.0)
            nm[h, d] = jnp.where(strict, -(bcol * kk) * decay, 0.0)
            tinv[h, d] = eye + nm[h, d]
            eg = jnp.exp(gcol)
            rhs[h, d] = jnp.concatenate([v * bcol, k * (bcol * eg)], axis=1).astype(BF16)
            qd16 = (q * eg).astype(BF16)
            ktt = (k * jnp.exp(gt - gcol)).T.astype(BF16)
            for c in range(nck):
                rows = slice(c * CHUNK, (c + 1) * CHUNK)
                wq_ref[d, 0, c, CHUNK:2 * CHUNK, sl] = qd16[rows]
                ktt_ref[d, 0, c, :, h * CHUNK:(h + 1) * CHUNK] = ktt[:, rows]
            qkd = qk * decay
            qk_cols[d].append(qkd[:, :CHUNK] + qkd[:, CHUNK:])
            egl_ref[d, 0, :, sl] = jnp.broadcast_to(jnp.exp(gt), (r, DN_D))
    for d in range(2):
        qk_ref[d, 0] = jnp.concatenate(qk_cols[d], axis=1).astype(BF16)
    for _ in range(int(math.log2(CHUNK)) - 1):
        for hd in chains:
            nm[hd] = _bdot(nm[hd], nm[hd])
        for hd in chains:
            tinv[hd] = tinv[hd] + _bdot(tinv[hd], nm[hd])
    for h, d in chains:
        sl = slice(h * DN_D, (h + 1) * DN_D)
        uw = _bdot(tinv[h, d], rhs[h, d])
        u_ref[d, 0, :, sl] = uw[:, :DN_D]
        w16 = uw[:, DN_D:].astype(BF16)
        for c in range(nck):
            wq_ref[d, 0, c, 0:CHUNK, sl] = w16[c * CHUNK:(c + 1) * CHUNK]


def dn_wy(q, k, v, proj, par):
    b, s, w = q.shape
    r = WY_ROWS
    nck = r // CHUNK
    nchunk = s // CHUNK
    seq = pl.BlockSpec((1, r, w), lambda bb, i: (bb, i, 0))
    dseq = pl.BlockSpec((2, 1, r, w), lambda bb, i: (0, bb, i, 0))
    return pl.pallas_call(
        _dn_wy_body,
        out_shape=(jax.ShapeDtypeStruct((2, b, s, w), F32),
                   jax.ShapeDtypeStruct((2, b, nchunk, 2 * CHUNK, w), BF16),
                   jax.ShapeDtypeStruct((2, b, nchunk, DN_D, DN_HEADS * CHUNK), BF16),
                   jax.ShapeDtypeStruct((2, b, s, DN_HEADS * CHUNK), BF16),
                   jax.ShapeDtypeStruct((2, b, s, w), F32)),
        grid=(b, s // r),
        in_specs=[seq, seq, seq,
                  pl.BlockSpec((1, r, LANES), lambda bb, i: (bb, i, OFF_SMALL // LANES)),
                  pl.BlockSpec((2, LANES), lambda bb, i: (0, 0))],
        out_specs=(dseq,
                   pl.BlockSpec((2, 1, nck, 2 * CHUNK, w), lambda bb, i: (0, bb, i, 0, 0)),
                   pl.BlockSpec((2, 1, nck, DN_D, DN_HEADS * CHUNK), lambda bb, i: (0, bb, i, 0, 0)),
                   pl.BlockSpec((2, 1, r, DN_HEADS * CHUNK), lambda bb, i: (0, bb, i, 0)),
                   dseq),
        compiler_params=_cparams(("arbitrary", "arbitrary")),
        name="dn_wy",
    )(q, k, v, proj, par)


def _dn_rec_body(*refs, nc):
    ins = (refs[0:5], refs[5:10])
    outs = refs[10:12]
    state_ref = refs[12]
    n = pl.program_id(1)

    @pl.when(n == 0)
    def _():
        state_ref[...] = jnp.zeros_like(state_ref)

    chains = [(d, h) for d in range(2) for h in range(DN_HEADS)]
    state = {(d, h): state_ref[d, h] for d, h in chains}
    for step in range(nc):
        chunk = (step, nc - 1 - step)
        wq, v_new = {}, {}
        for d, h in chains:
            sl = slice(h * DN_D, (h + 1) * DN_D)
            wq[d, h] = jnp.dot(ins[d][1][0, 0, chunk[d], :, sl], state[d, h].astype(BF16),
                               preferred_element_type=F32)
        for d, h in chains:
            sl = slice(h * DN_D, (h + 1) * DN_D)
            rows = slice(chunk[d] * CHUNK, (chunk[d] + 1) * CHUNK)
            v_new[d, h] = (ins[d][0][0, 0, rows, sl] - wq[d, h][:CHUNK]).astype(BF16)
        for d, h in chains:
            sl = slice(h * DN_D, (h + 1) * DN_D)
            rows = slice(chunk[d] * CHUNK, (chunk[d] + 1) * CHUNK)
            qk = ins[d][3][0, 0, rows, h * CHUNK:(h + 1) * CHUNK]
            outs[d][0, rows, sl] = wq[d, h][CHUNK:] + jnp.dot(qk, v_new[d, h], preferred_element_type=F32)
        for d, h in chains:
            sl = slice(h * DN_D, (h + 1) * DN_D)
            first = chunk[d] * CHUNK
            ktt = ins[d][2][0, 0, chunk[d], :, h * CHUNK:(h + 1) * CHUNK]
            state[d, h] = (state[d, h] * ins[d][4][0, 0, first:first + 1, sl]
                           + jnp.dot(ktt, v_new[d, h], preferred_element_type=F32))
    for d, h in chains:
        state_ref[d, h] = state[d, h]


def dn_rec(u, wq, ktt, qk, egl):
    _, b, s, w = u.shape
    nchunk = s // CHUNK
    nc = max(c for c in (4, 2, 1) if nchunk % c == 0)
    nblk = nchunk // nc
    rows = nc * CHUNK

    def specs(d):
        def blk(n):
            return n if d == 0 else nblk - 1 - n
        return [pl.BlockSpec((1, 1, rows, w), lambda bb, n: (d, bb, blk(n), 0)),
                pl.BlockSpec((1, 1, nc, 2 * CHUNK, w), lambda bb, n: (d, bb, blk(n), 0, 0)),
                pl.BlockSpec((1, 1, nc, DN_D, DN_HEADS * CHUNK), lambda bb, n: (d, bb, blk(n), 0, 0)),
                pl.BlockSpec((1, 1, rows, DN_HEADS * CHUNK), lambda bb, n: (d, bb, blk(n), 0)),
                pl.BlockSpec((1, 1, rows, w), lambda bb, n: (d, bb, blk(n), 0))]

    out = jax.ShapeDtypeStruct((b, s, w), F32)
    return pl.pallas_call(
        functools.partial(_dn_rec_body, nc=nc),
        out_shape=(out, out),
        grid=(b, nblk),
        in_specs=specs(0) + specs(1),
        out_specs=(pl.BlockSpec((1, rows, w), lambda bb, n: (bb, n, 0)),
                   pl.BlockSpec((1, rows, w), lambda bb, n: (bb, nblk - 1 - n, 0))),
        scratch_shapes=[pltpu.VMEM((2, DN_HEADS, DN_D, DN_D), F32)],
        compiler_params=_cparams(("arbitrary", "arbitrary")),
        name="dn_rec",
    )(u, wq, ktt, qk, egl, u, wq, ktt, qk, egl)


def _rope_head(x, g, cos, sin_signed, first_half):
    xf = x * lax.rsqrt(jnp.mean(x * x, axis=-1, keepdims=True) + EPS) * g
    partner = jnp.where(first_half, pltpu.roll(xf, LANES - ROPE_SUB // 2, axis=1),
                        pltpu.roll(xf, ROPE_SUB // 2, axis=1))
    return xf * cos + partner * sin_signed


def _att_prep_body(q_ref, k_ref, v_ref, cos_ref, sin_ref, qg_ref, kg_ref, qo_ref, ko_ref, vo_ref):
    cos = cos_ref[...]
    sin_signed = sin_ref[...]
    lane = lax.broadcasted_iota(jnp.int32, cos.shape, 1)
    first_half = (lane % ROPE_SUB) < ROPE_SUB // 2
    scale = ATT_DH ** -0.5 * LOG2E
    for h in range(ATT_HEADS):
        sl = slice(h * ATT_DH, (h + 1) * ATT_DH)
        qo_ref[0, :, sl] = (_rope_head(q_ref[0, :, sl], qg_ref[...], cos, sin_signed, first_half)
                            * scale).astype(BF16)
    for h in range(ATT_KV_HEADS):
        sl = slice(h * ATT_DH, (h + 1) * ATT_DH)
        ko_ref[0, :, sl] = _rope_head(k_ref[0, :, sl], kg_ref[...], cos, sin_signed, first_half).astype(BF16)
    for h in range(ATT_KV_HEADS):
        vo_ref[0, :, 2 * h * ATT_DH:(2 * h + 1) * ATT_DH] = v_ref[0, :, h * ATT_DH:(h + 1) * ATT_DH].astype(BF16)
        vo_ref[0, :, (2 * h + 1) * ATT_DH:(2 * h + 2) * ATT_DH] = jnp.ones((v_ref.shape[1], ATT_DH), BF16)


def att_prep(proj, cos, sin_signed, qg, kg, ts=512):
    b, s, _ = proj.shape
    ts = _tile(s, ts, 16)
    tab = pl.BlockSpec((ts, ATT_DH), lambda bb, i: (i, 0))
    vec = pl.BlockSpec((1, ATT_DH), lambda bb, i: (0, 0))
    return pl.pallas_call(
        _att_prep_body,
        out_shape=(jax.ShapeDtypeStruct((b, s, ATT_W), BF16),
                   jax.ShapeDtypeStruct((b, s, ATT_KV_W), BF16),
                   jax.ShapeDtypeStruct((b, s, 2 * ATT_KV_W), BF16)),
        grid=(b, s // ts),
        in_specs=[pl.BlockSpec((1, ts, ATT_W), lambda bb, i: (bb, i, OFF_AQ // ATT_W)),
                  pl.BlockSpec((1, ts, ATT_KV_W), lambda bb, i: (bb, i, OFF_AK // ATT_KV_W)),
                  pl.BlockSpec((1, ts, ATT_KV_W), lambda bb, i: (bb, i, OFF_AV // ATT_KV_W)),
                  tab, tab, vec, vec],
        out_specs=(pl.BlockSpec((1, ts, ATT_W), lambda bb, i: (bb, i, 0)),
                   pl.BlockSpec((1, ts, ATT_KV_W), lambda bb, i: (bb, i, 0)),
                   pl.BlockSpec((1, ts, 2 * ATT_KV_W), lambda bb, i: (bb, i, 0))),
        compiler_params=_cparams(("arbitrary", "arbitrary")),
        name="att_prep",
    )(proj, proj, proj, cos, sin_signed, qg.reshape(1, ATT_DH), kg.reshape(1, ATT_DH))


def _flash_body(q_ref, k_ref, v_ref, o_ref, *, tk, nkv):
    tq = q_ref.shape[1]
    heads = range(ATT_GROUP)
    qs = [q_ref[0, :, h * ATT_DH:(h + 1) * ATT_DH] for h in heads]

    def step(t, carry):
        start = pl.multiple_of(t * tk, tk)
        ks = k_ref[0, pl.ds(start, tk), :]
        vs = v_ref[0, pl.ds(start, tk), :]
        new = []
        for h in heads:
            m, acc = carry[h]
            sc = lax.dot_general(qs[h], ks, (((1,), (1,)), ((), ())), preferred_element_type=F32)
            m_new = jnp.maximum(m, jnp.max(sc, axis=-1, keepdims=True))
            p = jnp.exp2(sc - m_new)
            acc = jnp.exp2(m - m_new) * acc + jnp.dot(p.astype(BF16), vs, preferred_element_type=F32)
            new.append((m_new, acc))
        return tuple(new)

    init = tuple((jnp.full((tq, 1), -1e30, F32), jnp.zeros((tq, 2 * ATT_DH), F32)) for _ in heads)
    res = lax.fori_loop(0, nkv, step, init, unroll=2 if nkv % 2 == 0 else 1)
    for h in heads:
        acc = res[h][1]
        o_ref[0, :, h * ATT_DH:(h + 1) * ATT_DH] = (acc[:, :ATT_DH] / acc[:, ATT_DH:]).astype(o_ref.dtype)


def flash_attention(q, k, v, tq=512, tk=512):
    b, s, _ = q.shape
    tq = _tile(s, tq, 16)
    tk = _tile(s, tk, 16)
    gw = ATT_GROUP * ATT_DH
    return pl.pallas_call(
        functools.partial(_flash_body, tk=tk, nkv=s // tk),
        out_shape=jax.ShapeDtypeStruct((b, s, ATT_W), BF16),
        grid=(b, ATT_KV_HEADS, s // tq),
        in_specs=[pl.BlockSpec((1, tq, gw), lambda bb, g, i: (bb, i, g)),
                  pl.BlockSpec((1, s, ATT_DH), lambda bb, g, i: (bb, 0, g)),
                  pl.BlockSpec((1, s, 2 * ATT_DH), lambda bb, g, i: (bb, 0, g))],
        out_specs=pl.BlockSpec((1, tq, gw), lambda bb, g, i: (bb, i, g)),
        compiler_params=_cparams(("arbitrary", "arbitrary", "arbitrary")),
        name="flash_attention",
    )(q, k, v)


def _gla_chunk(qs, k, v, gk, state_ref, d, reverse):
    c = CHUNK
    nsub = c // SUB
    ri = lax.broadcasted_iota(jnp.int32, (c, c), 0)
    ci = lax.broadcasted_iota(jnp.int32, (c, c), 1)
    incl = (ci >= ri) if reverse else (ri >= ci)
    gc = _hdot(incl.astype(F32), gk)
    last = 0 if reverse else c - 1
    gl = gc[last:last + 1, :]
    qd = qs * jnp.exp(gc)
    kt = k * jnp.exp(gl - gc)
    egl = jnp.exp(gl)
    row = lax.broadcasted_iota(jnp.int32, (c, GLA_QK), 0)

    a_off = [[] for _ in range(GLA_HEADS)]
    for i in range(nsub):
        if i == (nsub - 1 if reverse else 0):
            for h in range(GLA_HEADS):
                a_off[h].append(jnp.zeros((SUB, c), F32))
            continue
        first = (i + 1) * SUB - 1 if reverse else i * SUB
        rs = slice(i * SUB, (i + 1) * SUB)
        ref = gc[first:first + 1, :]
        qi = qs[rs, :] * jnp.exp(gc[rs, :] - ref)
        early = (row > first) if reverse else (row < first)
        kf = k * jnp.where(early, jnp.exp(jnp.where(early, ref - gc, 0.0)), 0.0)
        for h in range(GLA_HEADS):
            sl = slice(h * GLA_DK, (h + 1) * GLA_DK)
            a_off[h].append(_bdot_nt(qi[:, sl], kf[:, sl]))

    lane = lax.broadcasted_iota(jnp.int32, (LANES, 2 * GLA_DV), 0)
    col = lax.broadcasted_iota(jnp.int32, (LANES, 2 * GLA_DV), 1)
    head_sum = ((lane // GLA_DK) == (col // GLA_DV)).astype(BF16)
    sub_row = lax.broadcasted_iota(jnp.int32, (SUB, LANES), 0)
    o_diag = []
    for p in range(GLA_HEADS // 2):
        ls = slice(p * LANES, (p + 1) * LANES)
        vs = slice(p * 2 * GLA_DV, (p + 1) * 2 * GLA_DV)
        blocks = []
        for i in range(nsub):
            rs = slice(i * SUB, (i + 1) * SUB)
            gci, qsi = gc[rs, ls], qs[rs, ls]
            terms = []
            for jj in range(SUB):
                j = i * SUB + jj
                later = (sub_row <= jj) if reverse else (sub_row >= jj)
                e = jnp.where(later, jnp.exp(jnp.where(later, gci - gc[j:j + 1, ls], 0.0)), 0.0)
                terms.append(qsi * k[j:j + 1, ls] * e)
            r = jnp.dot(jnp.concatenate(terms, axis=0).astype(BF16), head_sum,
                        preferred_element_type=F32)
            acc = r[0:SUB, :] * v[i * SUB:i * SUB + 1, vs]
            for jj in range(1, SUB):
                acc = acc + r[jj * SUB:(jj + 1) * SUB, :] * v[i * SUB + jj:i * SUB + jj + 1, vs]
            blocks.append(acc)
        o_diag.append(jnp.concatenate(blocks, axis=0))

    outs = []
    for h in range(GLA_HEADS):
        sl = slice(h * GLA_DK, (h + 1) * GLA_DK)
        vh = v[:, h * GLA_DV:(h + 1) * GLA_DV]
        st = state_ref[d, h]
        a = jnp.concatenate(a_off[h], axis=0)
        od = o_diag[h // 2][:, (h % 2) * GLA_DV:(h % 2 + 1) * GLA_DV]
        outs.append(_bdot_nt(qd[:, sl], st) + _bdot(a, vh) + od)
        state_ref[d, h] = st * egl[:, sl] + lax.dot_general(
            vh.astype(BF16), kt[:, sl].astype(BF16), (((0,), (0,)), ((), ())), preferred_element_type=F32)
    return jnp.concatenate(outs, axis=1)


def _gla_scan_body(qf_ref, kf_ref, vf_ref, smf_ref, qb_ref, kb_ref, vb_ref, smb_ref, up_ref, upb_ref,
                   of_ref, ob_ref, state_ref):
    n = pl.program_id(1)

    @pl.when(n == 0)
    def _():
        state_ref[...] = jnp.zeros_like(state_ref)

    ins = ((qf_ref, kf_ref, vf_ref, smf_ref, of_ref), (qb_ref, kb_ref, vb_ref, smb_ref, ob_ref))
    for d, (q_ref, k_ref, v_ref, sm_ref, o_ref) in enumerate(ins):
        gk = jax.nn.log_sigmoid(_hdot(sm_ref[0], up_ref[d]) + upb_ref[d]) * (1.0 / GLA_NORMALIZER)
        qs = q_ref[0] * (GLA_DK ** -0.5)
        o_ref[0] = _gla_chunk(qs, k_ref[0], v_ref[0], gk, state_ref, d, reverse=(d == 1))


def gla_scan(proj, up_pad, upb):
    b, s, _ = proj.shape
    nchunk = s // CHUNK

    def specs(d):
        def cidx(n):
            return n if d == 0 else nchunk - 1 - n
        return [pl.BlockSpec((1, CHUNK, GLA_QK), lambda bb, n: (bb, cidx(n), OFF_GQ // GLA_QK)),
                pl.BlockSpec((1, CHUNK, GLA_QK), lambda bb, n: (bb, cidx(n), OFF_GK // GLA_QK)),
                pl.BlockSpec((1, CHUNK, GLA_W), lambda bb, n: (bb, cidx(n), OFF_GV // GLA_W)),
                pl.BlockSpec((1, CHUNK, LANES), lambda bb, n: (bb, cidx(n), OFF_SMALL // LANES))]

    out = jax.ShapeDtypeStruct((b, s, GLA_W), F32)
    return pl.pallas_call(
        _gla_scan_body,
        out_shape=(out, out),
        grid=(b, nchunk),
        in_specs=specs(0) + specs(1) + [pl.BlockSpec((2, LANES, GLA_QK), lambda bb, n: (0, 0, 0)),
                                        pl.BlockSpec((2, 1, GLA_QK), lambda bb, n: (0, 0, 0))],
        out_specs=(pl.BlockSpec((1, CHUNK, GLA_W), lambda bb, n: (bb, n, 0)),
                   pl.BlockSpec((1, CHUNK, GLA_W), lambda bb, n: (bb, nchunk - 1 - n, 0))),
        scratch_shapes=[pltpu.VMEM((2, GLA_HEADS, GLA_DV, GLA_DK), F32)],
        compiler_params=_cparams(("arbitrary", "arbitrary")),
        name="gla_scan",
    )(proj, proj, proj, proj, proj, proj, proj, proj, up_pad, upb)


def _gated_norm(o, gate, g, heads, d):
    outs = []
    for h in range(heads):
        sl = slice(h * d, (h + 1) * d)
        t = o[:, sl]
        y = t * lax.rsqrt(jnp.mean(t * t, axis=-1, keepdims=True) + EPS) * g
        gt = gate[:, sl]
        outs.append(y * (gt * jax.nn.sigmoid(gt)))
    return jnp.concatenate(outs, axis=1)


def _merge_body(dnf_ref, dnb_ref, dgate_ref, dng_ref, att_ref, glf_ref, glb_ref, ggate_ref, glg_ref, o_ref):
    o_dn = _gated_norm(dnf_ref[0] + dnb_ref[0], dgate_ref[0], dng_ref[...], DN_HEADS, DN_D)
    o_gla = _gated_norm(glf_ref[0] + glb_ref[0], ggate_ref[0], glg_ref[...], GLA_HEADS, GLA_DV)
    o_ref[0, :, 0:DN_W] = o_dn.astype(BF16)
    o_ref[0, :, DN_W:DN_W + ATT_W] = att_ref[0]
    o_ref[0, :, DN_W + ATT_W:] = o_gla.astype(BF16)


def merge_heads(dn_f, dn_b, proj, dn_g, att, gl_f, gl_b, gla_g, ts=512):
    b, s, _ = proj.shape
    ts = _tile(s, ts, 16)
    mix_w = DN_W + ATT_W + GLA_W

    def spec(w, blk=0):
        return pl.BlockSpec((1, ts, w), lambda bb, i: (bb, i, blk))

    vec = pl.BlockSpec((1, LANES), lambda bb, i: (0, 0))
    return pl.pallas_call(
        _merge_body,
        out_shape=jax.ShapeDtypeStruct((b, s, mix_w), BF16),
        grid=(b, s // ts),
        in_specs=[spec(DN_W), spec(DN_W), spec(DN_W, OFF_DGATE // DN_W), vec,
                  spec(ATT_W), spec(GLA_W), spec(GLA_W), spec(GLA_W, OFF_GGATE // GLA_W), vec],
        out_specs=spec(mix_w),
        compiler_params=_cparams(("arbitrary", "arbitrary")),
        name="merge_heads",
    )(dn_f, dn_b, proj, dn_g.reshape(1, DN_D), att, gl_f, gl_b, proj, gla_g.reshape(1, GLA_DV))


def _split_points():
    pts, acc = [], 0
    for sz in IN_SIZES[:-1]:
        acc += sz
        pts.append(acc)
    return pts


def _relayout_w_in(w):
    d = w.shape[0]
    (dq, dk, dv, dgate, a_f, a_b, b_f, b_b, aq, ak, av, gq, gkk, gv, ggate, lr_f, lr_b) = jnp.split(
        w, _split_points(), axis=1)
    small = jnp.concatenate([a_f, a_b, b_f, b_b, lr_f, lr_b], axis=1)
    small = jnp.pad(small, ((0, 0), (0, LANES - small.shape[1])))
    cols = jnp.concatenate([dq, dk, dv, dgate, aq, ak, av, gq, gv, ggate, gkk, small,
                            jnp.zeros((d, PROJ_COLS - OFF_SMALL - LANES), w.dtype)], axis=1)
    return cols.astype(BF16)[None, None]


def _rope_tables(s):
    rows = s // GRID_W
    row = jnp.repeat(jnp.arange(rows, dtype=jnp.int32), GRID_W).astype(F32)
    col = jnp.tile(jnp.arange(GRID_W, dtype=jnp.int32), rows).astype(F32)
    inv_freq = ROPE_THETA ** (-jnp.arange(0, ROPE_SUB, 2, dtype=F32) / ROPE_SUB)
    ang_r = row[:, None] * inv_freq[None, :]
    ang_c = col[:, None] * inv_freq[None, :]
    cos = jnp.concatenate([jnp.cos(ang_r), jnp.cos(ang_r), jnp.cos(ang_c), jnp.cos(ang_c)], axis=1)
    sin = jnp.concatenate([-jnp.sin(ang_r), jnp.sin(ang_r), -jnp.sin(ang_c), jnp.sin(ang_c)], axis=1)
    return cos, sin


def _dn_params(dn_a_log, dn_dt_bias):
    rows = jnp.stack([dn_a_log.reshape(-1), dn_dt_bias.reshape(-1)], axis=0)
    return jnp.pad(rows, ((0, 0), (SM_A, LANES - SM_A - 2 * DN_HEADS)))


def _gla_params(gla_up, gla_up_b):
    ups = []
    for d in range(2):
        lo = SM_LR + d * GLA_RANK
        ups.append(jnp.pad(gla_up[d], ((lo, LANES - lo - GLA_RANK), (0, 0))))
    return jnp.stack(ups, axis=0), gla_up_b[:, None, :]


def _mixer(x16, bsz, s, w_in, dn_conv, dn_a_log, dn_dt_bias, dn_norm_g, att_qn_g, att_kn_g,
           gla_up, gla_up_b, gla_norm_g, rope):
    t = bsz * s
    bm = _tile(t, 512, 16)
    proj = grouped_matmul(x16, _relayout_w_in(w_in), 0, jnp.zeros((t // bm,), jnp.int32), bm, 1024, F32)
    proj = proj.reshape(bsz, s, PROJ_COLS)

    q, k, v = dn_prep(proj, dn_conv)
    u, wq, ktt, qk, egl = dn_wy(q, k, v, proj, _dn_params(dn_a_log, dn_dt_bias))
    dn_f, dn_b = dn_rec(u, wq, ktt, qk, egl)

    cos, sin_signed = rope
    aq, ak, av = att_prep(proj, cos, sin_signed, att_qn_g, att_kn_g)
    o_att = flash_attention(aq, ak, av)

    gl_f, gl_b = gla_scan(proj, *_gla_params(gla_up, gla_up_b))

    return merge_heads(dn_f, dn_b, proj, dn_norm_g, o_att, gl_f, gl_b, gla_norm_g).reshape(t, -1)


def _moe(x32, x16, router_w, w_gate, w_up, w_down, layer, ln_g, ln_b, alpha):
    t, d = x32.shape
    rw = jnp.pad(router_w, ((0, 0), (0, LANES - N_EXPERTS)))
    logits = router_logits(x32, rw)[:, :N_EXPERTS]
    top_val, top_idx = lax.top_k(logits, TOP_K)
    gates = jax.nn.softmax(top_val, axis=-1)
    e_flat = top_idx.reshape(-1).astype(jnp.int32)
    tok_flat = jnp.repeat(jnp.arange(t, dtype=jnp.int32), TOP_K)
    onehot = (e_flat[:, None] == jnp.arange(N_EXPERTS, dtype=jnp.int32)[None, :]).astype(jnp.int32)
    csum = jnp.cumsum(onehot, axis=0)
    counts = csum[-1]
    rank = jnp.take_along_axis(csum, e_flat[:, None], axis=1)[:, 0] - 1
    padded = (counts + MOE_BLOCK - 1) // MOE_BLOCK * MOE_BLOCK
    pstart = jnp.cumsum(padded) - padded
    pend = pstart + padded
    dest = pstart[e_flat] + rank
    nb = -(-(TOP_K * t) // MOE_BLOCK) + N_EXPERTS
    cap = nb * MOE_BLOCK
    buf_tok = jnp.zeros((cap,), jnp.int32).at[dest].set(tok_flat)
    block_start = jnp.arange(nb, dtype=jnp.int32) * MOE_BLOCK
    block_exp = jnp.minimum(jnp.sum(block_start[:, None] >= pend[None, :], axis=-1),
                            N_EXPERTS - 1).astype(jnp.int32)
    xb = x16[buf_tok]
    h = grouped_swiglu(xb, w_gate, w_up, layer, block_exp, MOE_BLOCK, 1024)
    yb = grouped_matmul(h, w_down, layer, block_exp, MOE_BLOCK, 512, BF16)
    return combine_ln(x32, yb[dest].reshape(t, TOP_K * d), gates, ln_g, ln_b, alpha)


def _dense_ffn(x32, x16, w_gate, w_up, w_down, layer, ln_g, ln_b, alpha):
    t = x32.shape[0]
    bm = _tile(t, 512, 16)
    grp = jnp.zeros((t // bm,), jnp.int32)
    h = grouped_swiglu(x16, w_gate, w_up, layer, grp, bm, 512)
    return matmul_res_ln(h, w_down[layer].astype(BF16), x32, ln_g, ln_b, alpha)


def kernel(x, w_in, dn_conv, dn_a_log, dn_dt_bias, dn_norm_g, att_qn_g, att_kn_g, gla_up, gla_up_b,
           gla_norm_g, w_out, ln1_g, ln1_b, ln2_g, ln2_b, ffn_w_gate, ffn_w_up, ffn_w_down, router_w,
           exp_w_gate, exp_w_up, exp_w_down):
    bsz, s, d = x.shape
    depth = w_in.shape[0]
    alpha = (2.0 * depth) ** 0.25
    t = bsz * s
    rope = _rope_tables(s)
    x32 = x.reshape(t, d)
    x16 = x32.astype(BF16)
    for layer in range(depth):
        mix = _mixer(x16, bsz, s, w_in[layer], dn_conv[layer], dn_a_log[layer], dn_dt_bias[layer],
                     dn_norm_g[layer], att_qn_g[layer], att_kn_g[layer], gla_up[layer],
                     gla_up_b[layer], gla_norm_g[layer], rope)
        x32, x16 = matmul_res_ln(mix, w_out[layer].astype(BF16), x32, ln1_g[layer], ln1_b[layer], alpha)
        j = layer // 2
        if layer % 2 == 0:
            x32, x16 = _dense_ffn(x32, x16, ffn_w_gate, ffn_w_up, ffn_w_down, j,
                                  ln2_g[layer], ln2_b[layer], alpha)
        else:
            x32, x16 = _moe(x32, x16, router_w[j], exp_w_gate, exp_w_up, exp_w_down, j,
                            ln2_g[layer], ln2_b[layer], alpha)
    return x32.reshape(bsz, s, d)
```

```python
import functools
import math

import jax
import jax.numpy as jnp
from jax import lax
from jax.experimental import pallas as pl
from jax.experimental.pallas import tpu as pltpu

F32 = jnp.float32
BF16 = jnp.bfloat16
HIGHEST = lax.Precision.HIGHEST

DN_HEADS, DN_D = 6, 128
ATT_HEADS, ATT_KV_HEADS, ATT_DH = 6, 2, 128
ATT_GROUP = ATT_HEADS // ATT_KV_HEADS
ROPE_SUB, ROPE_THETA, GRID_W = 64, 10000.0, 64
GLA_HEADS, GLA_DK, GLA_DV, GLA_RANK = 4, 64, 128, 16
GLA_NORMALIZER = 16.0
CHUNK = 64
SUB = 8
MASKED_LOG = -1e30
WY_ROWS = 2 * CHUNK
CONV_W = 5
N_EXPERTS, TOP_K, MOE_BLOCK = 8, 2, 256
EPS = 1e-6
LOG2E = 1.4426950408889634
LANES = 128
HALO = 8

DN_W = DN_HEADS * DN_D
ATT_W = ATT_HEADS * ATT_DH
ATT_KV_W = ATT_KV_HEADS * ATT_DH
GLA_QK = GLA_HEADS * GLA_DK
GLA_W = GLA_HEADS * GLA_DV
IN_SIZES = (DN_W, DN_W, DN_W, DN_W, DN_HEADS, DN_HEADS, DN_HEADS, DN_HEADS,
            ATT_W, ATT_KV_W, ATT_KV_W, GLA_QK, GLA_QK, GLA_W, GLA_W, GLA_RANK, GLA_RANK)
OFF_DQ, OFF_DGATE, OFF_AQ, OFF_AK, OFF_AV = 0, 2304, 3072, 3840, 4096
OFF_GQ, OFF_GV, OFF_GGATE, OFF_GK, OFF_SMALL = 4352, 4608, 5120, 5632, 5888
PROJ_COLS = 6144
SM_A, SM_B, SM_LR = 0, 2 * DN_HEADS, 4 * DN_HEADS

VMEM_LIMIT = 56 * 1024 * 1024


def _cparams(sem, vmem=VMEM_LIMIT):
    return pltpu.CompilerParams(dimension_semantics=sem, vmem_limit_bytes=vmem)


def _tile(n, pref, quantum=LANES):
    if n <= pref:
        return n
    t = pref - pref % quantum
    while n % t:
        t -= quantum
    return t


def _bdot(a, b):
    return jnp.dot(a.astype(BF16), b.astype(BF16), preferred_element_type=F32)


def _bdot_nt(a, b):
    return lax.dot_general(a.astype(BF16), b.astype(BF16), (((1,), (1,)), ((), ())),
                           preferred_element_type=F32)


def _hdot(a, b):
    return jnp.dot(a, b, preferred_element_type=F32, precision=HIGHEST)


def _group_changed(be_ref):
    i = pl.program_id(1)
    return (i == 0) | (be_ref[i] != be_ref[jnp.maximum(i - 1, 0)])


def _gmm_body(be_ref, x_ref, w_ref, o_ref, *w16):
    if w16:
        @pl.when(_group_changed(be_ref))
        def _():
            w16[0][...] = w_ref[...].astype(BF16)
        w = w16[0][...]
    else:
        w = w_ref[...]
    o_ref[...] = jnp.dot(x_ref[...], w, preferred_element_type=F32).astype(o_ref.dtype)


def _weight_spec(w, layer, k, tn):
    if w.ndim == 3:
        return pl.BlockSpec((None, k, tn), lambda j, i, be: (layer, 0, j))
    return pl.BlockSpec((None, None, k, tn), lambda j, i, be: (layer, be[i], 0, j))


def grouped_matmul(x, w, layer, block_group, bm, tn, out_dtype):
    m, k = x.shape
    n = w.shape[-1]
    tn = _tile(n, tn)
    scratch = [pltpu.VMEM((k, tn), BF16)] if w.dtype != BF16 else []
    return pl.pallas_call(
        _gmm_body,
        out_shape=jax.ShapeDtypeStruct((m, n), out_dtype),
        grid_spec=pltpu.PrefetchScalarGridSpec(
            num_scalar_prefetch=1, grid=(n // tn, m // bm),
            in_specs=[pl.BlockSpec((bm, k), lambda j, i, be: (i, 0)), _weight_spec(w, layer, k, tn)],
            out_specs=pl.BlockSpec((bm, tn), lambda j, i, be: (i, j)),
            scratch_shapes=scratch),
        compiler_params=_cparams(("arbitrary", "arbitrary")),
        name="grouped_matmul",
    )(block_group, x, w)


def _gswiglu_body(be_ref, x_ref, wg_ref, wu_ref, o_ref, wg16, wu16):
    @pl.when(_group_changed(be_ref))
    def _():
        wg16[...] = wg_ref[...].astype(BF16)
        wu16[...] = wu_ref[...].astype(BF16)

    x = x_ref[...]
    g = jnp.dot(x, wg16[...], preferred_element_type=F32)
    u = jnp.dot(x, wu16[...], preferred_element_type=F32)
    o_ref[...] = (g * jax.nn.sigmoid(g) * u).astype(o_ref.dtype)


def grouped_swiglu(x, wg, wu, layer, block_group, bm, tn):
    m, k = x.shape
    n = wg.shape[-1]
    tn = _tile(n, tn)
    return pl.pallas_call(
        _gswiglu_body,
        out_shape=jax.ShapeDtypeStruct((m, n), BF16),
        grid_spec=pltpu.PrefetchScalarGridSpec(
            num_scalar_prefetch=1, grid=(n // tn, m // bm),
            in_specs=[pl.BlockSpec((bm, k), lambda j, i, be: (i, 0)),
                      _weight_spec(wg, layer, k, tn), _weight_spec(wu, layer, k, tn)],
            out_specs=pl.BlockSpec((bm, tn), lambda j, i, be: (i, j)),
            scratch_shapes=[pltpu.VMEM((k, tn), BF16), pltpu.VMEM((k, tn), BF16)]),
        compiler_params=_cparams(("arbitrary", "arbitrary")),
        name="grouped_swiglu",
    )(block_group, x, wg, wu)


def _res_ln(res, y, g, b, alpha):
    z = alpha * res + y
    mu = jnp.mean(z, axis=-1, keepdims=True)
    zc = z - mu
    var = jnp.mean(zc * zc, axis=-1, keepdims=True)
    return zc * lax.rsqrt(var + EPS) * g + b


def _mm_res_ln_body(x_ref, w_ref, res_ref, g_ref, b_ref, o32_ref, o16_ref, *acc, nk, alpha):
    kk = pl.program_id(1)
    part = jnp.dot(x_ref[...], w_ref[...], preferred_element_type=F32)

    def finish(y):
        out = _res_ln(res_ref[...], y, g_ref[...], b_ref[...], alpha)
        o32_ref[...] = out
        o16_ref[...] = out.astype(BF16)

    if nk == 1:
        finish(part)
        return
    acc_ref = acc[0]

    @pl.when(kk == 0)
    def _():
        acc_ref[...] = part

    @pl.when((kk > 0) & (kk < nk - 1))
    def _():
        acc_ref[...] += part

    @pl.when(kk == nk - 1)
    def _():
        finish(acc_ref[...] + part)


def matmul_res_ln(x, w, res, g, b, alpha, tm=512, tk=2048):
    m, k = x.shape
    n = w.shape[-1]
    tm = _tile(m, tm, 8)
    tk = _tile(k, tk)
    nk = k // tk
    return pl.pallas_call(
        functools.partial(_mm_res_ln_body, nk=nk, alpha=alpha),
        out_shape=(jax.ShapeDtypeStruct((m, n), F32), jax.ShapeDtypeStruct((m, n), BF16)),
        grid=(m // tm, nk),
        in_specs=[pl.BlockSpec((tm, tk), lambda i, kk: (i, kk)),
                  pl.BlockSpec((tk, n), lambda i, kk: (kk, 0)),
                  pl.BlockSpec((tm, n), lambda i, kk: (i, 0)),
                  pl.BlockSpec((1, n), lambda i, kk: (0, 0)),
                  pl.BlockSpec((1, n), lambda i, kk: (0, 0))],
        out_specs=(pl.BlockSpec((tm, n), lambda i, kk: (i, 0)),
                   pl.BlockSpec((tm, n), lambda i, kk: (i, 0))),
        scratch_shapes=[pltpu.VMEM((tm, n), F32)] if nk > 1 else [],
        compiler_params=_cparams(("arbitrary", "arbitrary")),
        name="matmul_res_ln",
    )(x, w, res, g.reshape(1, n), b.reshape(1, n))


def _combine_ln_body(res_ref, *refs, alpha):
    y_refs, (gate_ref, g_ref, b_ref, o32_ref, o16_ref) = refs[:TOP_K], refs[TOP_K:]
    gate = gate_ref[...]
    y = y_refs[0][...].astype(F32) * gate[:, 0:1]
    for kk in range(1, TOP_K):
        y = y + y_refs[kk][...].astype(F32) * gate[:, kk:kk + 1]
    out = _res_ln(res_ref[...], y, g_ref[...], b_ref[...], alpha)
    o32_ref[...] = out
    o16_ref[...] = out.astype(BF16)


def combine_ln(res, y, gates, g, b, alpha, tm=512):
    m, n = res.shape
    tm = _tile(m, tm, 16)
    nblk = m // tm
    row = pl.BlockSpec((tm, n), lambda i: (i, 0))
    vec = pl.BlockSpec((1, n), lambda i: (0, 0))

    def choice(kk):
        return pl.BlockSpec((tm, n), lambda i: (kk * nblk + i, 0))

    return pl.pallas_call(
        functools.partial(_combine_ln_body, alpha=alpha),
        out_shape=(jax.ShapeDtypeStruct((m, n), F32), jax.ShapeDtypeStruct((m, n), BF16)),
        grid=(nblk,),
        in_specs=[row] + [choice(kk) for kk in range(TOP_K)] + [pl.BlockSpec((tm, TOP_K), lambda i: (i, 0)), vec, vec],
        out_specs=(row, row),
        compiler_params=_cparams(("arbitrary",)),
        name="combine_ln",
    )(res, *([y] * TOP_K), gates, g.reshape(1, n), b.reshape(1, n))


def _router_body(x_ref, w_ref, o_ref):
    o_ref[...] = _hdot(x_ref[...], w_ref[...])


def router_logits(x, w_pad, tm=512):
    m, k = x.shape
    n = w_pad.shape[-1]
    tm = _tile(m, tm, 8)
    return pl.pallas_call(
        _router_body,
        out_shape=jax.ShapeDtypeStruct((m, n), F32),
        grid=(m // tm,),
        in_specs=[pl.BlockSpec((tm, k), lambda i: (i, 0)), pl.BlockSpec((k, n), lambda i: (0, 0))],
        out_specs=pl.BlockSpec((tm, n), lambda i: (i, 0)),
        compiler_params=_cparams(("arbitrary",)),
        name="router_logits",
    )(x, w_pad)


def _dn_prep_body(cur_ref, prev_ref, next_ref, w_ref, q_ref, k_ref, v_ref, buf_ref, *, ts, nblk):
    i = pl.program_id(1)
    buf_ref[0:HALO, :] = jnp.where(i > 0, prev_ref[0], 0.0)
    buf_ref[HALO:HALO + ts, :] = cur_ref[0]
    buf_ref[HALO + ts:2 * HALO + ts, :] = jnp.where(i < nblk - 1, next_ref[0], 0.0)
    base = HALO - CONV_W // 2
    acc = buf_ref[base:base + ts, :] * w_ref[0:1, :]
    for j in range(1, CONV_W):
        acc = acc + buf_ref[base + j:base + j + ts, :] * w_ref[j:j + 1, :]
    y = acc * jax.nn.sigmoid(acc)
    for h in range(DN_HEADS):
        for off, ref in ((0, q_ref), (DN_W, k_ref)):
            t = y[:, off + h * DN_D:off + (h + 1) * DN_D]
            ref[0, :, h * DN_D:(h + 1) * DN_D] = t * lax.rsqrt(jnp.sum(t * t, axis=-1, keepdims=True) + EPS)
    v_ref[0] = y[:, 2 * DN_W:3 * DN_W]


def dn_prep(proj, conv_w, ts=256):
    b, s, _ = proj.shape
    ts = _tile(s, ts, HALO)
    nblk = s // ts
    c = 3 * DN_W
    hb = ts // HALO
    out = jax.ShapeDtypeStruct((b, s, DN_W), F32)
    ospec = pl.BlockSpec((1, ts, DN_W), lambda bb, i: (bb, i, 0))
    return pl.pallas_call(
        functools.partial(_dn_prep_body, ts=ts, nblk=nblk),
        out_shape=(out, out, out),
        grid=(b, nblk),
        in_specs=[pl.BlockSpec((1, ts, c), lambda bb, i: (bb, i, 0)),
                  pl.BlockSpec((1, HALO, c), lambda bb, i: (bb, jnp.maximum(i * hb - 1, 0), 0)),
                  pl.BlockSpec((1, HALO, c), lambda bb, i: (bb, jnp.minimum((i + 1) * hb, nblk * hb - 1), 0)),
                  pl.BlockSpec((CONV_W, c), lambda bb, i: (0, 0))],
        out_specs=(ospec, ospec, ospec),
        scratch_shapes=[pltpu.VMEM((ts + 2 * HALO, c), F32)],
        compiler_params=_cparams(("arbitrary", "arbitrary")),
        name="dn_prep",
    )(proj, proj, proj, conv_w)


def _dn_wy_body(q_ref, k_ref, v_ref, sm_ref, par_ref, u_ref, wq_ref, ktt_ref, qk_ref, egl_ref):
    r = WY_ROWS
    nck = r // CHUNK
    sm = sm_ref[0]
    par = par_ref[...]
    g_all = -jnp.exp(par[0:1, :]) * jax.nn.softplus(sm + par[1:2, :])
    beta_all = jax.nn.sigmoid(sm)
    ri = lax.broadcasted_iota(jnp.int32, (r, r), 0)
    ci = lax.broadcasted_iota(jnp.int32, (r, r), 1)
    same = (ri // CHUNK) == (ci // CHUNK)
    eye = (ri == ci).astype(F32)
    gtot = _hdot(same.astype(F32), g_all)
    masks, gcs = [], []
    for d in range(2):
        delta = ri - ci if d == 0 else ci - ri
        incl = same & (delta >= 0)
        strict = same & (delta > 0)
        gc = _hdot(incl.astype(F32), g_all)
        masks.append((incl, strict))
        gcs.append((gc, gc.T))
    scale = DN_D ** -0.5
    chains = [(h, d) for h in range(DN_HEADS) for d in range(2)]
    qk_cols = ([], [])
    nm, tinv, rhs = {}, {}, {}
    for h in range(DN_HEADS):
        sl = slice(h * DN_D, (h + 1) * DN_D)
        q = q_ref[0, :, sl] * scale
        k = k_ref[0, :, sl]
        v = v_ref[0, :, sl]
        kq = _bdot_nt(jnp.concatenate([k, q], axis=0), k)
        kk, qk = kq[:r], kq[r:]
        for d in range(2):
            lane = SM_A + d * DN_HEADS + h
            incl, strict = masks[d]
            gc, gct = gcs[d]
            gcol = jnp.broadcast_to(gc[:, lane:lane + 1], (r, r))
            gt = jnp.broadcast_to(gtot[:, lane:lane + 1], (r, r))
            bcol = jnp.broadcast_to(beta_all[:, SM_B + d * DN_HEADS + h:SM_B + d * DN_HEADS + h + 1], (r, r))
            grow = gct[lane:lane + 1, :]
            decay = jnp.where(incl, jnp.exp(jnp.where(incl, gcol - grow, 0.0)), 0.0)
            nm[h, d] = jnp.where(strict, -(bcol * kk) * decay, 0.0)
            tinv[h, d] = eye + nm[h, d]
            eg = jnp.exp(gcol)
            rhs[h, d] = jnp.concatenate([v * bcol, k * (bcol * eg)], axis=1).astype(BF16)
            qd16 = (q * eg).astype(BF16)
            ktt = _bdot_nt(eye, (k * jnp.exp(gt - gcol)).astype(BF16)).astype(BF16)
            for c in range(nck):
                rows = slice(c * CHUNK, (c + 1) * CHUNK)
                wq_ref[d, 0, c, CHUNK:2 * CHUNK, sl] = qd16[rows]
                ktt_ref[d, 0, c, :, h * CHUNK:(h + 1) * CHUNK] = ktt[:, rows]
            qkd = qk * decay
            qk_cols[d].append(qkd[:, :CHUNK] + qkd[:, CHUNK:])
            egl_ref[d, 0, :, sl] = jnp.exp(gt)
    for d in range(2):
        qk_ref[d, 0] = jnp.concatenate(qk_cols[d], axis=1).astype(BF16)
    for _ in range(int(math.log2(CHUNK)) - 1):
        for hd in chains:
            nm[hd] = _bdot(nm[hd], nm[hd])
        for hd in chains:
            tinv[hd] = tinv[hd] + _bdot(tinv[hd], nm[hd])
    for h, d in chains:
        sl = slice(h * DN_D, (h + 1) * DN_D)
        uw = _bdot(tinv[h, d], rhs[h, d])
        u_ref[d, 0, :, sl] = uw[:, :DN_D]
        w16 = uw[:, DN_D:].astype(BF16)
        for c in range(nck):
            wq_ref[d, 0, c, 0:CHUNK, sl] = w16[c * CHUNK:(c + 1) * CHUNK]


def dn_wy(q, k, v, proj, par):
    b, s, w = q.shape
    r = WY_ROWS
    nck = r // CHUNK
    nchunk = s // CHUNK
    seq = pl.BlockSpec((1, r, w), lambda bb, i: (bb, i, 0))
    dseq = pl.BlockSpec((2, 1, r, w), lambda bb, i: (0, bb, i, 0))
    return pl.pallas_call(
        _dn_wy_body,
        out_shape=(jax.ShapeDtypeStruct((2, b, s, w), F32),
                   jax.ShapeDtypeStruct((2, b, nchunk, 2 * CHUNK, w), BF16),
                   jax.ShapeDtypeStruct((2, b, nchunk, DN_D, DN_HEADS * CHUNK), BF16),
                   jax.ShapeDtypeStruct((2, b, s, DN_HEADS * CHUNK), BF16),
                   jax.ShapeDtypeStruct((2, b, s, w), F32)),
        grid=(b, s // r),
        in_specs=[seq, seq, seq,
                  pl.BlockSpec((1, r, LANES), lambda bb, i: (bb, i, OFF_SMALL // LANES)),
                  pl.BlockSpec((2, LANES), lambda bb, i: (0, 0))],
        out_specs=(dseq,
                   pl.BlockSpec((2, 1, nck, 2 * CHUNK, w), lambda bb, i: (0, bb, i, 0, 0)),
                   pl.BlockSpec((2, 1, nck, DN_D, DN_HEADS * CHUNK), lambda bb, i: (0, bb, i, 0, 0)),
                   pl.BlockSpec((2, 1, r, DN_HEADS * CHUNK), lambda bb, i: (0, bb, i, 0)),
                   dseq),
        compiler_params=_cparams(("arbitrary", "arbitrary")),
        name="dn_wy",
    )(q, k, v, proj, par)


def _dn_rec_body(*refs, nc):
    ins = (refs[0:5], refs[5:10])
    outs = refs[10:12]
    state_ref = refs[12]
    n = pl.program_id(1)

    @pl.when(n == 0)
    def _():
        state_ref[...] = jnp.zeros_like(state_ref)

    chains = [(d, h) for d in range(2) for h in range(DN_HEADS)]
    state = {(d, h): state_ref[d, h] for d, h in chains}
    for step in range(nc):
        chunk = (step, nc - 1 - step)
        wq, v_new = {}, {}
        for d, h in chains:
            sl = slice(h * DN_D, (h + 1) * DN_D)
            wq[d, h] = jnp.dot(ins[d][1][0, 0, chunk[d], :, sl], state[d, h].astype(BF16),
                               preferred_element_type=F32)
        for d, h in chains:
            sl = slice(h * DN_D, (h + 1) * DN_D)
            rows = slice(chunk[d] * CHUNK, (chunk[d] + 1) * CHUNK)
            v_new[d, h] = (ins[d][0][0, 0, rows, sl] - wq[d, h][:CHUNK]).astype(BF16)
        for d, h in chains:
            sl = slice(h * DN_D, (h + 1) * DN_D)
            rows = slice(chunk[d] * CHUNK, (chunk[d] + 1) * CHUNK)
            qk = ins[d][3][0, 0, rows, h * CHUNK:(h + 1) * CHUNK]
            outs[d][0, rows, sl] = wq[d, h][CHUNK:] + jnp.dot(qk, v_new[d, h], preferred_element_type=F32)
        for d, h in chains:
            sl = slice(h * DN_D, (h + 1) * DN_D)
            first = chunk[d] * CHUNK
            ktt = ins[d][2][0, 0, chunk[d], :, h * CHUNK:(h + 1) * CHUNK]
            state[d, h] = (state[d, h] * ins[d][4][0, 0, first:first + 1, sl]
                           + jnp.dot(ktt, v_new[d, h], preferred_element_type=F32))
    for d, h in chains:
        state_ref[d, h] = state[d, h]


def dn_rec(u, wq, ktt, qk, egl):
    _, b, s, w = u.shape
    nchunk = s // CHUNK
    nc = max(c for c in (4, 2, 1) if nchunk % c == 0)
    nblk = nchunk // nc
    rows = nc * CHUNK

    def specs(d):
        def blk(n):
            return n if d == 0 else nblk - 1 - n
        return [pl.BlockSpec((1, 1, rows, w), lambda bb, n: (d, bb, blk(n), 0)),
                pl.BlockSpec((1, 1, nc, 2 * CHUNK, w), lambda bb, n: (d, bb, blk(n), 0, 0)),
                pl.BlockSpec((1, 1, nc, DN_D, DN_HEADS * CHUNK), lambda bb, n: (d, bb, blk(n), 0, 0)),
                pl.BlockSpec((1, 1, rows, DN_HEADS * CHUNK), lambda bb, n: (d, bb, blk(n), 0)),
                pl.BlockSpec((1, 1, rows, w), lambda bb, n: (d, bb, blk(n), 0))]

    out = jax.ShapeDtypeStruct((b, s, w), F32)
    return pl.pallas_call(
        functools.partial(_dn_rec_body, nc=nc),
        out_shape=(out, out),
        grid=(b, nblk),
        in_specs=specs(0) + specs(1),
        out_specs=(pl.BlockSpec((1, rows, w), lambda bb, n: (bb, n, 0)),
                   pl.BlockSpec((1, rows, w), lambda bb, n: (bb, nblk - 1 - n, 0))),
        scratch_shapes=[pltpu.VMEM((2, DN_HEADS, DN_D, DN_D), F32)],
        compiler_params=_cparams(("arbitrary", "arbitrary")),
        name="dn_rec",
    )(u, wq, ktt, qk, egl, u, wq, ktt, qk, egl)


def _rope_head(x, g, cos, sin_signed, first_half):
    xf = x * lax.rsqrt(jnp.mean(x * x, axis=-1, keepdims=True) + EPS) * g
    partner = jnp.where(first_half, pltpu.roll(xf, LANES - ROPE_SUB // 2, axis=1),
                        pltpu.roll(xf, ROPE_SUB // 2, axis=1))
    return xf * cos + partner * sin_signed


def _att_prep_body(q_ref, k_ref, v_ref, cos_ref, sin_ref, qg_ref, kg_ref, qo_ref, ko_ref, vo_ref, kn_ref):
    cos = cos_ref[...]
    sin_signed = sin_ref[...]
    lane = lax.broadcasted_iota(jnp.int32, cos.shape, 1)
    first_half = (lane % ROPE_SUB) < ROPE_SUB // 2
    scale = ATT_DH ** -0.5 * LOG2E
    for h in range(ATT_HEADS):
        sl = slice(h * ATT_DH, (h + 1) * ATT_DH)
        qo_ref[0, :, sl] = (_rope_head(q_ref[0, :, sl], qg_ref[...], cos, sin_signed, first_half)
                            * scale).astype(BF16)
    for h in range(ATT_KV_HEADS):
        sl = slice(h * ATT_DH, (h + 1) * ATT_DH)
        k16 = _rope_head(k_ref[0, :, sl], kg_ref[...], cos, sin_signed, first_half).astype(BF16)
        ko_ref[0, :, sl] = k16
        kf = k16.astype(F32)
        n2 = jnp.max(jnp.sum(kf * kf, axis=-1, keepdims=True), axis=0, keepdims=True)
        kn_ref[0, 0, :, sl] = jnp.broadcast_to(n2, (1, ATT_DH))
    for h in range(ATT_KV_HEADS):
        vo_ref[0, :, 2 * h * ATT_DH:(2 * h + 1) * ATT_DH] = v_ref[0, :, h * ATT_DH:(h + 1) * ATT_DH].astype(BF16)
        vo_ref[0, :, (2 * h + 1) * ATT_DH:(2 * h + 2) * ATT_DH] = jnp.ones((v_ref.shape[1], ATT_DH), BF16)


def att_prep(proj, cos, sin_signed, qg, kg, ts=512):
    b, s, _ = proj.shape
    ts = _tile(s, ts, 16)
    tab = pl.BlockSpec((ts, ATT_DH), lambda bb, i: (i, 0))
    vec = pl.BlockSpec((1, ATT_DH), lambda bb, i: (0, 0))
    return pl.pallas_call(
        _att_prep_body,
        out_shape=(jax.ShapeDtypeStruct((b, s, ATT_W), BF16),
                   jax.ShapeDtypeStruct((b, s, ATT_KV_W), BF16),
                   jax.ShapeDtypeStruct((b, s, 2 * ATT_KV_W), BF16),
                   jax.ShapeDtypeStruct((b, s // ts, 1, ATT_KV_W), F32)),
        grid=(b, s // ts),
        in_specs=[pl.BlockSpec((1, ts, ATT_W), lambda bb, i: (bb, i, OFF_AQ // ATT_W)),
                  pl.BlockSpec((1, ts, ATT_KV_W), lambda bb, i: (bb, i, OFF_AK // ATT_KV_W)),
                  pl.BlockSpec((1, ts, ATT_KV_W), lambda bb, i: (bb, i, OFF_AV // ATT_KV_W)),
                  tab, tab, vec, vec],
        out_specs=(pl.BlockSpec((1, ts, ATT_W), lambda bb, i: (bb, i, 0)),
                   pl.BlockSpec((1, ts, ATT_KV_W), lambda bb, i: (bb, i, 0)),
                   pl.BlockSpec((1, ts, 2 * ATT_KV_W), lambda bb, i: (bb, i, 0)),
                   pl.BlockSpec((1, 1, 1, ATT_KV_W), lambda bb, i: (bb, i, 0, 0))),
        compiler_params=_cparams(("arbitrary", "arbitrary")),
        name="att_prep",
    )(proj, proj, proj, cos, sin_signed, qg.reshape(1, ATT_DH), kg.reshape(1, ATT_DH))


MAX_FIXED_SHIFT = 60.0


def _flash_body(q_ref, k_ref, v_ref, kn_ref, o_ref, *, tk, nkv):
    tq = q_ref.shape[1]
    heads = range(ATT_GROUP)
    qs = [q_ref[0, :, h * ATT_DH:(h + 1) * ATT_DH] for h in heads]
    unroll = 2 if nkv % 2 == 0 else 1

    def tiles(t):
        start = pl.multiple_of(t * tk, tk)
        return k_ref[0, pl.ds(start, tk), :], v_ref[0, pl.ds(start, tk), :]

    def scores(h, ks):
        return lax.dot_general(qs[h], ks, (((1,), (1,)), ((), ())), preferred_element_type=F32)

    def finish(accs):
        for h in heads:
            o_ref[0, :, h * ATT_DH:(h + 1) * ATT_DH] = (accs[h][:, :ATT_DH] / accs[h][:, ATT_DH:]).astype(o_ref.dtype)

    kmax = jnp.sqrt(jnp.max(kn_ref[0, :, 0, :], axis=0, keepdims=True))[:, 0:1]
    shifts = []
    for h in heads:
        qf = qs[h].astype(F32)
        shifts.append(jnp.sqrt(jnp.sum(qf * qf, axis=-1, keepdims=True)) * (kmax * 1.001))
    worst = shifts[0]
    for h in heads[1:]:
        worst = jnp.maximum(worst, shifts[h])
    fixed_ok = jnp.max(worst) <= MAX_FIXED_SHIFT

    @pl.when(fixed_ok)
    def _():
        def step(t, accs):
            ks, vs = tiles(t)
            return tuple(accs[h] + jnp.dot(jnp.exp2(scores(h, ks) - shifts[h]).astype(BF16), vs,
                                           preferred_element_type=F32) for h in heads)

        init = tuple(jnp.zeros((tq, 2 * ATT_DH), F32) for _ in heads)
        finish(lax.fori_loop(0, nkv, step, init, unroll=unroll))

    @pl.when(jnp.logical_not(fixed_ok))
    def _():
        def step(t, carry):
            ks, vs = tiles(t)
            new = []
            for h in heads:
                m, acc = carry[h]
                sc = scores(h, ks)
                m_new = jnp.maximum(m, jnp.max(sc, axis=-1, keepdims=True))
                p = jnp.exp2(sc - m_new)
                acc = jnp.exp2(m - m_new) * acc + jnp.dot(p.astype(BF16), vs, preferred_element_type=F32)
                new.append((m_new, acc))
            return tuple(new)

        init = tuple((jnp.full((tq, 1), -1e30, F32), jnp.zeros((tq, 2 * ATT_DH), F32)) for _ in heads)
        res = lax.fori_loop(0, nkv, step, init, unroll=unroll)
        finish([res[h][1] for h in heads])


def flash_attention(q, k, v, kn, tq=512, tk=512):
    b, s, _ = q.shape
    tq = _tile(s, tq, 16)
    tk = _tile(s, tk, 16)
    gw = ATT_GROUP * ATT_DH
    return pl.pallas_call(
        functools.partial(_flash_body, tk=tk, nkv=s // tk),
        out_shape=jax.ShapeDtypeStruct((b, s, ATT_W), BF16),
        grid=(b, ATT_KV_HEADS, s // tq),
        in_specs=[pl.BlockSpec((1, tq, gw), lambda bb, g, i: (bb, i, g)),
                  pl.BlockSpec((1, s, ATT_DH), lambda bb, g, i: (bb, 0, g)),
                  pl.BlockSpec((1, s, 2 * ATT_DH), lambda bb, g, i: (bb, 0, g)),
                  pl.BlockSpec((1, kn.shape[1], 1, ATT_DH), lambda bb, g, i: (bb, 0, 0, g))],
        out_specs=pl.BlockSpec((1, tq, gw), lambda bb, g, i: (bb, i, g)),
        compiler_params=_cparams(("arbitrary", "arbitrary", "arbitrary")),
        name="flash_attention",
    )(q, k, v, kn)


def _gla_chunk(qs, k, v, gk, state_ref, d, reverse):
    c = CHUNK
    nsub = c // SUB
    ri = lax.broadcasted_iota(jnp.int32, (c, c), 0)
    ci = lax.broadcasted_iota(jnp.int32, (c, c), 1)
    incl = (ci >= ri) if reverse else (ri >= ci)
    gc = _hdot(incl.astype(F32), gk)
    last = 0 if reverse else c - 1
    gl = gc[last:last + 1, :]
    qd = qs * jnp.exp2(gc)
    kt = k * jnp.exp2(gl - gc)
    egl = jnp.exp2(gl)
    row = lax.broadcasted_iota(jnp.int32, (c, GLA_QK), 0)

    head_of_lane = lax.broadcasted_iota(jnp.int32, (SUB, GLA_QK), 1) // GLA_DK
    a_off = [[] for _ in range(GLA_HEADS)]
    for i in range(nsub):
        if i == (nsub - 1 if reverse else 0):
            for h in range(GLA_HEADS):
                a_off[h].append(jnp.zeros((SUB, c), F32))
            continue
        first = (i + 1) * SUB - 1 if reverse else i * SUB
        rs = slice(i * SUB, (i + 1) * SUB)
        ref = gc[first:first + 1, :]
        qi = qs[rs, :] * jnp.exp2(gc[rs, :] - ref)
        early = (row > first) if reverse else (row < first)
        kf = k * jnp.exp2(jnp.where(early, ref - gc, MASKED_LOG))
        stacked = jnp.concatenate([jnp.where(head_of_lane == h, qi, 0.0) for h in range(GLA_HEADS)], axis=0)
        prod = _bdot_nt(stacked, kf)
        for h in range(GLA_HEADS):
            a_off[h].append(prod[h * SUB:(h + 1) * SUB])

    lane = lax.broadcasted_iota(jnp.int32, (LANES, 2 * GLA_DV), 0)
    col = lax.broadcasted_iota(jnp.int32, (LANES, 2 * GLA_DV), 1)
    head_sum = ((lane // GLA_DK) == (col // GLA_DV)).astype(BF16)
    sub_row = lax.broadcasted_iota(jnp.int32, (SUB, LANES), 0)
    npair = GLA_HEADS // 2
    terms = []
    for p in range(npair):
        ls = slice(p * LANES, (p + 1) * LANES)
        for i in range(nsub):
            rs = slice(i * SUB, (i + 1) * SUB)
            gci, qsi = gc[rs, ls], qs[rs, ls]
            for jj in range(SUB):
                j = i * SUB + jj
                later = (sub_row <= jj) if reverse else (sub_row >= jj)
                e = jnp.exp2(jnp.where(later, gci - gc[j:j + 1, ls], MASKED_LOG))
                terms.append(qsi * k[j:j + 1, ls] * e)
    r = jnp.dot(jnp.concatenate(terms, axis=0).astype(BF16), head_sum, preferred_element_type=F32)
    o_diag = []
    for p in range(npair):
        vs = slice(p * 2 * GLA_DV, (p + 1) * 2 * GLA_DV)
        blocks = []
        for i in range(nsub):
            base = (p * nsub + i) * SUB * SUB
            acc = r[base:base + SUB, :] * v[i * SUB:i * SUB + 1, vs]
            for jj in range(1, SUB):
                acc = acc + r[base + jj * SUB:base + (jj + 1) * SUB, :] * v[i * SUB + jj:i * SUB + jj + 1, vs]
            blocks.append(acc)
        o_diag.append(jnp.concatenate(blocks, axis=0))

    outs = []
    for h in range(GLA_HEADS):
        sl = slice(h * GLA_DK, (h + 1) * GLA_DK)
        vh = v[:, h * GLA_DV:(h + 1) * GLA_DV]
        st = state_ref[d, h]
        a = jnp.concatenate(a_off[h], axis=0)
        od = o_diag[h // 2][:, (h % 2) * GLA_DV:(h % 2 + 1) * GLA_DV]
        outs.append(_bdot_nt(qd[:, sl], st) + _bdot(a, vh) + od)
        state_ref[d, h] = st * egl[:, sl] + lax.dot_general(
            vh.astype(BF16), kt[:, sl].astype(BF16), (((0,), (0,)), ((), ())), preferred_element_type=F32)
    return jnp.concatenate(outs, axis=1)


def _gla_scan_body(qf_ref, kf_ref, vf_ref, smf_ref, qb_ref, kb_ref, vb_ref, smb_ref, up_ref, upb_ref,
                   of_ref, ob_ref, state_ref):
    n = pl.program_id(1)

    @pl.when(n == 0)
    def _():
        state_ref[...] = jnp.zeros_like(state_ref)

    ins = ((qf_ref, kf_ref, vf_ref, smf_ref, of_ref), (qb_ref, kb_ref, vb_ref, smb_ref, ob_ref))
    for d, (q_ref, k_ref, v_ref, sm_ref, o_ref) in enumerate(ins):
        gk = jax.nn.log_sigmoid(_hdot(sm_ref[0], up_ref[d]) + upb_ref[d]) * (LOG2E / GLA_NORMALIZER)
        qs = q_ref[0] * (GLA_DK ** -0.5)
        o_ref[0] = _gla_chunk(qs, k_ref[0], v_ref[0], gk, state_ref, d, reverse=(d == 1))


def gla_scan(proj, up_pad, upb):
    b, s, _ = proj.shape
    nchunk = s // CHUNK

    def specs(d):
        def cidx(n):
            return n if d == 0 else nchunk - 1 - n
        return [pl.BlockSpec((1, CHUNK, GLA_QK), lambda bb, n: (bb, cidx(n), OFF_GQ // GLA_QK)),
                pl.BlockSpec((1, CHUNK, GLA_QK), lambda bb, n: (bb, cidx(n), OFF_GK // GLA_QK)),
                pl.BlockSpec((1, CHUNK, GLA_W), lambda bb, n: (bb, cidx(n), OFF_GV // GLA_W)),
                pl.BlockSpec((1, CHUNK, LANES), lambda bb, n: (bb, cidx(n), OFF_SMALL // LANES))]

    out = jax.ShapeDtypeStruct((b, s, GLA_W), F32)
    return pl.pallas_call(
        _gla_scan_body,
        out_shape=(out, out),
        grid=(b, nchunk),
        in_specs=specs(0) + specs(1) + [pl.BlockSpec((2, LANES, GLA_QK), lambda bb, n: (0, 0, 0)),
                                        pl.BlockSpec((2, 1, GLA_QK), lambda bb, n: (0, 0, 0))],
        out_specs=(pl.BlockSpec((1, CHUNK, GLA_W), lambda bb, n: (bb, n, 0)),
                   pl.BlockSpec((1, CHUNK, GLA_W), lambda bb, n: (bb, nchunk - 1 - n, 0))),
        scratch_shapes=[pltpu.VMEM((2, GLA_HEADS, GLA_DV, GLA_DK), F32)],
        compiler_params=_cparams(("arbitrary", "arbitrary")),
        name="gla_scan",
    )(proj, proj, proj, proj, proj, proj, proj, proj, up_pad, upb)


def _gated_norm(o, gate, g, heads, d):
    outs = []
    for h in range(heads):
        sl = slice(h * d, (h + 1) * d)
        t = o[:, sl]
        y = t * lax.rsqrt(jnp.mean(t * t, axis=-1, keepdims=True) + EPS) * g
        gt = gate[:, sl]
        outs.append(y * (gt * jax.nn.sigmoid(gt)))
    return jnp.concatenate(outs, axis=1)


def _merge_body(dnf_ref, dnb_ref, dgate_ref, dng_ref, att_ref, glf_ref, glb_ref, ggate_ref, glg_ref, o_ref):
    o_dn = _gated_norm(dnf_ref[0] + dnb_ref[0], dgate_ref[0], dng_ref[...], DN_HEADS, DN_D)
    o_gla = _gated_norm(glf_ref[0] + glb_ref[0], ggate_ref[0], glg_ref[...], GLA_HEADS, GLA_DV)
    o_ref[0, :, 0:DN_W] = o_dn.astype(BF16)
    o_ref[0, :, DN_W:DN_W + ATT_W] = att_ref[0]
    o_ref[0, :, DN_W + ATT_W:] = o_gla.astype(BF16)


def merge_heads(dn_f, dn_b, proj, dn_g, att, gl_f, gl_b, gla_g, ts=512):
    b, s, _ = proj.shape
    ts = _tile(s, ts, 16)
    mix_w = DN_W + ATT_W + GLA_W

    def spec(w, blk=0):
        return pl.BlockSpec((1, ts, w), lambda bb, i: (bb, i, blk))

    vec = pl.BlockSpec((1, LANES), lambda bb, i: (0, 0))
    return pl.pallas_call(
        _merge_body,
        out_shape=jax.ShapeDtypeStruct((b, s, mix_w), BF16),
        grid=(b, s // ts),
        in_specs=[spec(DN_W), spec(DN_W), spec(DN_W, OFF_DGATE // DN_W), vec,
                  spec(ATT_W), spec(GLA_W), spec(GLA_W), spec(GLA_W, OFF_GGATE // GLA_W), vec],
        out_specs=spec(mix_w),
        compiler_params=_cparams(("arbitrary", "arbitrary")),
        name="merge_heads",
    )(dn_f, dn_b, proj, dn_g.reshape(1, DN_D), att, gl_f, gl_b, proj, gla_g.reshape(1, GLA_DV))


def _split_points():
    pts, acc = [], 0
    for sz in IN_SIZES[:-1]:
        acc += sz
        pts.append(acc)
    return pts


def _relayout_w_in(w):
    d = w.shape[0]
    (dq, dk, dv, dgate, a_f, a_b, b_f, b_b, aq, ak, av, gq, gkk, gv, ggate, lr_f, lr_b) = jnp.split(
        w, _split_points(), axis=1)
    small = jnp.concatenate([a_f, a_b, b_f, b_b, lr_f, lr_b], axis=1)
    small = jnp.pad(small, ((0, 0), (0, LANES - small.shape[1])))
    cols = jnp.concatenate([dq, dk, dv, dgate, aq, ak, av, gq, gv, ggate, gkk, small,
                            jnp.zeros((d, PROJ_COLS - OFF_SMALL - LANES), w.dtype)], axis=1)
    return cols.astype(BF16)[None, None]


def _rope_tables(s):
    rows = s // GRID_W
    row = jnp.repeat(jnp.arange(rows, dtype=jnp.int32), GRID_W).astype(F32)
    col = jnp.tile(jnp.arange(GRID_W, dtype=jnp.int32), rows).astype(F32)
    inv_freq = ROPE_THETA ** (-jnp.arange(0, ROPE_SUB, 2, dtype=F32) / ROPE_SUB)
    ang_r = row[:, None] * inv_freq[None, :]
    ang_c = col[:, None] * inv_freq[None, :]
    cos = jnp.concatenate([jnp.cos(ang_r), jnp.cos(ang_r), jnp.cos(ang_c), jnp.cos(ang_c)], axis=1)
    sin = jnp.concatenate([-jnp.sin(ang_r), jnp.sin(ang_r), -jnp.sin(ang_c), jnp.sin(ang_c)], axis=1)
    return cos, sin


def _dn_params(dn_a_log, dn_dt_bias):
    rows = jnp.stack([dn_a_log.reshape(-1), dn_dt_bias.reshape(-1)], axis=0)
    return jnp.pad(rows, ((0, 0), (SM_A, LANES - SM_A - 2 * DN_HEADS)))


def _gla_params(gla_up, gla_up_b):
    ups = []
    for d in range(2):
        lo = SM_LR + d * GLA_RANK
        ups.append(jnp.pad(gla_up[d], ((lo, LANES - lo - GLA_RANK), (0, 0))))
    return jnp.stack(ups, axis=0), gla_up_b[:, None, :]


def _mixer(x16, bsz, s, w_in, dn_conv, dn_a_log, dn_dt_bias, dn_norm_g, att_qn_g, att_kn_g,
           gla_up, gla_up_b, gla_norm_g, rope):
    t = bsz * s
    bm = _tile(t, 512, 16)
    proj = grouped_matmul(x16, _relayout_w_in(w_in), 0, jnp.zeros((t // bm,), jnp.int32), bm, 1024, F32)
    proj = proj.reshape(bsz, s, PROJ_COLS)

    q, k, v = dn_prep(proj, dn_conv)
    u, wq, ktt, qk, egl = dn_wy(q, k, v, proj, _dn_params(dn_a_log, dn_dt_bias))
    dn_f, dn_b = dn_rec(u, wq, ktt, qk, egl)

    cos, sin_signed = rope
    aq, ak, av, kn = att_prep(proj, cos, sin_signed, att_qn_g, att_kn_g)
    o_att = flash_attention(aq, ak, av, kn)

    gl_f, gl_b = gla_scan(proj, *_gla_params(gla_up, gla_up_b))

    return merge_heads(dn_f, dn_b, proj, dn_norm_g, o_att, gl_f, gl_b, gla_norm_g).reshape(t, -1)


def _moe(x32, x16, router_w, w_gate, w_up, w_down, layer, ln_g, ln_b, alpha):
    t, d = x32.shape
    rw = jnp.pad(router_w, ((0, 0), (0, LANES - N_EXPERTS)))
    logits = router_logits(x32, rw)[:, :N_EXPERTS]
    top_val, top_idx = lax.top_k(logits, TOP_K)
    gates = jax.nn.softmax(top_val, axis=-1)
    e_flat = top_idx.reshape(-1).astype(jnp.int32)
    tok_flat = jnp.repeat(jnp.arange(t, dtype=jnp.int32), TOP_K)
    onehot = (e_flat[:, None] == jnp.arange(N_EXPERTS, dtype=jnp.int32)[None, :]).astype(jnp.int32)
    csum = jnp.cumsum(onehot, axis=0)
    counts = csum[-1]
    rank = jnp.take_along_axis(csum, e_flat[:, None], axis=1)[:, 0] - 1
    padded = (counts + MOE_BLOCK - 1) // MOE_BLOCK * MOE_BLOCK
    pstart = jnp.cumsum(padded) - padded
    pend = pstart + padded
    dest = pstart[e_flat] + rank
    nb = -(-(TOP_K * t) // MOE_BLOCK) + N_EXPERTS
    cap = nb * MOE_BLOCK
    buf_tok = jnp.zeros((cap,), jnp.int32).at[dest].set(tok_flat)
    block_start = jnp.arange(nb, dtype=jnp.int32) * MOE_BLOCK
    block_exp = jnp.minimum(jnp.sum(block_start[:, None] >= pend[None, :], axis=-1),
                            N_EXPERTS - 1).astype(jnp.int32)
    xb = x16[buf_tok]
    h = grouped_swiglu(xb, w_gate, w_up, layer, block_exp, MOE_BLOCK, 1024)
    yb = grouped_matmul(h, w_down, layer, block_exp, MOE_BLOCK, 512, BF16)
    dest_by_choice = dest.reshape(t, TOP_K).T.reshape(-1)
    return combine_ln(x32, yb[dest_by_choice], gates, ln_g, ln_b, alpha)


def _dense_ffn(x32, x16, w_gate, w_up, w_down, layer, ln_g, ln_b, alpha):
    t = x32.shape[0]
    bm = _tile(t, 512, 16)
    grp = jnp.zeros((t // bm,), jnp.int32)
    h = grouped_swiglu(x16, w_gate, w_up, layer, grp, bm, 512)
    return matmul_res_ln(h, w_down[layer].astype(BF16), x32, ln_g, ln_b, alpha)


def kernel(x, w_in, dn_conv, dn_a_log, dn_dt_bias, dn_norm_g, att_qn_g, att_kn_g, gla_up, gla_up_b,
           gla_norm_g, w_out, ln1_g, ln1_b, ln2_g, ln2_b, ffn_w_gate, ffn_w_up, ffn_w_down, router_w,
           exp_w_gate, exp_w_up, exp_w_down):
    bsz, s, d = x.shape
    depth = w_in.shape[0]
    alpha = (2.0 * depth) ** 0.25
    t = bsz * s
    rope = _rope_tables(s)
    x32 = x.reshape(t, d)
    x16 = x32.astype(BF16)
    for layer in range(depth):
        mix = _mixer(x16, bsz, s, w_in[layer], dn_conv[layer], dn_a_log[layer], dn_dt_bias[layer],
                     dn_norm_g[layer], att_qn_g[layer], att_kn_g[layer], gla_up[layer],
                     gla_up_b[layer], gla_norm_g[layer], rope)
        x32, x16 = matmul_res_ln(mix, w_out[layer].astype(BF16), x32, ln1_g[layer], ln1_b[layer], alpha)
        j = layer // 2
        if layer % 2 == 0:
            x32, x16 = _dense_ffn(x32, x16, ffn_w_gate, ffn_w_up, ffn_w_down, j,
                                  ln2_g[layer], ln2_b[layer], alpha)
        else:
            x32, x16 = _moe(x32, x16, router_w[j], exp_w_gate, exp_w_up, exp_w_down, j,
                            ln2_g[layer], ln2_b[layer], alpha)
    return x32.reshape(bsz, s, d)
```

```python
import functools
import math

import jax
import jax.numpy as jnp
from jax import lax
from jax.experimental import pallas as pl
from jax.experimental.pallas import tpu as pltpu

F32 = jnp.float32
BF16 = jnp.bfloat16
HIGHEST = lax.Precision.HIGHEST

DN_HEADS, DN_D = 6, 128
ATT_HEADS, ATT_KV_HEADS, ATT_DH = 6, 2, 128
ATT_GROUP = ATT_HEADS // ATT_KV_HEADS
ROPE_SUB, ROPE_THETA, GRID_W = 64, 10000.0, 64
GLA_HEADS, GLA_DK, GLA_DV, GLA_RANK = 4, 64, 128, 16
GLA_NORMALIZER = 16.0
CHUNK = 64
SUB = 8
MASKED_LOG = -1e30
WY_ROWS = 2 * CHUNK
EGL_ROWS = 8
CONV_W = 5
N_EXPERTS, TOP_K, MOE_BLOCK = 8, 2, 256
EPS = 1e-6
LOG2E = 1.4426950408889634
LANES = 128
HALO = 16

DN_W = DN_HEADS * DN_D
ATT_W = ATT_HEADS * ATT_DH
ATT_KV_W = ATT_KV_HEADS * ATT_DH
GLA_QK = GLA_HEADS * GLA_DK
GLA_W = GLA_HEADS * GLA_DV
IN_SIZES = (DN_W, DN_W, DN_W, DN_W, DN_HEADS, DN_HEADS, DN_HEADS, DN_HEADS,
            ATT_W, ATT_KV_W, ATT_KV_W, GLA_QK, GLA_QK, GLA_W, GLA_W, GLA_RANK, GLA_RANK)
OFF_DQ, OFF_DGATE, OFF_AQ, OFF_AK, OFF_AV = 0, 2304, 3072, 3840, 4096
OFF_GQ, OFF_GV, OFF_GGATE, OFF_GK, OFF_PAD = 4352, 4608, 5120, 5632, 5888
PROJ_COLS = 6144
SM_A, SM_B, SM_LR = 0, 2 * DN_HEADS, 4 * DN_HEADS

VMEM_LIMIT = 56 * 1024 * 1024


def _cparams(sem, vmem=VMEM_LIMIT):
    return pltpu.CompilerParams(dimension_semantics=sem, vmem_limit_bytes=vmem)


def _tile(n, pref, quantum=LANES):
    if n <= pref:
        return n
    t = pref - pref % quantum
    while n % t:
        t -= quantum
    return t


def _bdot(a, b):
    return jnp.dot(a.astype(BF16), b.astype(BF16), preferred_element_type=F32)


def _bdot_nt(a, b):
    return lax.dot_general(a.astype(BF16), b.astype(BF16), (((1,), (1,)), ((), ())),
                           preferred_element_type=F32)


def _hdot(a, b):
    return jnp.dot(a, b, preferred_element_type=F32, precision=HIGHEST)


def _block_table(group, used=None):
    used = jnp.ones_like(group) if used is None else used.astype(jnp.int32)
    return jnp.stack([group.astype(jnp.int32), used], axis=0)


def _group_changed(be_ref):
    i = pl.program_id(1)
    return (i == 0) | (be_ref[0, i] != be_ref[0, jnp.maximum(i - 1, 0)])


def _if_used(be_ref, o_ref, compute):
    used = be_ref[1, pl.program_id(1)] != 0

    @pl.when(used)
    def _():
        compute()

    @pl.when(jnp.logical_not(used))
    def _():
        o_ref[...] = jnp.zeros_like(o_ref)


def _gmm_body(be_ref, x_ref, w_ref, o_ref, *w16):
    if w16:
        @pl.when(_group_changed(be_ref))
        def _():
            w16[0][...] = w_ref[...].astype(BF16)
    w_use = w16[0] if w16 else w_ref

    def compute():
        o_ref[...] = jnp.dot(x_ref[...], w_use[...], preferred_element_type=F32).astype(o_ref.dtype)

    _if_used(be_ref, o_ref, compute)


def _weight_spec(w, layer, k, tn):
    if w.ndim == 3:
        return pl.BlockSpec((None, k, tn), lambda j, i, be: (layer, 0, j))
    return pl.BlockSpec((None, None, k, tn), lambda j, i, be: (layer, be[0, i], 0, j))


def grouped_matmul(x, w, layer, block_table, bm, tn, out_dtype):
    m, k = x.shape
    n = w.shape[-1]
    tn = _tile(n, tn)
    scratch = [pltpu.VMEM((k, tn), BF16)] if w.dtype != BF16 else []
    return pl.pallas_call(
        _gmm_body,
        out_shape=jax.ShapeDtypeStruct((m, n), out_dtype),
        grid_spec=pltpu.PrefetchScalarGridSpec(
            num_scalar_prefetch=1, grid=(n // tn, m // bm),
            in_specs=[pl.BlockSpec((bm, k), lambda j, i, be: (i, 0)), _weight_spec(w, layer, k, tn)],
            out_specs=pl.BlockSpec((bm, tn), lambda j, i, be: (i, j)),
            scratch_shapes=scratch),
        compiler_params=_cparams(("arbitrary", "arbitrary")),
        name="grouped_matmul",
    )(block_table, x, w)


def _gswiglu_body(be_ref, x_ref, wg_ref, wu_ref, o_ref, wg16, wu16):
    @pl.when(_group_changed(be_ref))
    def _():
        wg16[...] = wg_ref[...].astype(BF16)
        wu16[...] = wu_ref[...].astype(BF16)

    def compute():
        x = x_ref[...]
        g = jnp.dot(x, wg16[...], preferred_element_type=F32)
        u = jnp.dot(x, wu16[...], preferred_element_type=F32)
        o_ref[...] = (g * jax.nn.sigmoid(g) * u).astype(o_ref.dtype)

    _if_used(be_ref, o_ref, compute)


def grouped_swiglu(x, wg, wu, layer, block_table, bm, tn):
    m, k = x.shape
    n = wg.shape[-1]
    tn = _tile(n, tn)
    return pl.pallas_call(
        _gswiglu_body,
        out_shape=jax.ShapeDtypeStruct((m, n), BF16),
        grid_spec=pltpu.PrefetchScalarGridSpec(
            num_scalar_prefetch=1, grid=(n // tn, m // bm),
            in_specs=[pl.BlockSpec((bm, k), lambda j, i, be: (i, 0)),
                      _weight_spec(wg, layer, k, tn), _weight_spec(wu, layer, k, tn)],
            out_specs=pl.BlockSpec((bm, tn), lambda j, i, be: (i, j)),
            scratch_shapes=[pltpu.VMEM((k, tn), BF16), pltpu.VMEM((k, tn), BF16)]),
        compiler_params=_cparams(("arbitrary", "arbitrary")),
        name="grouped_swiglu",
    )(block_table, x, wg, wu)


def _res_ln(res, y, g, b, alpha):
    z = alpha * res + y
    mu = jnp.mean(z, axis=-1, keepdims=True)
    zc = z - mu
    var = jnp.mean(zc * zc, axis=-1, keepdims=True)
    return zc * lax.rsqrt(var + EPS) * g + b


def _mm_res_ln_body(x_ref, w_ref, res_ref, g_ref, b_ref, o32_ref, o16_ref, *acc, nk, alpha):
    kk = pl.program_id(1)
    part = jnp.dot(x_ref[...], w_ref[...], preferred_element_type=F32)

    def finish(y):
        out = _res_ln(res_ref[...], y, g_ref[...], b_ref[...], alpha)
        o32_ref[...] = out
        o16_ref[...] = out.astype(BF16)

    if nk == 1:
        finish(part)
        return
    acc_ref = acc[0]

    @pl.when(kk == 0)
    def _():
        acc_ref[...] = part

    @pl.when((kk > 0) & (kk < nk - 1))
    def _():
        acc_ref[...] += part

    @pl.when(kk == nk - 1)
    def _():
        finish(acc_ref[...] + part)


def matmul_res_ln(x, w, res, g, b, alpha, tm=512, tk=2048):
    m, k = x.shape
    n = w.shape[-1]
    tm = _tile(m, tm, 8)
    tk = _tile(k, tk)
    nk = k // tk
    return pl.pallas_call(
        functools.partial(_mm_res_ln_body, nk=nk, alpha=alpha),
        out_shape=(jax.ShapeDtypeStruct((m, n), F32), jax.ShapeDtypeStruct((m, n), BF16)),
        grid=(m // tm, nk),
        in_specs=[pl.BlockSpec((tm, tk), lambda i, kk: (i, kk)),
                  pl.BlockSpec((tk, n), lambda i, kk: (kk, 0)),
                  pl.BlockSpec((tm, n), lambda i, kk: (i, 0)),
                  pl.BlockSpec((1, n), lambda i, kk: (0, 0)),
                  pl.BlockSpec((1, n), lambda i, kk: (0, 0))],
        out_specs=(pl.BlockSpec((tm, n), lambda i, kk: (i, 0)),
                   pl.BlockSpec((tm, n), lambda i, kk: (i, 0))),
        scratch_shapes=[pltpu.VMEM((tm, n), F32)] if nk > 1 else [],
        compiler_params=_cparams(("arbitrary", "arbitrary")),
        name="matmul_res_ln",
    )(x, w, res, g.reshape(1, n), b.reshape(1, n))


def _combine_ln_body(res_ref, *refs, alpha):
    y_refs, (gate_ref, g_ref, b_ref, o32_ref, o16_ref) = refs[:TOP_K], refs[TOP_K:]
    gate = gate_ref[...]
    y = y_refs[0][...].astype(F32) * gate[:, 0:1]
    for kk in range(1, TOP_K):
        y = y + y_refs[kk][...].astype(F32) * gate[:, kk:kk + 1]
    out = _res_ln(res_ref[...], y, g_ref[...], b_ref[...], alpha)
    o32_ref[...] = out
    o16_ref[...] = out.astype(BF16)


def combine_ln(res, y, gates, g, b, alpha, tm=512):
    m, n = res.shape
    tm = _tile(m, tm, 16)
    nblk = m // tm
    row = pl.BlockSpec((tm, n), lambda i: (i, 0))
    vec = pl.BlockSpec((1, n), lambda i: (0, 0))

    def choice(kk):
        return pl.BlockSpec((tm, n), lambda i: (kk * nblk + i, 0))

    return pl.pallas_call(
        functools.partial(_combine_ln_body, alpha=alpha),
        out_shape=(jax.ShapeDtypeStruct((m, n), F32), jax.ShapeDtypeStruct((m, n), BF16)),
        grid=(nblk,),
        in_specs=[row] + [choice(kk) for kk in range(TOP_K)] + [pl.BlockSpec((tm, TOP_K), lambda i: (i, 0)), vec, vec],
        out_specs=(row, row),
        compiler_params=_cparams(("arbitrary",)),
        name="combine_ln",
    )(res, *([y] * TOP_K), gates, g.reshape(1, n), b.reshape(1, n))


def _router_body(x_ref, w_ref, o_ref):
    o_ref[...] = _hdot(x_ref[...], w_ref[...])


def router_logits(x, w_pad, tm=512):
    m, k = x.shape
    n = w_pad.shape[-1]
    tm = _tile(m, tm, 8)
    return pl.pallas_call(
        _router_body,
        out_shape=jax.ShapeDtypeStruct((m, n), F32),
        grid=(m // tm,),
        in_specs=[pl.BlockSpec((tm, k), lambda i: (i, 0)), pl.BlockSpec((k, n), lambda i: (0, 0))],
        out_specs=pl.BlockSpec((tm, n), lambda i: (i, 0)),
        compiler_params=_cparams(("arbitrary",)),
        name="router_logits",
    )(x, w_pad)


def _dn_prep_body(cur_ref, prev_ref, next_ref, w_ref, q_ref, k_ref, v_ref, buf_ref, *, ts, nblk):
    i = pl.program_id(1)
    buf_ref[0:HALO, :] = jnp.where(i > 0, prev_ref[0].astype(F32), 0.0)
    buf_ref[HALO:HALO + ts, :] = cur_ref[0].astype(F32)
    buf_ref[HALO + ts:2 * HALO + ts, :] = jnp.where(i < nblk - 1, next_ref[0].astype(F32), 0.0)
    base = HALO - CONV_W // 2
    acc = buf_ref[base:base + ts, :] * w_ref[0:1, :]
    for j in range(1, CONV_W):
        acc = acc + buf_ref[base + j:base + j + ts, :] * w_ref[j:j + 1, :]
    y = acc * jax.nn.sigmoid(acc)
    for h in range(DN_HEADS):
        for off, ref in ((0, q_ref), (DN_W, k_ref)):
            t = y[:, off + h * DN_D:off + (h + 1) * DN_D]
            ref[0, :, h * DN_D:(h + 1) * DN_D] = t * lax.rsqrt(jnp.sum(t * t, axis=-1, keepdims=True) + EPS)
    v_ref[0] = y[:, 2 * DN_W:3 * DN_W]


def dn_prep(proj, conv_w, ts=256):
    b, s, _ = proj.shape
    ts = _tile(s, ts, HALO)
    nblk = s // ts
    c = 3 * DN_W
    hb = ts // HALO
    out = jax.ShapeDtypeStruct((b, s, DN_W), F32)
    ospec = pl.BlockSpec((1, ts, DN_W), lambda bb, i: (bb, i, 0))
    return pl.pallas_call(
        functools.partial(_dn_prep_body, ts=ts, nblk=nblk),
        out_shape=(out, out, out),
        grid=(b, nblk),
        in_specs=[pl.BlockSpec((1, ts, c), lambda bb, i: (bb, i, 0)),
                  pl.BlockSpec((1, HALO, c), lambda bb, i: (bb, jnp.maximum(i * hb - 1, 0), 0)),
                  pl.BlockSpec((1, HALO, c), lambda bb, i: (bb, jnp.minimum((i + 1) * hb, nblk * hb - 1), 0)),
                  pl.BlockSpec((CONV_W, c), lambda bb, i: (0, 0))],
        out_specs=(ospec, ospec, ospec),
        scratch_shapes=[pltpu.VMEM((ts + 2 * HALO, c), F32)],
        compiler_params=_cparams(("arbitrary", "arbitrary")),
        name="dn_prep",
    )(proj, proj, proj, conv_w)


def _dn_wy_body(q_ref, k_ref, v_ref, sm_ref, par_ref, u_ref, wq_ref, ktt_ref, qk_ref, egl_ref):
    r = WY_ROWS
    nck = r // CHUNK
    sm = sm_ref[0]
    par = par_ref[...]
    g_all = -jnp.exp(par[0:1, :]) * jax.nn.softplus(sm + par[1:2, :])
    beta_all = jax.nn.sigmoid(sm)
    ri = lax.broadcasted_iota(jnp.int32, (r, r), 0)
    ci = lax.broadcasted_iota(jnp.int32, (r, r), 1)
    same = (ri // CHUNK) == (ci // CHUNK)
    eye = (ri == ci).astype(F32)
    gtot = _hdot(same.astype(F32), g_all)
    masks, gcs = [], []
    for d in range(2):
        delta = ri - ci if d == 0 else ci - ri
        incl = same & (delta >= 0)
        strict = same & (delta > 0)
        gc = _hdot(incl.astype(F32), g_all)
        masks.append((incl, strict))
        gcs.append((gc, gc.T))
    scale = DN_D ** -0.5
    chains = [(h, d) for h in range(DN_HEADS) for d in range(2)]
    qk_cols = ([], [])
    nm, tinv, rhs = {}, {}, {}
    for h in range(DN_HEADS):
        sl = slice(h * DN_D, (h + 1) * DN_D)
        q = q_ref[0, :, sl] * scale
        k = k_ref[0, :, sl]
        v = v_ref[0, :, sl]
        kq = _bdot_nt(jnp.concatenate([k, q], axis=0), k)
        kk, qk = kq[:r], kq[r:]
        for d in range(2):
            lane = SM_A + d * DN_HEADS + h
            incl, strict = masks[d]
            gc, gct = gcs[d]
            gcol = jnp.broadcast_to(gc[:, lane:lane + 1], (r, r))
            gt = jnp.broadcast_to(gtot[:, lane:lane + 1], (r, r))
            bcol = jnp.broadcast_to(beta_all[:, SM_B + d * DN_HEADS + h:SM_B + d * DN_HEADS + h + 1], (r, r))
            grow = gct[lane:lane + 1, :]
            decay = jnp.where(incl, jnp.exp(jnp.where(incl, gcol - grow, 0.0)), 0.0)
            nm[h, d] = jnp.where(strict, -(bcol * kk) * decay, 0.0)
            tinv[h, d] = eye + nm[h, d]
            eg = jnp.exp(gcol)
            rhs[h, d] = jnp.concatenate([v * bcol, k * (bcol * eg)], axis=1).astype(BF16)
            qd16 = (q * eg).astype(BF16)
            ktt = _bdot_nt(eye, (k * jnp.exp(gt - gcol)).astype(BF16)).astype(BF16)
            for c in range(nck):
                rows = slice(c * CHUNK, (c + 1) * CHUNK)
                wq_ref[d, 0, c, CHUNK:2 * CHUNK, sl] = qd16[rows]
                ktt_ref[d, 0, c, :, h * CHUNK:(h + 1) * CHUNK] = ktt[:, rows]
            qkd = qk * decay
            qk_cols[d].append(qkd[:, :CHUNK] + qkd[:, CHUNK:])
            egt = jnp.exp(gt)
            for c in range(nck):
                egl_ref[d, 0, c, :, sl] = egt[c * CHUNK:c * CHUNK + EGL_ROWS]
    for d in range(2):
        qk_ref[d, 0] = jnp.concatenate(qk_cols[d], axis=1).astype(BF16)
    for _ in range(int(math.log2(CHUNK)) - 1):
        for hd in chains:
            nm[hd] = _bdot(nm[hd], nm[hd])
        for hd in chains:
            tinv[hd] = tinv[hd] + _bdot(tinv[hd], nm[hd])
    for h, d in chains:
        sl = slice(h * DN_D, (h + 1) * DN_D)
        uw = _bdot(tinv[h, d], rhs[h, d])
        u_ref[d, 0, :, sl] = uw[:, :DN_D].astype(BF16)
        w16 = uw[:, DN_D:].astype(BF16)
        for c in range(nck):
            wq_ref[d, 0, c, 0:CHUNK, sl] = w16[c * CHUNK:(c + 1) * CHUNK]


def dn_wy(q, k, v, small, par):
    b, s, w = q.shape
    r = WY_ROWS
    nck = r // CHUNK
    nchunk = s // CHUNK
    seq = pl.BlockSpec((1, r, w), lambda bb, i: (bb, i, 0))
    dseq = pl.BlockSpec((2, 1, r, w), lambda bb, i: (0, bb, i, 0))
    return pl.pallas_call(
        _dn_wy_body,
        out_shape=(jax.ShapeDtypeStruct((2, b, s, w), BF16),
                   jax.ShapeDtypeStruct((2, b, nchunk, 2 * CHUNK, w), BF16),
                   jax.ShapeDtypeStruct((2, b, nchunk, DN_D, DN_HEADS * CHUNK), BF16),
                   jax.ShapeDtypeStruct((2, b, s, DN_HEADS * CHUNK), BF16),
                   jax.ShapeDtypeStruct((2, b, nchunk, EGL_ROWS, w), F32)),
        grid=(b, s // r),
        in_specs=[seq, seq, seq,
                  pl.BlockSpec((1, r, LANES), lambda bb, i: (bb, i, 0)),
                  pl.BlockSpec((2, LANES), lambda bb, i: (0, 0))],
        out_specs=(dseq,
                   pl.BlockSpec((2, 1, nck, 2 * CHUNK, w), lambda bb, i: (0, bb, i, 0, 0)),
                   pl.BlockSpec((2, 1, nck, DN_D, DN_HEADS * CHUNK), lambda bb, i: (0, bb, i, 0, 0)),
                   pl.BlockSpec((2, 1, r, DN_HEADS * CHUNK), lambda bb, i: (0, bb, i, 0)),
                   pl.BlockSpec((2, 1, nck, EGL_ROWS, w), lambda bb, i: (0, bb, i, 0, 0))),
        compiler_params=_cparams(("arbitrary", "arbitrary")),
        name="dn_wy",
    )(q, k, v, small, par)


def _dn_rec_body(*refs, nc):
    ins = (refs[0:5], refs[5:10])
    outs = refs[10:12]
    state_ref = refs[12]
    n = pl.program_id(1)

    @pl.when(n == 0)
    def _():
        state_ref[...] = jnp.zeros_like(state_ref)

    chains = [(d, h) for d in range(2) for h in range(DN_HEADS)]
    state = {(d, h): state_ref[d, h] for d, h in chains}
    for step in range(nc):
        chunk = (step, nc - 1 - step)
        wq, v_new = {}, {}
        for d, h in chains:
            sl = slice(h * DN_D, (h + 1) * DN_D)
            wq[d, h] = jnp.dot(ins[d][1][0, 0, chunk[d], :, sl], state[d, h].astype(BF16),
                               preferred_element_type=F32)
        for d, h in chains:
            sl = slice(h * DN_D, (h + 1) * DN_D)
            rows = slice(chunk[d] * CHUNK, (chunk[d] + 1) * CHUNK)
            v_new[d, h] = (ins[d][0][0, 0, rows, sl].astype(F32) - wq[d, h][:CHUNK]).astype(BF16)
        for d, h in chains:
            sl = slice(h * DN_D, (h + 1) * DN_D)
            rows = slice(chunk[d] * CHUNK, (chunk[d] + 1) * CHUNK)
            qk = ins[d][3][0, 0, rows, h * CHUNK:(h + 1) * CHUNK]
            outs[d][0, rows, sl] = (wq[d, h][CHUNK:]
                                    + jnp.dot(qk, v_new[d, h], preferred_element_type=F32)).astype(outs[d].dtype)
        for d, h in chains:
            sl = slice(h * DN_D, (h + 1) * DN_D)
            ktt = ins[d][2][0, 0, chunk[d], :, h * CHUNK:(h + 1) * CHUNK]
            state[d, h] = (state[d, h] * ins[d][4][0, 0, chunk[d], 0:1, sl]
                           + jnp.dot(ktt, v_new[d, h], preferred_element_type=F32))
    for d, h in chains:
        state_ref[d, h] = state[d, h]


def dn_rec(u, wq, ktt, qk, egl):
    _, b, s, w = u.shape
    nchunk = s // CHUNK
    nc = max(c for c in (4, 2, 1) if nchunk % c == 0)
    nblk = nchunk // nc
    rows = nc * CHUNK

    def specs(d):
        def blk(n):
            return n if d == 0 else nblk - 1 - n
        return [pl.BlockSpec((1, 1, rows, w), lambda bb, n: (d, bb, blk(n), 0)),
                pl.BlockSpec((1, 1, nc, 2 * CHUNK, w), lambda bb, n: (d, bb, blk(n), 0, 0)),
                pl.BlockSpec((1, 1, nc, DN_D, DN_HEADS * CHUNK), lambda bb, n: (d, bb, blk(n), 0, 0)),
                pl.BlockSpec((1, 1, rows, DN_HEADS * CHUNK), lambda bb, n: (d, bb, blk(n), 0)),
                pl.BlockSpec((1, 1, nc, EGL_ROWS, w), lambda bb, n: (d, bb, blk(n), 0, 0))]

    out = jax.ShapeDtypeStruct((b, s, w), BF16)
    return pl.pallas_call(
        functools.partial(_dn_rec_body, nc=nc),
        out_shape=(out, out),
        grid=(b, nblk),
        in_specs=specs(0) + specs(1),
        out_specs=(pl.BlockSpec((1, rows, w), lambda bb, n: (bb, n, 0)),
                   pl.BlockSpec((1, rows, w), lambda bb, n: (bb, nblk - 1 - n, 0))),
        scratch_shapes=[pltpu.VMEM((2, DN_HEADS, DN_D, DN_D), F32)],
        compiler_params=_cparams(("arbitrary", "arbitrary")),
        name="dn_rec",
    )(u, wq, ktt, qk, egl, u, wq, ktt, qk, egl)


def _rope_head(x, g, cos, sin_signed, first_half):
    x = x.astype(F32)
    xf = x * lax.rsqrt(jnp.mean(x * x, axis=-1, keepdims=True) + EPS) * g
    partner = jnp.where(first_half, pltpu.roll(xf, LANES - ROPE_SUB // 2, axis=1),
                        pltpu.roll(xf, ROPE_SUB // 2, axis=1))
    return xf * cos + partner * sin_signed


def _att_prep_body(q_ref, k_ref, v_ref, cos_ref, sin_ref, qg_ref, kg_ref, qo_ref, ko_ref, vo_ref, kn_ref):
    cos = cos_ref[...]
    sin_signed = sin_ref[...]
    lane = lax.broadcasted_iota(jnp.int32, cos.shape, 1)
    first_half = (lane % ROPE_SUB) < ROPE_SUB // 2
    scale = ATT_DH ** -0.5 * LOG2E
    for h in range(ATT_HEADS):
        sl = slice(h * ATT_DH, (h + 1) * ATT_DH)
        qo_ref[0, :, sl] = (_rope_head(q_ref[0, :, sl], qg_ref[...], cos, sin_signed, first_half)
                            * scale).astype(BF16)
    for h in range(ATT_KV_HEADS):
        sl = slice(h * ATT_DH, (h + 1) * ATT_DH)
        k16 = _rope_head(k_ref[0, :, sl], kg_ref[...], cos, sin_signed, first_half).astype(BF16)
        ko_ref[0, :, sl] = k16
        kf = k16.astype(F32)
        n2 = jnp.max(jnp.sum(kf * kf, axis=-1, keepdims=True), axis=0, keepdims=True)
        kn_ref[0, 0, :, sl] = jnp.broadcast_to(n2, (1, ATT_DH))
    for h in range(ATT_KV_HEADS):
        vo_ref[0, :, 2 * h * ATT_DH:(2 * h + 1) * ATT_DH] = v_ref[0, :, h * ATT_DH:(h + 1) * ATT_DH]
        vo_ref[0, :, (2 * h + 1) * ATT_DH:(2 * h + 2) * ATT_DH] = jnp.ones((v_ref.shape[1], ATT_DH), BF16)


def att_prep(proj, cos, sin_signed, qg, kg, ts=512):
    b, s, _ = proj.shape
    ts = _tile(s, ts, 16)
    tab = pl.BlockSpec((ts, ATT_DH), lambda bb, i: (i, 0))
    vec = pl.BlockSpec((1, ATT_DH), lambda bb, i: (0, 0))
    return pl.pallas_call(
        _att_prep_body,
        out_shape=(jax.ShapeDtypeStruct((b, s, ATT_W), BF16),
                   jax.ShapeDtypeStruct((b, s, ATT_KV_W), BF16),
                   jax.ShapeDtypeStruct((b, s, 2 * ATT_KV_W), BF16),
                   jax.ShapeDtypeStruct((b, s // ts, 1, ATT_KV_W), F32)),
        grid=(b, s // ts),
        in_specs=[pl.BlockSpec((1, ts, ATT_W), lambda bb, i: (bb, i, OFF_AQ // ATT_W)),
                  pl.BlockSpec((1, ts, ATT_KV_W), lambda bb, i: (bb, i, OFF_AK // ATT_KV_W)),
                  pl.BlockSpec((1, ts, ATT_KV_W), lambda bb, i: (bb, i, OFF_AV // ATT_KV_W)),
                  tab, tab, vec, vec],
        out_specs=(pl.BlockSpec((1, ts, ATT_W), lambda bb, i: (bb, i, 0)),
                   pl.BlockSpec((1, ts, ATT_KV_W), lambda bb, i: (bb, i, 0)),
                   pl.BlockSpec((1, ts, 2 * ATT_KV_W), lambda bb, i: (bb, i, 0)),
                   pl.BlockSpec((1, 1, 1, ATT_KV_W), lambda bb, i: (bb, i, 0, 0))),
        compiler_params=_cparams(("arbitrary", "arbitrary")),
        name="att_prep",
    )(proj, proj, proj, cos, sin_signed, qg.reshape(1, ATT_DH), kg.reshape(1, ATT_DH))


MAX_FIXED_SHIFT = 60.0


def _flash_body(q_ref, k_ref, v_ref, kn_ref, o_ref, *, tk, nkv):
    tq = q_ref.shape[1]
    heads = range(ATT_GROUP)
    qs = [q_ref[0, :, h * ATT_DH:(h + 1) * ATT_DH] for h in heads]
    unroll = max(u for u in (4, 2, 1) if nkv % u == 0)

    def tiles(t):
        start = pl.multiple_of(t * tk, tk)
        return k_ref[0, pl.ds(start, tk), :], v_ref[0, pl.ds(start, tk), :]

    def scores(h, ks):
        return lax.dot_general(qs[h], ks, (((1,), (1,)), ((), ())), preferred_element_type=F32)

    def finish(accs):
        for h in heads:
            o_ref[0, :, h * ATT_DH:(h + 1) * ATT_DH] = (accs[h][:, :ATT_DH] / accs[h][:, ATT_DH:]).astype(o_ref.dtype)

    kmax = jnp.sqrt(jnp.max(kn_ref[0, :, 0, :], axis=0, keepdims=True))[:, 0:1]
    shifts = []
    for h in heads:
        qf = qs[h].astype(F32)
        shifts.append(jnp.sqrt(jnp.sum(qf * qf, axis=-1, keepdims=True)) * (kmax * 1.001))
    worst = shifts[0]
    for h in heads[1:]:
        worst = jnp.maximum(worst, shifts[h])
    fixed_ok = jnp.max(worst) <= MAX_FIXED_SHIFT

    @pl.when(fixed_ok)
    def _():
        def step(t, accs):
            ks, vs = tiles(t)
            return tuple(accs[h] + jnp.dot(jnp.exp2(scores(h, ks) - shifts[h]).astype(BF16), vs,
                                           preferred_element_type=F32) for h in heads)

        init = tuple(jnp.zeros((tq, 2 * ATT_DH), F32) for _ in heads)
        finish(lax.fori_loop(0, nkv, step, init, unroll=unroll))

    @pl.when(jnp.logical_not(fixed_ok))
    def _():
        def step(t, carry):
            ks, vs = tiles(t)
            new = []
            for h in heads:
                m, acc = carry[h]
                sc = scores(h, ks)
                m_new = jnp.maximum(m, jnp.max(sc, axis=-1, keepdims=True))
                p = jnp.exp2(sc - m_new)
                acc = jnp.exp2(m - m_new) * acc + jnp.dot(p.astype(BF16), vs, preferred_element_type=F32)
                new.append((m_new, acc))
            return tuple(new)

        init = tuple((jnp.full((tq, 1), -1e30, F32), jnp.zeros((tq, 2 * ATT_DH), F32)) for _ in heads)
        res = lax.fori_loop(0, nkv, step, init, unroll=unroll)
        finish([res[h][1] for h in heads])


def flash_attention(q, k, v, kn, tq=512, tk=512):
    b, s, _ = q.shape
    tq = _tile(s, tq, 16)
    tk = _tile(s, tk, 16)
    gw = ATT_GROUP * ATT_DH
    return pl.pallas_call(
        functools.partial(_flash_body, tk=tk, nkv=s // tk),
        out_shape=jax.ShapeDtypeStruct((b, s, ATT_W), BF16),
        grid=(b, ATT_KV_HEADS, s // tq),
        in_specs=[pl.BlockSpec((1, tq, gw), lambda bb, g, i: (bb, i, g)),
                  pl.BlockSpec((1, s, ATT_DH), lambda bb, g, i: (bb, 0, g)),
                  pl.BlockSpec((1, s, 2 * ATT_DH), lambda bb, g, i: (bb, 0, g)),
                  pl.BlockSpec((1, kn.shape[1], 1, ATT_DH), lambda bb, g, i: (bb, 0, 0, g))],
        out_specs=pl.BlockSpec((1, tq, gw), lambda bb, g, i: (bb, i, g)),
        compiler_params=_cparams(("arbitrary", "arbitrary", "arbitrary")),
        name="flash_attention",
    )(q, k, v, kn)


def _gla_chunk(qs, k, v, gk, state_ref, d, reverse):
    c = CHUNK
    nsub = c // SUB
    ri = lax.broadcasted_iota(jnp.int32, (c, c), 0)
    ci = lax.broadcasted_iota(jnp.int32, (c, c), 1)
    incl = (ci >= ri) if reverse else (ri >= ci)
    gc = _hdot(incl.astype(F32), gk)
    last = 0 if reverse else c - 1
    gl = gc[last:last + 1, :]
    qd = qs * jnp.exp2(gc)
    kt = k * jnp.exp2(gl - gc)
    egl = jnp.exp2(gl)
    row = lax.broadcasted_iota(jnp.int32, (c, GLA_QK), 0)

    head_of_lane = lax.broadcasted_iota(jnp.int32, (SUB, GLA_QK), 1) // GLA_DK
    a_off = [[] for _ in range(GLA_HEADS)]
    for i in range(nsub):
        if i == (nsub - 1 if reverse else 0):
            for h in range(GLA_HEADS):
                a_off[h].append(jnp.zeros((SUB, c), F32))
            continue
        first = (i + 1) * SUB - 1 if reverse else i * SUB
        rs = slice(i * SUB, (i + 1) * SUB)
        ref = gc[first:first + 1, :]
        qi = qs[rs, :] * jnp.exp2(gc[rs, :] - ref)
        early = (row > first) if reverse else (row < first)
        kf = k * jnp.exp2(jnp.where(early, ref - gc, MASKED_LOG))
        stacked = jnp.concatenate([jnp.where(head_of_lane == h, qi, 0.0) for h in range(GLA_HEADS)], axis=0)
        prod = _bdot_nt(stacked, kf)
        for h in range(GLA_HEADS):
            a_off[h].append(prod[h * SUB:(h + 1) * SUB])

    lane = lax.broadcasted_iota(jnp.int32, (LANES, 2 * GLA_DV), 0)
    col = lax.broadcasted_iota(jnp.int32, (LANES, 2 * GLA_DV), 1)
    head_sum = ((lane // GLA_DK) == (col // GLA_DV)).astype(BF16)
    sub_row = lax.broadcasted_iota(jnp.int32, (SUB, LANES), 0)
    npair = GLA_HEADS // 2
    terms = []
    for p in range(npair):
        ls = slice(p * LANES, (p + 1) * LANES)
        for i in range(nsub):
            rs = slice(i * SUB, (i + 1) * SUB)
            gci, qsi = gc[rs, ls], qs[rs, ls]
            for jj in range(SUB):
                j = i * SUB + jj
                later = (sub_row <= jj) if reverse else (sub_row >= jj)
                e = jnp.exp2(jnp.where(later, gci - gc[j:j + 1, ls], MASKED_LOG))
                terms.append(qsi * k[j:j + 1, ls] * e)
    r = jnp.dot(jnp.concatenate(terms, axis=0).astype(BF16), head_sum, preferred_element_type=F32)
    o_diag = []
    for p in range(npair):
        vs = slice(p * 2 * GLA_DV, (p + 1) * 2 * GLA_DV)
        blocks = []
        for i in range(nsub):
            base = (p * nsub + i) * SUB * SUB
            acc = r[base:base + SUB, :] * v[i * SUB:i * SUB + 1, vs]
            for jj in range(1, SUB):
                acc = acc + r[base + jj * SUB:base + (jj + 1) * SUB, :] * v[i * SUB + jj:i * SUB + jj + 1, vs]
            blocks.append(acc)
        o_diag.append(jnp.concatenate(blocks, axis=0))

    outs = []
    for h in range(GLA_HEADS):
        sl = slice(h * GLA_DK, (h + 1) * GLA_DK)
        vh = v[:, h * GLA_DV:(h + 1) * GLA_DV]
        st = state_ref[d, h]
        a = jnp.concatenate(a_off[h], axis=0)
        od = o_diag[h // 2][:, (h % 2) * GLA_DV:(h % 2 + 1) * GLA_DV]
        outs.append(_bdot_nt(qd[:, sl], st) + _bdot(a, vh) + od)
        state_ref[d, h] = st * egl[:, sl] + lax.dot_general(
            vh.astype(BF16), kt[:, sl].astype(BF16), (((0,), (0,)), ((), ())), preferred_element_type=F32)
    return jnp.concatenate(outs, axis=1)


def _gla_scan_body(qf_ref, kf_ref, vf_ref, smf_ref, qb_ref, kb_ref, vb_ref, smb_ref, up_ref, upb_ref,
                   of_ref, ob_ref, state_ref):
    n = pl.program_id(1)

    @pl.when(n == 0)
    def _():
        state_ref[...] = jnp.zeros_like(state_ref)

    ins = ((qf_ref, kf_ref, vf_ref, smf_ref, of_ref), (qb_ref, kb_ref, vb_ref, smb_ref, ob_ref))
    for d, (q_ref, k_ref, v_ref, sm_ref, o_ref) in enumerate(ins):
        gk = jax.nn.log_sigmoid(_hdot(sm_ref[0], up_ref[d]) + upb_ref[d]) * (LOG2E / GLA_NORMALIZER)
        qs = q_ref[0].astype(F32) * (GLA_DK ** -0.5)
        o_ref[0] = _gla_chunk(qs, k_ref[0].astype(F32), v_ref[0].astype(F32), gk, state_ref, d,
                              reverse=(d == 1)).astype(o_ref.dtype)


def gla_scan(proj, small, up_pad, upb):
    b, s, _ = proj.shape
    nchunk = s // CHUNK

    def specs(d):
        def cidx(n):
            return n if d == 0 else nchunk - 1 - n
        return [pl.BlockSpec((1, CHUNK, GLA_QK), lambda bb, n: (bb, cidx(n), OFF_GQ // GLA_QK)),
                pl.BlockSpec((1, CHUNK, GLA_QK), lambda bb, n: (bb, cidx(n), OFF_GK // GLA_QK)),
                pl.BlockSpec((1, CHUNK, GLA_W), lambda bb, n: (bb, cidx(n), OFF_GV // GLA_W)),
                pl.BlockSpec((1, CHUNK, LANES), lambda bb, n: (bb, cidx(n), 0))]

    out = jax.ShapeDtypeStruct((b, s, GLA_W), BF16)
    return pl.pallas_call(
        _gla_scan_body,
        out_shape=(out, out),
        grid=(b, nchunk),
        in_specs=specs(0) + specs(1) + [pl.BlockSpec((2, LANES, GLA_QK), lambda bb, n: (0, 0, 0)),
                                        pl.BlockSpec((2, 1, GLA_QK), lambda bb, n: (0, 0, 0))],
        out_specs=(pl.BlockSpec((1, CHUNK, GLA_W), lambda bb, n: (bb, n, 0)),
                   pl.BlockSpec((1, CHUNK, GLA_W), lambda bb, n: (bb, nchunk - 1 - n, 0))),
        scratch_shapes=[pltpu.VMEM((2, GLA_HEADS, GLA_DV, GLA_DK), F32)],
        compiler_params=_cparams(("arbitrary", "arbitrary")),
        name="gla_scan",
    )(proj, proj, proj, small, proj, proj, proj, small, up_pad, upb)


def _gated_norm(o, gate, g, heads, d):
    outs = []
    for h in range(heads):
        sl = slice(h * d, (h + 1) * d)
        t = o[:, sl]
        y = t * lax.rsqrt(jnp.mean(t * t, axis=-1, keepdims=True) + EPS) * g
        gt = gate[:, sl].astype(F32)
        outs.append(y * (gt * jax.nn.sigmoid(gt)))
    return jnp.concatenate(outs, axis=1)


def _merge_body(dnf_ref, dnb_ref, dgate_ref, dng_ref, att_ref, glf_ref, glb_ref, ggate_ref, glg_ref, o_ref):
    o_dn = _gated_norm(dnf_ref[0].astype(F32) + dnb_ref[0].astype(F32), dgate_ref[0], dng_ref[...],
                       DN_HEADS, DN_D)
    o_gla = _gated_norm(glf_ref[0].astype(F32) + glb_ref[0].astype(F32), ggate_ref[0], glg_ref[...],
                        GLA_HEADS, GLA_DV)
    o_ref[0, :, 0:DN_W] = o_dn.astype(BF16)
    o_ref[0, :, DN_W:DN_W + ATT_W] = att_ref[0]
    o_ref[0, :, DN_W + ATT_W:] = o_gla.astype(BF16)


def merge_heads(dn_f, dn_b, proj, dn_g, att, gl_f, gl_b, gla_g, ts=512):
    b, s, _ = proj.shape
    ts = _tile(s, ts, 16)
    mix_w = DN_W + ATT_W + GLA_W

    def spec(w, blk=0):
        return pl.BlockSpec((1, ts, w), lambda bb, i: (bb, i, blk))

    vec = pl.BlockSpec((1, LANES), lambda bb, i: (0, 0))
    return pl.pallas_call(
        _merge_body,
        out_shape=jax.ShapeDtypeStruct((b, s, mix_w), BF16),
        grid=(b, s // ts),
        in_specs=[spec(DN_W), spec(DN_W), spec(DN_W, OFF_DGATE // DN_W), vec,
                  spec(ATT_W), spec(GLA_W), spec(GLA_W), spec(GLA_W, OFF_GGATE // GLA_W), vec],
        out_specs=spec(mix_w),
        compiler_params=_cparams(("arbitrary", "arbitrary")),
        name="merge_heads",
    )(dn_f, dn_b, proj, dn_g.reshape(1, DN_D), att, gl_f, gl_b, proj, gla_g.reshape(1, GLA_DV))


def _split_points():
    pts, acc = [], 0
    for sz in IN_SIZES[:-1]:
        acc += sz
        pts.append(acc)
    return pts


def _relayout_w_in(w):
    d = w.shape[0]
    (dq, dk, dv, dgate, a_f, a_b, b_f, b_b, aq, ak, av, gq, gkk, gv, ggate, lr_f, lr_b) = jnp.split(
        w, _split_points(), axis=1)
    small = jnp.concatenate([a_f, a_b, b_f, b_b, lr_f, lr_b], axis=1)
    small = jnp.pad(small, ((0, 0), (0, LANES - small.shape[1])))
    cols = jnp.concatenate([dq, dk, dv, dgate, aq, ak, av, gq, gv, ggate, gkk,
                            jnp.zeros((d, PROJ_COLS - OFF_PAD), w.dtype)], axis=1)
    return cols.astype(BF16)[None, None], small.astype(BF16)[None, None]


def _rope_tables(s):
    rows = s // GRID_W
    row = jnp.repeat(jnp.arange(rows, dtype=jnp.int32), GRID_W).astype(F32)
    col = jnp.tile(jnp.arange(GRID_W, dtype=jnp.int32), rows).astype(F32)
    inv_freq = ROPE_THETA ** (-jnp.arange(0, ROPE_SUB, 2, dtype=F32) / ROPE_SUB)
    ang_r = row[:, None] * inv_freq[None, :]
    ang_c = col[:, None] * inv_freq[None, :]
    cos = jnp.concatenate([jnp.cos(ang_r), jnp.cos(ang_r), jnp.cos(ang_c), jnp.cos(ang_c)], axis=1)
    sin = jnp.concatenate([-jnp.sin(ang_r), jnp.sin(ang_r), -jnp.sin(ang_c), jnp.sin(ang_c)], axis=1)
    return cos, sin


def _dn_params(dn_a_log, dn_dt_bias):
    rows = jnp.stack([dn_a_log.reshape(-1), dn_dt_bias.reshape(-1)], axis=0)
    return jnp.pad(rows, ((0, 0), (SM_A, LANES - SM_A - 2 * DN_HEADS)))


def _gla_params(gla_up, gla_up_b):
    ups = []
    for d in range(2):
        lo = SM_LR + d * GLA_RANK
        ups.append(jnp.pad(gla_up[d], ((lo, LANES - lo - GLA_RANK), (0, 0))))
    return jnp.stack(ups, axis=0), gla_up_b[:, None, :]


def _mixer(x16, bsz, s, w_in, dn_conv, dn_a_log, dn_dt_bias, dn_norm_g, att_qn_g, att_kn_g,
           gla_up, gla_up_b, gla_norm_g, rope):
    t = bsz * s
    bm = _tile(t, 512, 16)
    one_group = _block_table(jnp.zeros((t // bm,), jnp.int32))
    w_main, w_small = _relayout_w_in(w_in)
    proj = grouped_matmul(x16, w_main, 0, one_group, bm, 1024, BF16).reshape(bsz, s, PROJ_COLS)
    small = grouped_matmul(x16, w_small, 0, one_group, bm, LANES, F32).reshape(bsz, s, LANES)

    q, k, v = dn_prep(proj, dn_conv)
    u, wq, ktt, qk, egl = dn_wy(q, k, v, small, _dn_params(dn_a_log, dn_dt_bias))
    dn_f, dn_b = dn_rec(u, wq, ktt, qk, egl)

    cos, sin_signed = rope
    aq, ak, av, kn = att_prep(proj, cos, sin_signed, att_qn_g, att_kn_g)
    o_att = flash_attention(aq, ak, av, kn)

    gl_f, gl_b = gla_scan(proj, small, *_gla_params(gla_up, gla_up_b))

    return merge_heads(dn_f, dn_b, proj, dn_norm_g, o_att, gl_f, gl_b, gla_norm_g).reshape(t, -1)


def _moe(x32, x16, router_w, w_gate, w_up, w_down, layer, ln_g, ln_b, alpha):
    t, d = x32.shape
    rw = jnp.pad(router_w, ((0, 0), (0, LANES - N_EXPERTS)))
    logits = router_logits(x32, rw)[:, :N_EXPERTS]
    top_val, top_idx = lax.top_k(logits, TOP_K)
    gates = jax.nn.softmax(top_val, axis=-1)
    e_flat = top_idx.reshape(-1).astype(jnp.int32)
    tok_flat = jnp.repeat(jnp.arange(t, dtype=jnp.int32), TOP_K)
    onehot = (e_flat[:, None] == jnp.arange(N_EXPERTS, dtype=jnp.int32)[None, :]).astype(jnp.int32)
    csum = jnp.cumsum(onehot, axis=0)
    counts = csum[-1]
    rank = jnp.take_along_axis(csum, e_flat[:, None], axis=1)[:, 0] - 1
    padded = (counts + MOE_BLOCK - 1) // MOE_BLOCK * MOE_BLOCK
    pstart = jnp.cumsum(padded) - padded
    pend = pstart + padded
    dest = pstart[e_flat] + rank
    nb = -(-(TOP_K * t) // MOE_BLOCK) + N_EXPERTS
    cap = nb * MOE_BLOCK
    buf_tok = jnp.zeros((cap,), jnp.int32).at[dest].set(tok_flat)
    block_start = jnp.arange(nb, dtype=jnp.int32) * MOE_BLOCK
    block_exp = jnp.minimum(jnp.sum(block_start[:, None] >= pend[None, :], axis=-1), N_EXPERTS - 1)
    table = _block_table(block_exp, block_start < pend[-1])
    xb = x16[buf_tok]
    h = grouped_swiglu(xb, w_gate, w_up, layer, table, MOE_BLOCK, 1024)
    yb = grouped_matmul(h, w_down, layer, table, MOE_BLOCK, 512, BF16)
    dest_by_choice = dest.reshape(t, TOP_K).T.reshape(-1)
    return combine_ln(x32, yb[dest_by_choice], gates, ln_g, ln_b, alpha)


def _dense_ffn(x32, x16, w_gate, w_up, w_down, layer, ln_g, ln_b, alpha):
    t = x32.shape[0]
    bm = _tile(t, 512, 16)
    h = grouped_swiglu(x16, w_gate, w_up, layer, _block_table(jnp.zeros((t // bm,), jnp.int32)), bm, 512)
    return matmul_res_ln(h, w_down[layer].astype(BF16), x32, ln_g, ln_b, alpha)


def kernel(x, w_in, dn_conv, dn_a_log, dn_dt_bias, dn_norm_g, att_qn_g, att_kn_g, gla_up, gla_up_b,
           gla_norm_g, w_out, ln1_g, ln1_b, ln2_g, ln2_b, ffn_w_gate, ffn_w_up, ffn_w_down, router_w,
           exp_w_gate, exp_w_up, exp_w_down):
    bsz, s, d = x.shape
    depth = w_in.shape[0]
    alpha = (2.0 * depth) ** 0.25
    t = bsz * s
    rope = _rope_tables(s)
    x32 = x.reshape(t, d)
    x16 = x32.astype(BF16)
    for layer in range(depth):
        mix = _mixer(x16, bsz, s, w_in[layer], dn_conv[layer], dn_a_log[layer], dn_dt_bias[layer],
                     dn_norm_g[layer], att_qn_g[layer], att_kn_g[layer], gla_up[layer],
                     gla_up_b[layer], gla_norm_g[layer], rope)
        x32, x16 = matmul_res_ln(mix, w_out[layer].astype(BF16), x32, ln1_g[layer], ln1_b[layer], alpha)
        j = layer // 2
        if layer % 2 == 0:
            x32, x16 = _dense_ffn(x32, x16, ffn_w_gate, ffn_w_up, ffn_w_down, j,
                                  ln2_g[layer], ln2_b[layer], alpha)
        else:
            x32, x16 = _moe(x32, x16, router_w[j], exp_w_gate, exp_w_up, exp_w_down, j,
                            ln2_g[layer], ln2_b[layer], alpha)
    return x32.reshape(bsz, s, d)
```

```python
import functools
import math

import jax
import jax.numpy as jnp
from jax import lax
from jax.experimental import pallas as pl
from jax.experimental.pallas import tpu as pltpu

F32 = jnp.float32
BF16 = jnp.bfloat16
HIGHEST = lax.Precision.HIGHEST

DN_HEADS, DN_D = 6, 128
ATT_HEADS, ATT_KV_HEADS, ATT_DH = 6, 2, 128
ATT_GROUP = ATT_HEADS // ATT_KV_HEADS
ROPE_SUB, ROPE_THETA, GRID_W = 64, 10000.0, 64
GLA_HEADS, GLA_DK, GLA_DV, GLA_RANK = 4, 64, 128, 16
GLA_NORMALIZER = 16.0
CHUNK = 64
SUB = 8
MASKED_LOG = -1e30
WY_ROWS = 2 * CHUNK
EGL_ROWS = 8
CONV_W = 5
N_EXPERTS, TOP_K, MOE_BLOCK = 8, 2, 256
EPS = 1e-6
LOG2E = 1.4426950408889634
LANES = 128
HALO = 16

DN_W = DN_HEADS * DN_D
ATT_W = ATT_HEADS * ATT_DH
ATT_KV_W = ATT_KV_HEADS * ATT_DH
GLA_QK = GLA_HEADS * GLA_DK
GLA_W = GLA_HEADS * GLA_DV
IN_SIZES = (DN_W, DN_W, DN_W, DN_W, DN_HEADS, DN_HEADS, DN_HEADS, DN_HEADS,
            ATT_W, ATT_KV_W, ATT_KV_W, GLA_QK, GLA_QK, GLA_W, GLA_W, GLA_RANK, GLA_RANK)
OFF_DQ, OFF_DGATE, OFF_AQ, OFF_AK, OFF_AV = 0, 2304, 3072, 3840, 4096
OFF_GQ, OFF_GV, OFF_GGATE, OFF_GK, OFF_PAD = 4352, 4608, 5120, 5632, 5888
PROJ_COLS = 6144
SM_A, SM_B, SM_LR = 0, 2 * DN_HEADS, 4 * DN_HEADS

VMEM_LIMIT = 56 * 1024 * 1024


def _cparams(sem, vmem=VMEM_LIMIT):
    return pltpu.CompilerParams(dimension_semantics=sem, vmem_limit_bytes=vmem)


def _tile(n, pref, quantum=LANES):
    if n <= pref:
        return n
    t = pref - pref % quantum
    while n % t:
        t -= quantum
    return t


def _bdot(a, b):
    return jnp.dot(a.astype(BF16), b.astype(BF16), preferred_element_type=F32)


def _bdot_nt(a, b):
    return lax.dot_general(a.astype(BF16), b.astype(BF16), (((1,), (1,)), ((), ())),
                           preferred_element_type=F32)


def _hdot(a, b):
    return jnp.dot(a, b, preferred_element_type=F32, precision=HIGHEST)


def _block_table(group, used=None):
    used = jnp.ones_like(group) if used is None else used.astype(jnp.int32)
    return jnp.stack([group.astype(jnp.int32), used], axis=0)


def _group_changed(be_ref):
    i = pl.program_id(1)
    return (i == 0) | (be_ref[0, i] != be_ref[0, jnp.maximum(i - 1, 0)])


def _if_used(be_ref, o_ref, compute):
    used = be_ref[1, pl.program_id(1)] != 0

    @pl.when(used)
    def _():
        compute()

    @pl.when(jnp.logical_not(used))
    def _():
        o_ref[...] = jnp.zeros_like(o_ref)


def _gmm_body(be_ref, x_ref, w_ref, o_ref, *w16):
    if w16:
        @pl.when(_group_changed(be_ref))
        def _():
            w16[0][...] = w_ref[...].astype(BF16)
    w_use = w16[0] if w16 else w_ref

    def compute():
        o_ref[...] = jnp.dot(x_ref[...], w_use[...], preferred_element_type=F32).astype(o_ref.dtype)

    _if_used(be_ref, o_ref, compute)


def _weight_spec(w, layer, k, tn):
    if w.ndim == 3:
        return pl.BlockSpec((None, k, tn), lambda j, i, be: (layer, 0, j))
    return pl.BlockSpec((None, None, k, tn), lambda j, i, be: (layer, be[0, i], 0, j))


def grouped_matmul(x, w, layer, block_table, bm, tn, out_dtype):
    m, k = x.shape
    n = w.shape[-1]
    tn = _tile(n, tn)
    scratch = [pltpu.VMEM((k, tn), BF16)] if w.dtype != BF16 else []
    return pl.pallas_call(
        _gmm_body,
        out_shape=jax.ShapeDtypeStruct((m, n), out_dtype),
        grid_spec=pltpu.PrefetchScalarGridSpec(
            num_scalar_prefetch=1, grid=(n // tn, m // bm),
            in_specs=[pl.BlockSpec((bm, k), lambda j, i, be: (i, 0)), _weight_spec(w, layer, k, tn)],
            out_specs=pl.BlockSpec((bm, tn), lambda j, i, be: (i, j)),
            scratch_shapes=scratch),
        compiler_params=_cparams(("arbitrary", "arbitrary")),
        name="grouped_matmul",
    )(block_table, x, w)


def _gswiglu_body(be_ref, x_ref, wg_ref, wu_ref, o_ref, wg16, wu16):
    @pl.when(_group_changed(be_ref))
    def _():
        wg16[...] = wg_ref[...].astype(BF16)
        wu16[...] = wu_ref[...].astype(BF16)

    def compute():
        x = x_ref[...]
        g = jnp.dot(x, wg16[...], preferred_element_type=F32)
        u = jnp.dot(x, wu16[...], preferred_element_type=F32)
        o_ref[...] = (g * jax.nn.sigmoid(g) * u).astype(o_ref.dtype)

    _if_used(be_ref, o_ref, compute)


def grouped_swiglu(x, wg, wu, layer, block_table, bm, tn):
    m, k = x.shape
    n = wg.shape[-1]
    tn = _tile(n, tn)
    return pl.pallas_call(
        _gswiglu_body,
        out_shape=jax.ShapeDtypeStruct((m, n), BF16),
        grid_spec=pltpu.PrefetchScalarGridSpec(
            num_scalar_prefetch=1, grid=(n // tn, m // bm),
            in_specs=[pl.BlockSpec((bm, k), lambda j, i, be: (i, 0)),
                      _weight_spec(wg, layer, k, tn), _weight_spec(wu, layer, k, tn)],
            out_specs=pl.BlockSpec((bm, tn), lambda j, i, be: (i, j)),
            scratch_shapes=[pltpu.VMEM((k, tn), BF16), pltpu.VMEM((k, tn), BF16)]),
        compiler_params=_cparams(("arbitrary", "arbitrary")),
        name="grouped_swiglu",
    )(block_table, x, wg, wu)


def _res_ln(res, y, g, b, alpha):
    z = alpha * res + y
    mu = jnp.mean(z, axis=-1, keepdims=True)
    zc = z - mu
    var = jnp.mean(zc * zc, axis=-1, keepdims=True)
    return zc * lax.rsqrt(var + EPS) * g + b


def _mm_res_ln_body(x_ref, w_ref, res_ref, g_ref, b_ref, o32_ref, o16_ref, *acc, nk, alpha):
    kk = pl.program_id(1)
    part = jnp.dot(x_ref[...], w_ref[...], preferred_element_type=F32)

    def finish(y):
        out = _res_ln(res_ref[...], y, g_ref[...], b_ref[...], alpha)
        o32_ref[...] = out
        o16_ref[...] = out.astype(BF16)

    if nk == 1:
        finish(part)
        return
    acc_ref = acc[0]

    @pl.when(kk == 0)
    def _():
        acc_ref[...] = part

    @pl.when((kk > 0) & (kk < nk - 1))
    def _():
        acc_ref[...] += part

    @pl.when(kk == nk - 1)
    def _():
        finish(acc_ref[...] + part)


def matmul_res_ln(x, w, res, g, b, alpha, tm=512, tk=2048):
    m, k = x.shape
    n = w.shape[-1]
    tm = _tile(m, tm, 8)
    tk = _tile(k, tk)
    nk = k // tk
    return pl.pallas_call(
        functools.partial(_mm_res_ln_body, nk=nk, alpha=alpha),
        out_shape=(jax.ShapeDtypeStruct((m, n), F32), jax.ShapeDtypeStruct((m, n), BF16)),
        grid=(m // tm, nk),
        in_specs=[pl.BlockSpec((tm, tk), lambda i, kk: (i, kk)),
                  pl.BlockSpec((tk, n), lambda i, kk: (kk, 0)),
                  pl.BlockSpec((tm, n), lambda i, kk: (i, 0)),
                  pl.BlockSpec((1, n), lambda i, kk: (0, 0)),
                  pl.BlockSpec((1, n), lambda i, kk: (0, 0))],
        out_specs=(pl.BlockSpec((tm, n), lambda i, kk: (i, 0)),
                   pl.BlockSpec((tm, n), lambda i, kk: (i, 0))),
        scratch_shapes=[pltpu.VMEM((tm, n), F32)] if nk > 1 else [],
        compiler_params=_cparams(("arbitrary", "arbitrary")),
        name="matmul_res_ln",
    )(x, w, res, g.reshape(1, n), b.reshape(1, n))


def _combine_ln_body(res_ref, *refs, alpha):
    y_refs, (gate_ref, g_ref, b_ref, o32_ref, o16_ref) = refs[:TOP_K], refs[TOP_K:]
    gate = gate_ref[...]
    y = y_refs[0][...].astype(F32) * gate[:, 0:1]
    for kk in range(1, TOP_K):
        y = y + y_refs[kk][...].astype(F32) * gate[:, kk:kk + 1]
    out = _res_ln(res_ref[...], y, g_ref[...], b_ref[...], alpha)
    o32_ref[...] = out
    o16_ref[...] = out.astype(BF16)


def combine_ln(res, y, gates, g, b, alpha, tm=512):
    m, n = res.shape
    tm = _tile(m, tm, 16)
    nblk = m // tm
    row = pl.BlockSpec((tm, n), lambda i: (i, 0))
    vec = pl.BlockSpec((1, n), lambda i: (0, 0))

    def choice(kk):
        return pl.BlockSpec((tm, n), lambda i: (kk * nblk + i, 0))

    return pl.pallas_call(
        functools.partial(_combine_ln_body, alpha=alpha),
        out_shape=(jax.ShapeDtypeStruct((m, n), F32), jax.ShapeDtypeStruct((m, n), BF16)),
        grid=(nblk,),
        in_specs=[row] + [choice(kk) for kk in range(TOP_K)] + [pl.BlockSpec((tm, TOP_K), lambda i: (i, 0)), vec, vec],
        out_specs=(row, row),
        compiler_params=_cparams(("arbitrary",)),
        name="combine_ln",
    )(res, *([y] * TOP_K), gates, g.reshape(1, n), b.reshape(1, n))


def _router_body(x_ref, w_ref, o_ref):
    o_ref[...] = _hdot(x_ref[...], w_ref[...])


def router_logits(x, w_pad, tm=512):
    m, k = x.shape
    n = w_pad.shape[-1]
    tm = _tile(m, tm, 8)
    return pl.pallas_call(
        _router_body,
        out_shape=jax.ShapeDtypeStruct((m, n), F32),
        grid=(m // tm,),
        in_specs=[pl.BlockSpec((tm, k), lambda i: (i, 0)), pl.BlockSpec((k, n), lambda i: (0, 0))],
        out_specs=pl.BlockSpec((tm, n), lambda i: (i, 0)),
        compiler_params=_cparams(("arbitrary",)),
        name="router_logits",
    )(x, w_pad)


def _dn_prep_body(cur_ref, prev_ref, next_ref, w_ref, q_ref, k_ref, v_ref, buf_ref, *, ts, nblk):
    i = pl.program_id(1)
    buf_ref[0:HALO, :] = jnp.where(i > 0, prev_ref[0].astype(F32), 0.0)
    buf_ref[HALO:HALO + ts, :] = cur_ref[0].astype(F32)
    buf_ref[HALO + ts:2 * HALO + ts, :] = jnp.where(i < nblk - 1, next_ref[0].astype(F32), 0.0)
    base = HALO - CONV_W // 2
    acc = buf_ref[base:base + ts, :] * w_ref[0:1, :]
    for j in range(1, CONV_W):
        acc = acc + buf_ref[base + j:base + j + ts, :] * w_ref[j:j + 1, :]
    y = acc * jax.nn.sigmoid(acc)
    for h in range(DN_HEADS):
        for off, ref in ((0, q_ref), (DN_W, k_ref)):
            t = y[:, off + h * DN_D:off + (h + 1) * DN_D]
            ref[0, :, h * DN_D:(h + 1) * DN_D] = t * lax.rsqrt(jnp.sum(t * t, axis=-1, keepdims=True) + EPS)
    v_ref[0] = y[:, 2 * DN_W:3 * DN_W]


def dn_prep(proj, conv_w, ts=512):
    b, s, _ = proj.shape
    ts = _tile(s, ts, HALO)
    nblk = s // ts
    c = 3 * DN_W
    hb = ts // HALO
    out = jax.ShapeDtypeStruct((b, s, DN_W), F32)
    ospec = pl.BlockSpec((1, ts, DN_W), lambda bb, i: (bb, i, 0))
    return pl.pallas_call(
        functools.partial(_dn_prep_body, ts=ts, nblk=nblk),
        out_shape=(out, out, out),
        grid=(b, nblk),
        in_specs=[pl.BlockSpec((1, ts, c), lambda bb, i: (bb, i, 0)),
                  pl.BlockSpec((1, HALO, c), lambda bb, i: (bb, jnp.maximum(i * hb - 1, 0), 0)),
                  pl.BlockSpec((1, HALO, c), lambda bb, i: (bb, jnp.minimum((i + 1) * hb, nblk * hb - 1), 0)),
                  pl.BlockSpec((CONV_W, c), lambda bb, i: (0, 0))],
        out_specs=(ospec, ospec, ospec),
        scratch_shapes=[pltpu.VMEM((ts + 2 * HALO, c), F32)],
        compiler_params=_cparams(("arbitrary", "arbitrary")),
        name="dn_prep",
    )(proj, proj, proj, conv_w)


def _dn_wy_body(q_ref, k_ref, v_ref, sm_ref, par_ref, u_ref, wq_ref, ktt_ref, qk_ref, egl_ref, *, tiles):
    def tile(t, carry):
        _dn_wy_tile(t, q_ref, k_ref, v_ref, sm_ref, par_ref, u_ref, wq_ref, ktt_ref, qk_ref, egl_ref)
        return carry

    lax.fori_loop(0, tiles, tile, 0)


def _dn_wy_tile(t, q_ref, k_ref, v_ref, sm_ref, par_ref, u_ref, wq_ref, ktt_ref, qk_ref, egl_ref):
    r = WY_ROWS
    nck = r // CHUNK
    rows_t = pl.ds(pl.multiple_of(t * r, r), r)
    sm = sm_ref[0, rows_t, :]
    par = par_ref[...]
    g_all = -jnp.exp(par[0:1, :]) * jax.nn.softplus(sm + par[1:2, :])
    beta_all = jax.nn.sigmoid(sm)
    ri = lax.broadcasted_iota(jnp.int32, (r, r), 0)
    ci = lax.broadcasted_iota(jnp.int32, (r, r), 1)
    same = (ri // CHUNK) == (ci // CHUNK)
    eye = (ri == ci).astype(F32)
    masks = []
    for d in range(2):
        delta = ri - ci if d == 0 else ci - ri
        masks.append((same & (delta >= 0), same & (delta > 0)))
    scale = DN_D ** -0.5
    gtot = _hdot(same.astype(F32), g_all)
    gcs = []
    for d in range(2):
        gc = _hdot(masks[d][0].astype(F32), g_all)
        gcs.append((gc, gc.T))
    chains = [(h, d) for h in range(DN_HEADS) for d in range(2)]
    qk_cols = ([], [])
    nm, tinv, rhs = {}, {}, {}
    for h in range(DN_HEADS):
        sl = slice(h * DN_D, (h + 1) * DN_D)
        q = q_ref[0, rows_t, sl] * scale
        k = k_ref[0, rows_t, sl]
        v = v_ref[0, rows_t, sl]
        kq = _bdot_nt(jnp.concatenate([k, q], axis=0), k)
        kk, qk = kq[:r], kq[r:]
        for d in range(2):
            lane = SM_A + d * DN_HEADS + h
            incl, strict = masks[d]
            gc, gct = gcs[d]
            gcol = jnp.broadcast_to(gc[:, lane:lane + 1], (r, r))
            gt = jnp.broadcast_to(gtot[:, lane:lane + 1], (r, r))
            bcol = jnp.broadcast_to(beta_all[:, SM_B + d * DN_HEADS + h:SM_B + d * DN_HEADS + h + 1], (r, r))
            grow = gct[lane:lane + 1, :]
            decay = jnp.where(incl, jnp.exp(jnp.where(incl, gcol - grow, 0.0)), 0.0)
            nm[h, d] = jnp.where(strict, -(bcol * kk) * decay, 0.0)
            tinv[h, d] = eye + nm[h, d]
            eg = jnp.exp(gcol)
            rhs[h, d] = jnp.concatenate([v * bcol, k * (bcol * eg)], axis=1).astype(BF16)
            qd16 = (q * eg).astype(BF16)
            ktt = _bdot_nt(eye, (k * jnp.exp(gt - gcol)).astype(BF16)).astype(BF16)
            for c in range(nck):
                rows = slice(c * CHUNK, (c + 1) * CHUNK)
                wq_ref[d, 0, t * nck + c, CHUNK:2 * CHUNK, sl] = qd16[rows]
                ktt_ref[d, 0, t * nck + c, :, h * CHUNK:(h + 1) * CHUNK] = ktt[:, rows]
            qkd = qk * decay
            qk_cols[d].append(qkd[:, :CHUNK] + qkd[:, CHUNK:])
            egt = jnp.exp(gt)
            for c in range(nck):
                egl_ref[d, 0, t * nck + c, :, sl] = egt[c * CHUNK:c * CHUNK + EGL_ROWS]
    for d in range(2):
        qk_ref[d, 0, rows_t, :] = jnp.concatenate(qk_cols[d], axis=1).astype(BF16)
    for _ in range(int(math.log2(CHUNK)) - 1):
        for hd in chains:
            nm[hd] = _bdot(nm[hd], nm[hd])
        for hd in chains:
            tinv[hd] = tinv[hd] + _bdot(tinv[hd], nm[hd])
    for h, d in chains:
        sl = slice(h * DN_D, (h + 1) * DN_D)
        uw = _bdot(tinv[h, d], rhs[h, d])
        u_ref[d, 0, rows_t, sl] = uw[:, :DN_D].astype(BF16)
        w16 = uw[:, DN_D:].astype(BF16)
        for c in range(nck):
            wq_ref[d, 0, t * nck + c, 0:CHUNK, sl] = w16[c * CHUNK:(c + 1) * CHUNK]


def dn_wy(q, k, v, small, par):
    b, s, w = q.shape
    nchunk = s // CHUNK
    tiles = max(c for c in (4, 2, 1) if s % (c * WY_ROWS) == 0)
    r = tiles * WY_ROWS
    nck = r // CHUNK
    seq = pl.BlockSpec((1, r, w), lambda bb, i: (bb, i, 0))
    dseq = pl.BlockSpec((2, 1, r, w), lambda bb, i: (0, bb, i, 0))
    return pl.pallas_call(
        functools.partial(_dn_wy_body, tiles=tiles),
        out_shape=(jax.ShapeDtypeStruct((2, b, s, w), BF16),
                   jax.ShapeDtypeStruct((2, b, nchunk, 2 * CHUNK, w), BF16),
                   jax.ShapeDtypeStruct((2, b, nchunk, DN_D, DN_HEADS * CHUNK), BF16),
                   jax.ShapeDtypeStruct((2, b, s, DN_HEADS * CHUNK), BF16),
                   jax.ShapeDtypeStruct((2, b, nchunk, EGL_ROWS, w), F32)),
        grid=(b, s // r),
        in_specs=[seq, seq, seq,
                  pl.BlockSpec((1, r, LANES), lambda bb, i: (bb, i, 0)),
                  pl.BlockSpec((2, LANES), lambda bb, i: (0, 0))],
        out_specs=(dseq,
                   pl.BlockSpec((2, 1, nck, 2 * CHUNK, w), lambda bb, i: (0, bb, i, 0, 0)),
                   pl.BlockSpec((2, 1, nck, DN_D, DN_HEADS * CHUNK), lambda bb, i: (0, bb, i, 0, 0)),
                   pl.BlockSpec((2, 1, r, DN_HEADS * CHUNK), lambda bb, i: (0, bb, i, 0)),
                   pl.BlockSpec((2, 1, nck, EGL_ROWS, w), lambda bb, i: (0, bb, i, 0, 0))),
        compiler_params=_cparams(("arbitrary", "arbitrary")),
        name="dn_wy",
    )(q, k, v, small, par)


def _dn_rec_body(*refs, nc):
    ins = (refs[0:5], refs[5:10])
    outs = refs[10:12]
    state_ref = refs[12]
    n = pl.program_id(1)

    @pl.when(n == 0)
    def _():
        state_ref[...] = jnp.zeros_like(state_ref)

    chains = [(d, h) for d in range(2) for h in range(DN_HEADS)]
    state = {(d, h): state_ref[d, h] for d, h in chains}
    for step in range(nc):
        chunk = (step, nc - 1 - step)
        wq, v_new = {}, {}
        for d, h in chains:
            sl = slice(h * DN_D, (h + 1) * DN_D)
            wq[d, h] = jnp.dot(ins[d][1][0, 0, chunk[d], :, sl], state[d, h].astype(BF16),
                               preferred_element_type=F32)
        for d, h in chains:
            sl = slice(h * DN_D, (h + 1) * DN_D)
            rows = slice(chunk[d] * CHUNK, (chunk[d] + 1) * CHUNK)
            v_new[d, h] = (ins[d][0][0, 0, rows, sl].astype(F32) - wq[d, h][:CHUNK]).astype(BF16)
        for d, h in chains:
            sl = slice(h * DN_D, (h + 1) * DN_D)
            rows = slice(chunk[d] * CHUNK, (chunk[d] + 1) * CHUNK)
            qk = ins[d][3][0, 0, rows, h * CHUNK:(h + 1) * CHUNK]
            outs[d][0, rows, sl] = (wq[d, h][CHUNK:]
                                    + jnp.dot(qk, v_new[d, h], preferred_element_type=F32)).astype(outs[d].dtype)
        for d, h in chains:
            sl = slice(h * DN_D, (h + 1) * DN_D)
            ktt = ins[d][2][0, 0, chunk[d], :, h * CHUNK:(h + 1) * CHUNK]
            state[d, h] = (state[d, h] * ins[d][4][0, 0, chunk[d], 0:1, sl]
                           + jnp.dot(ktt, v_new[d, h], preferred_element_type=F32))
    for d, h in chains:
        state_ref[d, h] = state[d, h]


def dn_rec(u, wq, ktt, qk, egl):
    _, b, s, w = u.shape
    nchunk = s // CHUNK
    nc = max(c for c in (4, 2, 1) if nchunk % c == 0)
    nblk = nchunk // nc
    rows = nc * CHUNK

    def specs(d):
        def blk(n):
            return n if d == 0 else nblk - 1 - n
        return [pl.BlockSpec((1, 1, rows, w), lambda bb, n: (d, bb, blk(n), 0)),
                pl.BlockSpec((1, 1, nc, 2 * CHUNK, w), lambda bb, n: (d, bb, blk(n), 0, 0)),
                pl.BlockSpec((1, 1, nc, DN_D, DN_HEADS * CHUNK), lambda bb, n: (d, bb, blk(n), 0, 0)),
                pl.BlockSpec((1, 1, rows, DN_HEADS * CHUNK), lambda bb, n: (d, bb, blk(n), 0)),
                pl.BlockSpec((1, 1, nc, EGL_ROWS, w), lambda bb, n: (d, bb, blk(n), 0, 0))]

    out = jax.ShapeDtypeStruct((b, s, w), BF16)
    return pl.pallas_call(
        functools.partial(_dn_rec_body, nc=nc),
        out_shape=(out, out),
        grid=(b, nblk),
        in_specs=specs(0) + specs(1),
        out_specs=(pl.BlockSpec((1, rows, w), lambda bb, n: (bb, n, 0)),
                   pl.BlockSpec((1, rows, w), lambda bb, n: (bb, nblk - 1 - n, 0))),
        scratch_shapes=[pltpu.VMEM((2, DN_HEADS, DN_D, DN_D), F32)],
        compiler_params=_cparams(("arbitrary", "arbitrary")),
        name="dn_rec",
    )(u, wq, ktt, qk, egl, u, wq, ktt, qk, egl)


def _rope_head(x, g, cos, sin_signed, first_half):
    x = x.astype(F32)
    xf = x * lax.rsqrt(jnp.mean(x * x, axis=-1, keepdims=True) + EPS) * g
    partner = jnp.where(first_half, pltpu.roll(xf, LANES - ROPE_SUB // 2, axis=1),
                        pltpu.roll(xf, ROPE_SUB // 2, axis=1))
    return xf * cos + partner * sin_signed


def _att_prep_body(q_ref, k_ref, v_ref, cos_ref, sin_ref, qg_ref, kg_ref, qo_ref, ko_ref, vo_ref, kn_ref):
    cos = cos_ref[...]
    sin_signed = sin_ref[...]
    lane = lax.broadcasted_iota(jnp.int32, cos.shape, 1)
    first_half = (lane % ROPE_SUB) < ROPE_SUB // 2
    scale = ATT_DH ** -0.5 * LOG2E
    for h in range(ATT_HEADS):
        sl = slice(h * ATT_DH, (h + 1) * ATT_DH)
        qo_ref[0, :, sl] = (_rope_head(q_ref[0, :, sl], qg_ref[...], cos, sin_signed, first_half)
                            * scale).astype(BF16)
    for h in range(ATT_KV_HEADS):
        sl = slice(h * ATT_DH, (h + 1) * ATT_DH)
        k16 = _rope_head(k_ref[0, :, sl], kg_ref[...], cos, sin_signed, first_half).astype(BF16)
        ko_ref[0, :, sl] = k16
        kf = k16.astype(F32)
        n2 = jnp.max(jnp.sum(kf * kf, axis=-1, keepdims=True), axis=0, keepdims=True)
        kn_ref[0, 0, :, sl] = jnp.broadcast_to(n2, (1, ATT_DH))
    for h in range(ATT_KV_HEADS):
        vo_ref[0, :, 2 * h * ATT_DH:(2 * h + 1) * ATT_DH] = v_ref[0, :, h * ATT_DH:(h + 1) * ATT_DH]
        vo_ref[0, :, (2 * h + 1) * ATT_DH:(2 * h + 2) * ATT_DH] = jnp.ones((v_ref.shape[1], ATT_DH), BF16)


def att_prep(proj, cos, sin_signed, qg, kg, ts=512):
    b, s, _ = proj.shape
    ts = _tile(s, ts, 16)
    tab = pl.BlockSpec((ts, ATT_DH), lambda bb, i: (i, 0))
    vec = pl.BlockSpec((1, ATT_DH), lambda bb, i: (0, 0))
    return pl.pallas_call(
        _att_prep_body,
        out_shape=(jax.ShapeDtypeStruct((b, s, ATT_W), BF16),
                   jax.ShapeDtypeStruct((b, s, ATT_KV_W), BF16),
                   jax.ShapeDtypeStruct((b, s, 2 * ATT_KV_W), BF16),
                   jax.ShapeDtypeStruct((b, s // ts, 1, ATT_KV_W), F32)),
        grid=(b, s // ts),
        in_specs=[pl.BlockSpec((1, ts, ATT_W), lambda bb, i: (bb, i, OFF_AQ // ATT_W)),
                  pl.BlockSpec((1, ts, ATT_KV_W), lambda bb, i: (bb, i, OFF_AK // ATT_KV_W)),
                  pl.BlockSpec((1, ts, ATT_KV_W), lambda bb, i: (bb, i, OFF_AV // ATT_KV_W)),
                  tab, tab, vec, vec],
        out_specs=(pl.BlockSpec((1, ts, ATT_W), lambda bb, i: (bb, i, 0)),
                   pl.BlockSpec((1, ts, ATT_KV_W), lambda bb, i: (bb, i, 0)),
                   pl.BlockSpec((1, ts, 2 * ATT_KV_W), lambda bb, i: (bb, i, 0)),
                   pl.BlockSpec((1, 1, 1, ATT_KV_W), lambda bb, i: (bb, i, 0, 0))),
        compiler_params=_cparams(("arbitrary", "arbitrary")),
        name="att_prep",
    )(proj, proj, proj, cos, sin_signed, qg.reshape(1, ATT_DH), kg.reshape(1, ATT_DH))


MAX_FIXED_SHIFT = 60.0


def _flash_body(q_ref, k_ref, v_ref, kn_ref, o_ref, *, tk, nkv):
    tq = q_ref.shape[1]
    heads = range(ATT_GROUP)
    qs = [q_ref[0, :, h * ATT_DH:(h + 1) * ATT_DH] for h in heads]
    unroll = max(u for u in (4, 2, 1) if nkv % u == 0)

    def tiles(t):
        start = pl.multiple_of(t * tk, tk)
        return k_ref[0, pl.ds(start, tk), :], v_ref[0, pl.ds(start, tk), :]

    def scores(h, ks):
        return lax.dot_general(qs[h], ks, (((1,), (1,)), ((), ())), preferred_element_type=F32)

    def finish(accs):
        for h in heads:
            o_ref[0, :, h * ATT_DH:(h + 1) * ATT_DH] = (accs[h][:, :ATT_DH] / accs[h][:, ATT_DH:]).astype(o_ref.dtype)

    kmax = jnp.sqrt(jnp.max(kn_ref[0, :, 0, :], axis=0, keepdims=True))[:, 0:1]
    shifts = []
    for h in heads:
        qf = qs[h].astype(F32)
        shifts.append(jnp.sqrt(jnp.sum(qf * qf, axis=-1, keepdims=True)) * (kmax * 1.001))
    worst = shifts[0]
    for h in heads[1:]:
        worst = jnp.maximum(worst, shifts[h])
    fixed_ok = jnp.max(worst) <= MAX_FIXED_SHIFT

    @pl.when(fixed_ok)
    def _():
        def step(t, accs):
            ks, vs = tiles(t)
            return tuple(accs[h] + jnp.dot(jnp.exp2(scores(h, ks) - shifts[h]).astype(BF16), vs,
                                           preferred_element_type=F32) for h in heads)

        init = tuple(jnp.zeros((tq, 2 * ATT_DH), F32) for _ in heads)
        finish(lax.fori_loop(0, nkv, step, init, unroll=unroll))

    @pl.when(jnp.logical_not(fixed_ok))
    def _():
        def step(t, carry):
            ks, vs = tiles(t)
            new = []
            for h in heads:
                m, acc = carry[h]
                sc = scores(h, ks)
                m_new = jnp.maximum(m, jnp.max(sc, axis=-1, keepdims=True))
                p = jnp.exp2(sc - m_new)
                acc = jnp.exp2(m - m_new) * acc + jnp.dot(p.astype(BF16), vs, preferred_element_type=F32)
                new.append((m_new, acc))
            return tuple(new)

        init = tuple((jnp.full((tq, 1), -1e30, F32), jnp.zeros((tq, 2 * ATT_DH), F32)) for _ in heads)
        res = lax.fori_loop(0, nkv, step, init, unroll=unroll)
        finish([res[h][1] for h in heads])


def flash_attention(q, k, v, kn, tq=512, tk=512):
    b, s, _ = q.shape
    tq = _tile(s, tq, 16)
    tk = _tile(s, tk, 16)
    gw = ATT_GROUP * ATT_DH
    return pl.pallas_call(
        functools.partial(_flash_body, tk=tk, nkv=s // tk),
        out_shape=jax.ShapeDtypeStruct((b, s, ATT_W), BF16),
        grid=(b, ATT_KV_HEADS, s // tq),
        in_specs=[pl.BlockSpec((1, tq, gw), lambda bb, g, i: (bb, i, g)),
                  pl.BlockSpec((1, s, ATT_DH), lambda bb, g, i: (bb, 0, g)),
                  pl.BlockSpec((1, s, 2 * ATT_DH), lambda bb, g, i: (bb, 0, g)),
                  pl.BlockSpec((1, kn.shape[1], 1, ATT_DH), lambda bb, g, i: (bb, 0, 0, g))],
        out_specs=pl.BlockSpec((1, tq, gw), lambda bb, g, i: (bb, i, g)),
        compiler_params=_cparams(("arbitrary", "arbitrary", "arbitrary")),
        name="flash_attention",
    )(q, k, v, kn)


def _gla_chunk(qs, k, v, gk, state_ref, d, reverse):
    c = CHUNK
    nsub = c // SUB
    ri = lax.broadcasted_iota(jnp.int32, (c, c), 0)
    ci = lax.broadcasted_iota(jnp.int32, (c, c), 1)
    incl = (ci >= ri) if reverse else (ri >= ci)
    gc = _hdot(incl.astype(F32), gk)
    last = 0 if reverse else c - 1
    gl = gc[last:last + 1, :]
    qd = qs * jnp.exp2(gc)
    kt = k * jnp.exp2(gl - gc)
    egl = jnp.exp2(gl)
    row = lax.broadcasted_iota(jnp.int32, (c, GLA_QK), 0)

    head_of_lane = lax.broadcasted_iota(jnp.int32, (SUB, GLA_QK), 1) // GLA_DK
    a_off = [[] for _ in range(GLA_HEADS)]
    for i in range(nsub):
        if i == (nsub - 1 if reverse else 0):
            for h in range(GLA_HEADS):
                a_off[h].append(jnp.zeros((SUB, c), F32))
            continue
        first = (i + 1) * SUB - 1 if reverse else i * SUB
        rs = slice(i * SUB, (i + 1) * SUB)
        ref = gc[first:first + 1, :]
        qi = qs[rs, :] * jnp.exp2(gc[rs, :] - ref)
        early = (row > first) if reverse else (row < first)
        kf = k * jnp.exp2(jnp.where(early, ref - gc, MASKED_LOG))
        stacked = jnp.concatenate([jnp.where(head_of_lane == h, qi, 0.0) for h in range(GLA_HEADS)], axis=0)
        prod = _bdot_nt(stacked, kf)
        for h in range(GLA_HEADS):
            a_off[h].append(prod[h * SUB:(h + 1) * SUB])

    lane = lax.broadcasted_iota(jnp.int32, (LANES, 2 * GLA_DV), 0)
    col = lax.broadcasted_iota(jnp.int32, (LANES, 2 * GLA_DV), 1)
    head_sum = ((lane // GLA_DK) == (col // GLA_DV)).astype(BF16)
    sub_row = lax.broadcasted_iota(jnp.int32, (SUB, LANES), 0)
    npair = GLA_HEADS // 2
    terms = []
    for p in range(npair):
        ls = slice(p * LANES, (p + 1) * LANES)
        for i in range(nsub):
            rs = slice(i * SUB, (i + 1) * SUB)
            gci, qsi = gc[rs, ls], qs[rs, ls]
            for jj in range(SUB):
                j = i * SUB + jj
                later = (sub_row <= jj) if reverse else (sub_row >= jj)
                e = jnp.exp2(jnp.where(later, gci - gc[j:j + 1, ls], MASKED_LOG))
                terms.append(qsi * k[j:j + 1, ls] * e)
    r = jnp.dot(jnp.concatenate(terms, axis=0).astype(BF16), head_sum, preferred_element_type=F32)
    o_diag = []
    for p in range(npair):
        vs = slice(p * 2 * GLA_DV, (p + 1) * 2 * GLA_DV)
        blocks = []
        for i in range(nsub):
            base = (p * nsub + i) * SUB * SUB
            acc = r[base:base + SUB, :] * v[i * SUB:i * SUB + 1, vs]
            for jj in range(1, SUB):
                acc = acc + r[base + jj * SUB:base + (jj + 1) * SUB, :] * v[i * SUB + jj:i * SUB + jj + 1, vs]
            blocks.append(acc)
        o_diag.append(jnp.concatenate(blocks, axis=0))

    outs = []
    for h in range(GLA_HEADS):
        sl = slice(h * GLA_DK, (h + 1) * GLA_DK)
        vh = v[:, h * GLA_DV:(h + 1) * GLA_DV]
        st = state_ref[d, h]
        a = jnp.concatenate(a_off[h], axis=0)
        od = o_diag[h // 2][:, (h % 2) * GLA_DV:(h % 2 + 1) * GLA_DV]
        outs.append(_bdot_nt(qd[:, sl], st) + _bdot(a, vh) + od)
        state_ref[d, h] = st * egl[:, sl] + lax.dot_general(
            vh.astype(BF16), kt[:, sl].astype(BF16), (((0,), (0,)), ((), ())), preferred_element_type=F32)
    return jnp.concatenate(outs, axis=1)


def _gla_scan_body(qf_ref, kf_ref, vf_ref, smf_ref, qb_ref, kb_ref, vb_ref, smb_ref, up_ref, upb_ref,
                   of_ref, ob_ref, state_ref, *, nc):
    n = pl.program_id(1)

    @pl.when(n == 0)
    def _():
        state_ref[...] = jnp.zeros_like(state_ref)

    ins = ((qf_ref, kf_ref, vf_ref, smf_ref, of_ref), (qb_ref, kb_ref, vb_ref, smb_ref, ob_ref))

    def chunk_pair(t, carry):
        for d, (q_ref, k_ref, v_ref, sm_ref, o_ref) in enumerate(ins):
            c = t if d == 0 else nc - 1 - t
            rows = pl.ds(pl.multiple_of(c * CHUNK, CHUNK), CHUNK)
            gk = jax.nn.log_sigmoid(_hdot(sm_ref[0, rows, :], up_ref[d]) + upb_ref[d]) * (LOG2E / GLA_NORMALIZER)
            qs = q_ref[0, rows, :].astype(F32) * (GLA_DK ** -0.5)
            o_ref[0, rows, :] = _gla_chunk(qs, k_ref[0, rows, :].astype(F32), v_ref[0, rows, :].astype(F32), gk,
                                           state_ref, d, reverse=(d == 1)).astype(o_ref.dtype)
        return carry

    lax.fori_loop(0, nc, chunk_pair, 0)


def gla_scan(proj, small, up_pad, upb):
    b, s, _ = proj.shape
    nchunk = s // CHUNK
    nc = max(c for c in (4, 2, 1) if nchunk % c == 0)
    nblk = nchunk // nc
    rows = nc * CHUNK

    def specs(d):
        def cidx(n):
            return n if d == 0 else nblk - 1 - n
        return [pl.BlockSpec((1, rows, GLA_QK), lambda bb, n: (bb, cidx(n), OFF_GQ // GLA_QK)),
                pl.BlockSpec((1, rows, GLA_QK), lambda bb, n: (bb, cidx(n), OFF_GK // GLA_QK)),
                pl.BlockSpec((1, rows, GLA_W), lambda bb, n: (bb, cidx(n), OFF_GV // GLA_W)),
                pl.BlockSpec((1, rows, LANES), lambda bb, n: (bb, cidx(n), 0))]

    out = jax.ShapeDtypeStruct((b, s, GLA_W), BF16)
    return pl.pallas_call(
        functools.partial(_gla_scan_body, nc=nc),
        out_shape=(out, out),
        grid=(b, nblk),
        in_specs=specs(0) + specs(1) + [pl.BlockSpec((2, LANES, GLA_QK), lambda bb, n: (0, 0, 0)),
                                        pl.BlockSpec((2, 1, GLA_QK), lambda bb, n: (0, 0, 0))],
        out_specs=(pl.BlockSpec((1, rows, GLA_W), lambda bb, n: (bb, n, 0)),
                   pl.BlockSpec((1, rows, GLA_W), lambda bb, n: (bb, nblk - 1 - n, 0))),
        scratch_shapes=[pltpu.VMEM((2, GLA_HEADS, GLA_DV, GLA_DK), F32)],
        compiler_params=_cparams(("arbitrary", "arbitrary")),
        name="gla_scan",
    )(proj, proj, proj, small, proj, proj, proj, small, up_pad, upb)


def _gated_norm(o, gate, g, heads, d):
    outs = []
    for h in range(heads):
        sl = slice(h * d, (h + 1) * d)
        t = o[:, sl]
        y = t * lax.rsqrt(jnp.mean(t * t, axis=-1, keepdims=True) + EPS) * g
        gt = gate[:, sl].astype(F32)
        outs.append(y * (gt * jax.nn.sigmoid(gt)))
    return jnp.concatenate(outs, axis=1)


def _merge_body(dnf_ref, dnb_ref, dgate_ref, dng_ref, att_ref, glf_ref, glb_ref, ggate_ref, glg_ref, o_ref):
    o_dn = _gated_norm(dnf_ref[0].astype(F32) + dnb_ref[0].astype(F32), dgate_ref[0], dng_ref[...],
                       DN_HEADS, DN_D)
    o_gla = _gated_norm(glf_ref[0].astype(F32) + glb_ref[0].astype(F32), ggate_ref[0], glg_ref[...],
                        GLA_HEADS, GLA_DV)
    o_ref[0, :, 0:DN_W] = o_dn.astype(BF16)
    o_ref[0, :, DN_W:DN_W + ATT_W] = att_ref[0]
    o_ref[0, :, DN_W + ATT_W:] = o_gla.astype(BF16)


def merge_heads(dn_f, dn_b, proj, dn_g, att, gl_f, gl_b, gla_g, ts=512):
    b, s, _ = proj.shape
    ts = _tile(s, ts, 16)
    mix_w = DN_W + ATT_W + GLA_W

    def spec(w, blk=0):
        return pl.BlockSpec((1, ts, w), lambda bb, i: (bb, i, blk))

    vec = pl.BlockSpec((1, LANES), lambda bb, i: (0, 0))
    return pl.pallas_call(
        _merge_body,
        out_shape=jax.ShapeDtypeStruct((b, s, mix_w), BF16),
        grid=(b, s // ts),
        in_specs=[spec(DN_W), spec(DN_W), spec(DN_W, OFF_DGATE // DN_W), vec,
                  spec(ATT_W), spec(GLA_W), spec(GLA_W), spec(GLA_W, OFF_GGATE // GLA_W), vec],
        out_specs=spec(mix_w),
        compiler_params=_cparams(("arbitrary", "arbitrary")),
        name="merge_heads",
    )(dn_f, dn_b, proj, dn_g.reshape(1, DN_D), att, gl_f, gl_b, proj, gla_g.reshape(1, GLA_DV))


def _split_points():
    pts, acc = [], 0
    for sz in IN_SIZES[:-1]:
        acc += sz
        pts.append(acc)
    return pts


def _relayout_w_in(w):
    d = w.shape[0]
    (dq, dk, dv, dgate, a_f, a_b, b_f, b_b, aq, ak, av, gq, gkk, gv, ggate, lr_f, lr_b) = jnp.split(
        w, _split_points(), axis=1)
    small = jnp.concatenate([a_f, a_b, b_f, b_b, lr_f, lr_b], axis=1)
    small = jnp.pad(small, ((0, 0), (0, LANES - small.shape[1])))
    cols = jnp.concatenate([dq, dk, dv, dgate, aq, ak, av, gq, gv, ggate, gkk,
                            jnp.zeros((d, PROJ_COLS - OFF_PAD), w.dtype)], axis=1)
    return cols.astype(BF16)[None, None], small.astype(BF16)[None, None]


def _rope_tables(s):
    rows = s // GRID_W
    row = jnp.repeat(jnp.arange(rows, dtype=jnp.int32), GRID_W).astype(F32)
    col = jnp.tile(jnp.arange(GRID_W, dtype=jnp.int32), rows).astype(F32)
    inv_freq = ROPE_THETA ** (-jnp.arange(0, ROPE_SUB, 2, dtype=F32) / ROPE_SUB)
    ang_r = row[:, None] * inv_freq[None, :]
    ang_c = col[:, None] * inv_freq[None, :]
    cos = jnp.concatenate([jnp.cos(ang_r), jnp.cos(ang_r), jnp.cos(ang_c), jnp.cos(ang_c)], axis=1)
    sin = jnp.concatenate([-jnp.sin(ang_r), jnp.sin(ang_r), -jnp.sin(ang_c), jnp.sin(ang_c)], axis=1)
    return cos, sin


def _dn_params(dn_a_log, dn_dt_bias):
    rows = jnp.stack([dn_a_log.reshape(-1), dn_dt_bias.reshape(-1)], axis=0)
    return jnp.pad(rows, ((0, 0), (SM_A, LANES - SM_A - 2 * DN_HEADS)))


def _gla_params(gla_up, gla_up_b):
    ups = []
    for d in range(2):
        lo = SM_LR + d * GLA_RANK
        ups.append(jnp.pad(gla_up[d], ((lo, LANES - lo - GLA_RANK), (0, 0))))
    return jnp.stack(ups, axis=0), gla_up_b[:, None, :]


def _mixer(x16, bsz, s, w_in, dn_conv, dn_a_log, dn_dt_bias, dn_norm_g, att_qn_g, att_kn_g,
           gla_up, gla_up_b, gla_norm_g, rope):
    t = bsz * s
    bm = _tile(t, 1024, 16)
    one_group = _block_table(jnp.zeros((t // bm,), jnp.int32))
    w_main, w_small = _relayout_w_in(w_in)
    proj = grouped_matmul(x16, w_main, 0, one_group, bm, 1024, BF16).reshape(bsz, s, PROJ_COLS)
    small = grouped_matmul(x16, w_small, 0, one_group, bm, LANES, F32).reshape(bsz, s, LANES)

    q, k, v = dn_prep(proj, dn_conv)
    u, wq, ktt, qk, egl = dn_wy(q, k, v, small, _dn_params(dn_a_log, dn_dt_bias))
    dn_f, dn_b = dn_rec(u, wq, ktt, qk, egl)

    cos, sin_signed = rope
    aq, ak, av, kn = att_prep(proj, cos, sin_signed, att_qn_g, att_kn_g)
    o_att = flash_attention(aq, ak, av, kn)

    gl_f, gl_b = gla_scan(proj, small, *_gla_params(gla_up, gla_up_b))

    return merge_heads(dn_f, dn_b, proj, dn_norm_g, o_att, gl_f, gl_b, gla_norm_g).reshape(t, -1)


def _moe(x32, x16, router_w, w_gate, w_up, w_down, layer, ln_g, ln_b, alpha):
    t, d = x32.shape
    rw = jnp.pad(router_w, ((0, 0), (0, LANES - N_EXPERTS)))
    logits = router_logits(x32, rw)[:, :N_EXPERTS]
    top_val, top_idx = lax.top_k(logits, TOP_K)
    gates = jax.nn.softmax(top_val, axis=-1)
    e_flat = top_idx.reshape(-1).astype(jnp.int32)
    tok_flat = jnp.repeat(jnp.arange(t, dtype=jnp.int32), TOP_K)
    onehot = (e_flat[:, None] == jnp.arange(N_EXPERTS, dtype=jnp.int32)[None, :]).astype(jnp.int32)
    csum = jnp.cumsum(onehot, axis=0)
    counts = csum[-1]
    rank = jnp.take_along_axis(csum, e_flat[:, None], axis=1)[:, 0] - 1
    padded = (counts + MOE_BLOCK - 1) // MOE_BLOCK * MOE_BLOCK
    pstart = jnp.cumsum(padded) - padded
    pend = pstart + padded
    dest = pstart[e_flat] + rank
    nb = -(-(TOP_K * t) // MOE_BLOCK) + N_EXPERTS
    cap = nb * MOE_BLOCK
    buf_tok = jnp.zeros((cap,), jnp.int32).at[dest].set(tok_flat)
    block_start = jnp.arange(nb, dtype=jnp.int32) * MOE_BLOCK
    block_exp = jnp.minimum(jnp.sum(block_start[:, None] >= pend[None, :], axis=-1), N_EXPERTS - 1)
    table = _block_table(block_exp, block_start < pend[-1])
    xb = x16[buf_tok]
    h = grouped_swiglu(xb, w_gate, w_up, layer, table, MOE_BLOCK, 1024)
    yb = grouped_matmul(h, w_down, layer, table, MOE_BLOCK, 512, BF16)
    dest_by_choice = dest.reshape(t, TOP_K).T.reshape(-1)
    return combine_ln(x32, yb[dest_by_choice], gates, ln_g, ln_b, alpha)


def _dense_ffn(x32, x16, w_gate, w_up, w_down, layer, ln_g, ln_b, alpha):
    t = x32.shape[0]
    bm = _tile(t, 1024, 16)
    h = grouped_swiglu(x16, w_gate, w_up, layer, _block_table(jnp.zeros((t // bm,), jnp.int32)), bm, 512)
    return matmul_res_ln(h, w_down[layer].astype(BF16), x32, ln_g, ln_b, alpha)


def kernel(x, w_in, dn_conv, dn_a_log, dn_dt_bias, dn_norm_g, att_qn_g, att_kn_g, gla_up, gla_up_b,
           gla_norm_g, w_out, ln1_g, ln1_b, ln2_g, ln2_b, ffn_w_gate, ffn_w_up, ffn_w_down, router_w,
           exp_w_gate, exp_w_up, exp_w_down):
    bsz, s, d = x.shape
    depth = w_in.shape[0]
    alpha = (2.0 * depth) ** 0.25
    t = bsz * s
    rope = _rope_tables(s)
    x32 = x.reshape(t, d)
    x16 = x32.astype(BF16)
    for layer in range(depth):
        mix = _mixer(x16, bsz, s, w_in[layer], dn_conv[layer], dn_a_log[layer], dn_dt_bias[layer],
                     dn_norm_g[layer], att_qn_g[layer], att_kn_g[layer], gla_up[layer],
                     gla_up_b[layer], gla_norm_g[layer], rope)
        x32, x16 = matmul_res_ln(mix, w_out[layer].astype(BF16), x32, ln1_g[layer], ln1_b[layer], alpha)
        j = layer // 2
        if layer % 2 == 0:
            x32, x16 = _dense_ffn(x32, x16, ffn_w_gate, ffn_w_up, ffn_w_down, j,
                                  ln2_g[layer], ln2_b[layer], alpha)
        else:
            x32, x16 = _moe(x32, x16, router_w[j], exp_w_gate, exp_w_up, exp_w_down, j,
                            ln2_g[layer], ln2_b[layer], alpha)
    return x32.reshape(bsz, s, d)
```

```python
import functools
import math

import jax
import jax.numpy as jnp
from jax import lax
from jax.experimental import pallas as pl
from jax.experimental.pallas import tpu as pltpu

F32 = jnp.float32
BF16 = jnp.bfloat16
HIGHEST = lax.Precision.HIGHEST

DN_HEADS, DN_D = 6, 128
ATT_HEADS, ATT_KV_HEADS, ATT_DH = 6, 2, 128
ATT_GROUP = ATT_HEADS // ATT_KV_HEADS
ROPE_SUB, ROPE_THETA, GRID_W = 64, 10000.0, 64
GLA_HEADS, GLA_DK, GLA_DV, GLA_RANK = 4, 64, 128, 16
GLA_NORMALIZER = 16.0
CHUNK = 64
SUB = 8
MASKED_LOG = -1e30
WY_ROWS = 2 * CHUNK
EGL_ROWS = 8
CONV_W = 5
N_EXPERTS, TOP_K, MOE_BLOCK = 8, 2, 256
EPS = 1e-6
LOG2E = 1.4426950408889634
LANES = 128
HALO = 16

DN_W = DN_HEADS * DN_D
ATT_W = ATT_HEADS * ATT_DH
ATT_KV_W = ATT_KV_HEADS * ATT_DH
GLA_QK = GLA_HEADS * GLA_DK
GLA_W = GLA_HEADS * GLA_DV
IN_SIZES = (DN_W, DN_W, DN_W, DN_W, DN_HEADS, DN_HEADS, DN_HEADS, DN_HEADS,
            ATT_W, ATT_KV_W, ATT_KV_W, GLA_QK, GLA_QK, GLA_W, GLA_W, GLA_RANK, GLA_RANK)
OFF_DQ, OFF_DGATE, OFF_AQ, OFF_AK, OFF_AV = 0, 2304, 3072, 3840, 4096
OFF_GQ, OFF_GV, OFF_GGATE, OFF_GK, OFF_PAD = 4352, 4608, 5120, 5632, 5888
PROJ_COLS = 6144
SM_A, SM_B, SM_LR = 0, 2 * DN_HEADS, 4 * DN_HEADS

VMEM_LIMIT = 56 * 1024 * 1024


def _cparams(sem, vmem=VMEM_LIMIT):
    return pltpu.CompilerParams(dimension_semantics=sem, vmem_limit_bytes=vmem)


def _tile(n, pref, quantum=LANES):
    if n <= pref:
        return n
    t = pref - pref % quantum
    while n % t:
        t -= quantum
    return t


def _bdot(a, b):
    return jnp.dot(a.astype(BF16), b.astype(BF16), preferred_element_type=F32)


def _bdot_nt(a, b):
    return lax.dot_general(a.astype(BF16), b.astype(BF16), (((1,), (1,)), ((), ())),
                           preferred_element_type=F32)


def _hdot(a, b):
    return jnp.dot(a, b, preferred_element_type=F32, precision=HIGHEST)


def _block_table(group, used=None):
    used = jnp.ones_like(group) if used is None else used.astype(jnp.int32)
    return jnp.stack([group.astype(jnp.int32), used], axis=0)


def _group_changed(be_ref):
    i = pl.program_id(1)
    return (i == 0) | (be_ref[0, i] != be_ref[0, jnp.maximum(i - 1, 0)])


def _if_used(be_ref, o_ref, compute):
    used = be_ref[1, pl.program_id(1)] != 0

    @pl.when(used)
    def _():
        compute()

    @pl.when(jnp.logical_not(used))
    def _():
        o_ref[...] = jnp.zeros_like(o_ref)


def _gmm_body(be_ref, x_ref, w_ref, o_ref, *w16):
    if w16:
        @pl.when(_group_changed(be_ref))
        def _():
            w16[0][...] = w_ref[...].astype(BF16)
    w_use = w16[0] if w16 else w_ref

    def compute():
        o_ref[...] = jnp.dot(x_ref[...], w_use[...], preferred_element_type=F32).astype(o_ref.dtype)

    _if_used(be_ref, o_ref, compute)


def _weight_spec(w, layer, k, tn):
    if w.ndim == 3:
        return pl.BlockSpec((None, k, tn), lambda j, i, be: (layer, 0, j))
    return pl.BlockSpec((None, None, k, tn), lambda j, i, be: (layer, be[0, i], 0, j))


def grouped_matmul(x, w, layer, block_table, bm, tn, out_dtype):
    m, k = x.shape
    n = w.shape[-1]
    tn = _tile(n, tn)
    scratch = [pltpu.VMEM((k, tn), BF16)] if w.dtype != BF16 else []
    return pl.pallas_call(
        _gmm_body,
        out_shape=jax.ShapeDtypeStruct((m, n), out_dtype),
        grid_spec=pltpu.PrefetchScalarGridSpec(
            num_scalar_prefetch=1, grid=(n // tn, m // bm),
            in_specs=[pl.BlockSpec((bm, k), lambda j, i, be: (i, 0)), _weight_spec(w, layer, k, tn)],
            out_specs=pl.BlockSpec((bm, tn), lambda j, i, be: (i, j)),
            scratch_shapes=scratch),
        compiler_params=_cparams(("arbitrary", "arbitrary")),
        name="grouped_matmul",
    )(block_table, x, w)


def _gswiglu_body(be_ref, x_ref, wg_ref, wu_ref, o_ref, wg16, wu16):
    @pl.when(_group_changed(be_ref))
    def _():
        wg16[...] = wg_ref[...].astype(BF16)
        wu16[...] = wu_ref[...].astype(BF16)

    def compute():
        x = x_ref[...]
        g = jnp.dot(x, wg16[...], preferred_element_type=F32)
        u = jnp.dot(x, wu16[...], preferred_element_type=F32)
        o_ref[...] = (g * jax.nn.sigmoid(g) * u).astype(o_ref.dtype)

    _if_used(be_ref, o_ref, compute)


def grouped_swiglu(x, wg, wu, layer, block_table, bm, tn):
    m, k = x.shape
    n = wg.shape[-1]
    tn = _tile(n, tn)
    return pl.pallas_call(
        _gswiglu_body,
        out_shape=jax.ShapeDtypeStruct((m, n), BF16),
        grid_spec=pltpu.PrefetchScalarGridSpec(
            num_scalar_prefetch=1, grid=(n // tn, m // bm),
            in_specs=[pl.BlockSpec((bm, k), lambda j, i, be: (i, 0)),
                      _weight_spec(wg, layer, k, tn), _weight_spec(wu, layer, k, tn)],
            out_specs=pl.BlockSpec((bm, tn), lambda j, i, be: (i, j)),
            scratch_shapes=[pltpu.VMEM((k, tn), BF16), pltpu.VMEM((k, tn), BF16)]),
        compiler_params=_cparams(("arbitrary", "arbitrary")),
        name="grouped_swiglu",
    )(block_table, x, wg, wu)


def _res_ln(res, y, g, b, alpha):
    z = alpha * res + y
    mu = jnp.mean(z, axis=-1, keepdims=True)
    zc = z - mu
    var = jnp.mean(zc * zc, axis=-1, keepdims=True)
    return zc * lax.rsqrt(var + EPS) * g + b


def _mm_res_ln_body(x_ref, w_ref, res_ref, g_ref, b_ref, o32_ref, o16_ref, *acc, nk, alpha):
    kk = pl.program_id(1)
    part = jnp.dot(x_ref[...], w_ref[...], preferred_element_type=F32)

    def finish(y):
        out = _res_ln(res_ref[...], y, g_ref[...], b_ref[...], alpha)
        o32_ref[...] = out
        o16_ref[...] = out.astype(BF16)

    if nk == 1:
        finish(part)
        return
    acc_ref = acc[0]

    @pl.when(kk == 0)
    def _():
        acc_ref[...] = part

    @pl.when((kk > 0) & (kk < nk - 1))
    def _():
        acc_ref[...] += part

    @pl.when(kk == nk - 1)
    def _():
        finish(acc_ref[...] + part)


def matmul_res_ln(x, w, res, g, b, alpha, tm=512, tk=2048):
    m, k = x.shape
    n = w.shape[-1]
    tm = _tile(m, tm, 8)
    tk = _tile(k, tk)
    nk = k // tk
    return pl.pallas_call(
        functools.partial(_mm_res_ln_body, nk=nk, alpha=alpha),
        out_shape=(jax.ShapeDtypeStruct((m, n), F32), jax.ShapeDtypeStruct((m, n), BF16)),
        grid=(m // tm, nk),
        in_specs=[pl.BlockSpec((tm, tk), lambda i, kk: (i, kk)),
                  pl.BlockSpec((tk, n), lambda i, kk: (kk, 0)),
                  pl.BlockSpec((tm, n), lambda i, kk: (i, 0)),
                  pl.BlockSpec((1, n), lambda i, kk: (0, 0)),
                  pl.BlockSpec((1, n), lambda i, kk: (0, 0))],
        out_specs=(pl.BlockSpec((tm, n), lambda i, kk: (i, 0)),
                   pl.BlockSpec((tm, n), lambda i, kk: (i, 0))),
        scratch_shapes=[pltpu.VMEM((tm, n), F32)] if nk > 1 else [],
        compiler_params=_cparams(("arbitrary", "arbitrary")),
        name="matmul_res_ln",
    )(x, w, res, g.reshape(1, n), b.reshape(1, n))


def _combine_ln_body(res_ref, *refs, alpha):
    y_refs, (gate_ref, g_ref, b_ref, o32_ref, o16_ref) = refs[:TOP_K], refs[TOP_K:]
    gate = gate_ref[...]
    y = y_refs[0][...].astype(F32) * gate[:, 0:1]
    for kk in range(1, TOP_K):
        y = y + y_refs[kk][...].astype(F32) * gate[:, kk:kk + 1]
    out = _res_ln(res_ref[...], y, g_ref[...], b_ref[...], alpha)
    o32_ref[...] = out
    o16_ref[...] = out.astype(BF16)


def combine_ln(res, y, gates, g, b, alpha, tm=512):
    m, n = res.shape
    tm = _tile(m, tm, 16)
    nblk = m // tm
    row = pl.BlockSpec((tm, n), lambda i: (i, 0))
    vec = pl.BlockSpec((1, n), lambda i: (0, 0))

    def choice(kk):
        return pl.BlockSpec((tm, n), lambda i: (kk * nblk + i, 0))

    return pl.pallas_call(
        functools.partial(_combine_ln_body, alpha=alpha),
        out_shape=(jax.ShapeDtypeStruct((m, n), F32), jax.ShapeDtypeStruct((m, n), BF16)),
        grid=(nblk,),
        in_specs=[row] + [choice(kk) for kk in range(TOP_K)] + [pl.BlockSpec((tm, TOP_K), lambda i: (i, 0)), vec, vec],
        out_specs=(row, row),
        compiler_params=_cparams(("arbitrary",)),
        name="combine_ln",
    )(res, *([y] * TOP_K), gates, g.reshape(1, n), b.reshape(1, n))


def _router_body(x_ref, w_ref, o_ref):
    o_ref[...] = _hdot(x_ref[...], w_ref[...])


def router_logits(x, w_pad, tm=512):
    m, k = x.shape
    n = w_pad.shape[-1]
    tm = _tile(m, tm, 8)
    return pl.pallas_call(
        _router_body,
        out_shape=jax.ShapeDtypeStruct((m, n), F32),
        grid=(m // tm,),
        in_specs=[pl.BlockSpec((tm, k), lambda i: (i, 0)), pl.BlockSpec((k, n), lambda i: (0, 0))],
        out_specs=pl.BlockSpec((tm, n), lambda i: (i, 0)),
        compiler_params=_cparams(("arbitrary",)),
        name="router_logits",
    )(x, w_pad)


def _dn_prep_body(cur_ref, prev_ref, next_ref, w_ref, q_ref, k_ref, v_ref, buf_ref, *, ts, nblk, sub):
    i = pl.program_id(1)
    buf_ref[0:HALO, :] = jnp.where(i > 0, prev_ref[0], jnp.zeros_like(prev_ref[0]))
    buf_ref[HALO:HALO + ts, :] = cur_ref[0]
    buf_ref[HALO + ts:2 * HALO + ts, :] = jnp.where(i < nblk - 1, next_ref[0], jnp.zeros_like(next_ref[0]))
    win = sub + 2 * HALO
    ri = lax.broadcasted_iota(jnp.int32, (sub, win), 0)
    ci = lax.broadcasted_iota(jnp.int32, (sub, win), 1)
    centre = CONV_W // 2
    shift = {j: (ci == ri + HALO + j - centre).astype(BF16) for j in range(CONV_W) if j != centre}
    for r in range(ts // sub):
        window = buf_ref[r * sub:r * sub + win, :]
        acc = window[HALO:HALO + sub, :].astype(F32) * w_ref[centre:centre + 1, :]
        for j, sel in shift.items():
            acc = acc + jnp.dot(sel, window, preferred_element_type=F32) * w_ref[j:j + 1, :]
        y = acc * jax.nn.sigmoid(acc)
        rows = slice(r * sub, (r + 1) * sub)
        for h in range(DN_HEADS):
            for off, ref in ((0, q_ref), (DN_W, k_ref)):
                t = y[:, off + h * DN_D:off + (h + 1) * DN_D]
                ref[0, rows, h * DN_D:(h + 1) * DN_D] = t * lax.rsqrt(jnp.sum(t * t, axis=-1, keepdims=True) + EPS)
        v_ref[0, rows, :] = y[:, 2 * DN_W:3 * DN_W]


def dn_prep(proj, conv_w, ts=512):
    b, s, _ = proj.shape
    ts = _tile(s, ts, HALO)
    sub = _tile(ts, LANES, HALO)
    nblk = s // ts
    c = 3 * DN_W
    hb = ts // HALO
    out = jax.ShapeDtypeStruct((b, s, DN_W), F32)
    ospec = pl.BlockSpec((1, ts, DN_W), lambda bb, i: (bb, i, 0))
    return pl.pallas_call(
        functools.partial(_dn_prep_body, ts=ts, nblk=nblk, sub=sub),
        out_shape=(out, out, out),
        grid=(b, nblk),
        in_specs=[pl.BlockSpec((1, ts, c), lambda bb, i: (bb, i, 0)),
                  pl.BlockSpec((1, HALO, c), lambda bb, i: (bb, jnp.maximum(i * hb - 1, 0), 0)),
                  pl.BlockSpec((1, HALO, c), lambda bb, i: (bb, jnp.minimum((i + 1) * hb, nblk * hb - 1), 0)),
                  pl.BlockSpec((CONV_W, c), lambda bb, i: (0, 0))],
        out_specs=(ospec, ospec, ospec),
        scratch_shapes=[pltpu.VMEM((ts + 2 * HALO, c), BF16)],
        compiler_params=_cparams(("arbitrary", "arbitrary")),
        name="dn_prep",
    )(proj, proj, proj, conv_w)


def _dn_wy_body(q_ref, k_ref, v_ref, sm_ref, par_ref, u_ref, wq_ref, ktt_ref, qk_ref, egl_ref, *, tiles):
    def tile(t, carry):
        _dn_wy_tile(t, q_ref, k_ref, v_ref, sm_ref, par_ref, u_ref, wq_ref, ktt_ref, qk_ref, egl_ref)
        return carry

    lax.fori_loop(0, tiles, tile, 0)


def _dn_wy_tile(t, q_ref, k_ref, v_ref, sm_ref, par_ref, u_ref, wq_ref, ktt_ref, qk_ref, egl_ref):
    r = WY_ROWS
    nck = r // CHUNK
    rows_t = pl.ds(pl.multiple_of(t * r, r), r)
    sm = sm_ref[0, rows_t, :]
    par = par_ref[...]
    g_all = -jnp.exp(par[0:1, :]) * jax.nn.softplus(sm + par[1:2, :])
    beta_all = jax.nn.sigmoid(sm)
    ri = lax.broadcasted_iota(jnp.int32, (r, r), 0)
    ci = lax.broadcasted_iota(jnp.int32, (r, r), 1)
    same = (ri // CHUNK) == (ci // CHUNK)
    eye = (ri == ci).astype(F32)
    masks = []
    for d in range(2):
        delta = ri - ci if d == 0 else ci - ri
        masks.append((same & (delta >= 0), same & (delta > 0)))
    scale = DN_D ** -0.5
    gtot = _hdot(same.astype(F32), g_all)
    gcs = []
    for d in range(2):
        gc = _hdot(masks[d][0].astype(F32), g_all)
        gcs.append((gc, gc.T))
    chains = [(h, d) for h in range(DN_HEADS) for d in range(2)]
    qk_cols = ([], [])
    nm, tinv, rhs = {}, {}, {}
    for h in range(DN_HEADS):
        sl = slice(h * DN_D, (h + 1) * DN_D)
        q = q_ref[0, rows_t, sl] * scale
        k = k_ref[0, rows_t, sl]
        v = v_ref[0, rows_t, sl]
        kq = _bdot_nt(jnp.concatenate([k, q], axis=0), k)
        kk, qk = kq[:r], kq[r:]
        for d in range(2):
            lane = SM_A + d * DN_HEADS + h
            incl, strict = masks[d]
            gc, gct = gcs[d]
            gcol = jnp.broadcast_to(gc[:, lane:lane + 1], (r, r))
            gt = jnp.broadcast_to(gtot[:, lane:lane + 1], (r, r))
            bcol = jnp.broadcast_to(beta_all[:, SM_B + d * DN_HEADS + h:SM_B + d * DN_HEADS + h + 1], (r, r))
            grow = gct[lane:lane + 1, :]
            decay = jnp.where(incl, jnp.exp(jnp.where(incl, gcol - grow, 0.0)), 0.0)
            nm[h, d] = jnp.where(strict, -(bcol * kk) * decay, 0.0)
            tinv[h, d] = eye + nm[h, d]
            eg = jnp.exp(gcol)
            rhs[h, d] = jnp.concatenate([v * bcol, k * (bcol * eg)], axis=1).astype(BF16)
            qd16 = (q * eg).astype(BF16)
            ktt = _bdot_nt(eye, (k * jnp.exp(gt - gcol)).astype(BF16)).astype(BF16)
            for c in range(nck):
                rows = slice(c * CHUNK, (c + 1) * CHUNK)
                wq_ref[d, 0, t * nck + c, CHUNK:2 * CHUNK, sl] = qd16[rows]
                ktt_ref[d, 0, t * nck + c, :, h * CHUNK:(h + 1) * CHUNK] = ktt[:, rows]
            qkd = qk * decay
            qk_cols[d].append(qkd[:, :CHUNK] + qkd[:, CHUNK:])
            egt = jnp.exp(gt)
            for c in range(nck):
                egl_ref[d, 0, t * nck + c, :, sl] = egt[c * CHUNK:c * CHUNK + EGL_ROWS]
    for d in range(2):
        qk_ref[d, 0, rows_t, :] = jnp.concatenate(qk_cols[d], axis=1).astype(BF16)
    for _ in range(int(math.log2(CHUNK)) - 1):
        for hd in chains:
            nm[hd] = _bdot(nm[hd], nm[hd])
        for hd in chains:
            tinv[hd] = tinv[hd] + _bdot(tinv[hd], nm[hd])
    for h, d in chains:
        sl = slice(h * DN_D, (h + 1) * DN_D)
        uw = _bdot(tinv[h, d], rhs[h, d])
        u_ref[d, 0, rows_t, sl] = uw[:, :DN_D].astype(BF16)
        w16 = uw[:, DN_D:].astype(BF16)
        for c in range(nck):
            wq_ref[d, 0, t * nck + c, 0:CHUNK, sl] = w16[c * CHUNK:(c + 1) * CHUNK]


def dn_wy(q, k, v, small, par):
    b, s, w = q.shape
    nchunk = s // CHUNK
    tiles = max(c for c in (4, 2, 1) if s % (c * WY_ROWS) == 0)
    r = tiles * WY_ROWS
    nck = r // CHUNK
    seq = pl.BlockSpec((1, r, w), lambda bb, i: (bb, i, 0))
    dseq = pl.BlockSpec((2, 1, r, w), lambda bb, i: (0, bb, i, 0))
    return pl.pallas_call(
        functools.partial(_dn_wy_body, tiles=tiles),
        out_shape=(jax.ShapeDtypeStruct((2, b, s, w), BF16),
                   jax.ShapeDtypeStruct((2, b, nchunk, 2 * CHUNK, w), BF16),
                   jax.ShapeDtypeStruct((2, b, nchunk, DN_D, DN_HEADS * CHUNK), BF16),
                   jax.ShapeDtypeStruct((2, b, s, DN_HEADS * CHUNK), BF16),
                   jax.ShapeDtypeStruct((2, b, nchunk, EGL_ROWS, w), F32)),
        grid=(b, s // r),
        in_specs=[seq, seq, seq,
                  pl.BlockSpec((1, r, LANES), lambda bb, i: (bb, i, 0)),
                  pl.BlockSpec((2, LANES), lambda bb, i: (0, 0))],
        out_specs=(dseq,
                   pl.BlockSpec((2, 1, nck, 2 * CHUNK, w), lambda bb, i: (0, bb, i, 0, 0)),
                   pl.BlockSpec((2, 1, nck, DN_D, DN_HEADS * CHUNK), lambda bb, i: (0, bb, i, 0, 0)),
                   pl.BlockSpec((2, 1, r, DN_HEADS * CHUNK), lambda bb, i: (0, bb, i, 0)),
                   pl.BlockSpec((2, 1, nck, EGL_ROWS, w), lambda bb, i: (0, bb, i, 0, 0))),
        compiler_params=_cparams(("arbitrary", "arbitrary")),
        name="dn_wy",
    )(q, k, v, small, par)


def _dn_rec_body(*refs, nc):
    ins = (refs[0:5], refs[5:10])
    outs = refs[10:12]
    state_ref = refs[12]
    n = pl.program_id(1)

    @pl.when(n == 0)
    def _():
        state_ref[...] = jnp.zeros_like(state_ref)

    chains = [(d, h) for d in range(2) for h in range(DN_HEADS)]
    state = {(d, h): state_ref[d, h] for d, h in chains}
    for step in range(nc):
        chunk = (step, nc - 1 - step)
        wq, v_new = {}, {}
        for d, h in chains:
            sl = slice(h * DN_D, (h + 1) * DN_D)
            wq[d, h] = jnp.dot(ins[d][1][0, 0, chunk[d], :, sl], state[d, h].astype(BF16),
                               preferred_element_type=F32)
        for d, h in chains:
            sl = slice(h * DN_D, (h + 1) * DN_D)
            rows = slice(chunk[d] * CHUNK, (chunk[d] + 1) * CHUNK)
            v_new[d, h] = (ins[d][0][0, 0, rows, sl].astype(F32) - wq[d, h][:CHUNK]).astype(BF16)
        for d, h in chains:
            sl = slice(h * DN_D, (h + 1) * DN_D)
            rows = slice(chunk[d] * CHUNK, (chunk[d] + 1) * CHUNK)
            qk = ins[d][3][0, 0, rows, h * CHUNK:(h + 1) * CHUNK]
            outs[d][0, rows, sl] = (wq[d, h][CHUNK:]
                                    + jnp.dot(qk, v_new[d, h], preferred_element_type=F32)).astype(outs[d].dtype)
        for d, h in chains:
            sl = slice(h * DN_D, (h + 1) * DN_D)
            ktt = ins[d][2][0, 0, chunk[d], :, h * CHUNK:(h + 1) * CHUNK]
            state[d, h] = (state[d, h] * ins[d][4][0, 0, chunk[d], 0:1, sl]
                           + jnp.dot(ktt, v_new[d, h], preferred_element_type=F32))
    for d, h in chains:
        state_ref[d, h] = state[d, h]


def dn_rec(u, wq, ktt, qk, egl):
    _, b, s, w = u.shape
    nchunk = s // CHUNK
    nc = max(c for c in (4, 2, 1) if nchunk % c == 0)
    nblk = nchunk // nc
    rows = nc * CHUNK

    def specs(d):
        def blk(n):
            return n if d == 0 else nblk - 1 - n
        return [pl.BlockSpec((1, 1, rows, w), lambda bb, n: (d, bb, blk(n), 0)),
                pl.BlockSpec((1, 1, nc, 2 * CHUNK, w), lambda bb, n: (d, bb, blk(n), 0, 0)),
                pl.BlockSpec((1, 1, nc, DN_D, DN_HEADS * CHUNK), lambda bb, n: (d, bb, blk(n), 0, 0)),
                pl.BlockSpec((1, 1, rows, DN_HEADS * CHUNK), lambda bb, n: (d, bb, blk(n), 0)),
                pl.BlockSpec((1, 1, nc, EGL_ROWS, w), lambda bb, n: (d, bb, blk(n), 0, 0))]

    out = jax.ShapeDtypeStruct((b, s, w), BF16)
    return pl.pallas_call(
        functools.partial(_dn_rec_body, nc=nc),
        out_shape=(out, out),
        grid=(b, nblk),
        in_specs=specs(0) + specs(1),
        out_specs=(pl.BlockSpec((1, rows, w), lambda bb, n: (bb, n, 0)),
                   pl.BlockSpec((1, rows, w), lambda bb, n: (bb, nblk - 1 - n, 0))),
        scratch_shapes=[pltpu.VMEM((2, DN_HEADS, DN_D, DN_D), F32)],
        compiler_params=_cparams(("arbitrary", "arbitrary")),
        name="dn_rec",
    )(u, wq, ktt, qk, egl, u, wq, ktt, qk, egl)


def _rope_head(x, g, cos, sin_signed, first_half):
    x = x.astype(F32)
    xf = x * lax.rsqrt(jnp.mean(x * x, axis=-1, keepdims=True) + EPS) * g
    partner = jnp.where(first_half, pltpu.roll(xf, LANES - ROPE_SUB // 2, axis=1),
                        pltpu.roll(xf, ROPE_SUB // 2, axis=1))
    return xf * cos + partner * sin_signed


def _att_prep_body(q_ref, k_ref, v_ref, cos_ref, sin_ref, qg_ref, kg_ref, qo_ref, ko_ref, vo_ref, kn_ref):
    cos = cos_ref[...]
    sin_signed = sin_ref[...]
    lane = lax.broadcasted_iota(jnp.int32, cos.shape, 1)
    first_half = (lane % ROPE_SUB) < ROPE_SUB // 2
    scale = ATT_DH ** -0.5 * LOG2E
    for h in range(ATT_HEADS):
        sl = slice(h * ATT_DH, (h + 1) * ATT_DH)
        qo_ref[0, :, sl] = (_rope_head(q_ref[0, :, sl], qg_ref[...], cos, sin_signed, first_half)
                            * scale).astype(BF16)
    for h in range(ATT_KV_HEADS):
        sl = slice(h * ATT_DH, (h + 1) * ATT_DH)
        k16 = _rope_head(k_ref[0, :, sl], kg_ref[...], cos, sin_signed, first_half).astype(BF16)
        ko_ref[0, :, sl] = k16
        kf = k16.astype(F32)
        n2 = jnp.max(jnp.sum(kf * kf, axis=-1, keepdims=True), axis=0, keepdims=True)
        kn_ref[0, 0, :, sl] = jnp.broadcast_to(n2, (1, ATT_DH))
    for h in range(ATT_KV_HEADS):
        vo_ref[0, :, 2 * h * ATT_DH:(2 * h + 1) * ATT_DH] = v_ref[0, :, h * ATT_DH:(h + 1) * ATT_DH]
        vo_ref[0, :, (2 * h + 1) * ATT_DH:(2 * h + 2) * ATT_DH] = jnp.ones((v_ref.shape[1], ATT_DH), BF16)


def att_prep(proj, cos, sin_signed, qg, kg, ts=512):
    b, s, _ = proj.shape
    ts = _tile(s, ts, 16)
    tab = pl.BlockSpec((ts, ATT_DH), lambda bb, i: (i, 0))
    vec = pl.BlockSpec((1, ATT_DH), lambda bb, i: (0, 0))
    return pl.pallas_call(
        _att_prep_body,
        out_shape=(jax.ShapeDtypeStruct((b, s, ATT_W), BF16),
                   jax.ShapeDtypeStruct((b, s, ATT_KV_W), BF16),
                   jax.ShapeDtypeStruct((b, s, 2 * ATT_KV_W), BF16),
                   jax.ShapeDtypeStruct((b, s // ts, 1, ATT_KV_W), F32)),
        grid=(b, s // ts),
        in_specs=[pl.BlockSpec((1, ts, ATT_W), lambda bb, i: (bb, i, OFF_AQ // ATT_W)),
                  pl.BlockSpec((1, ts, ATT_KV_W), lambda bb, i: (bb, i, OFF_AK // ATT_KV_W)),
                  pl.BlockSpec((1, ts, ATT_KV_W), lambda bb, i: (bb, i, OFF_AV // ATT_KV_W)),
                  tab, tab, vec, vec],
        out_specs=(pl.BlockSpec((1, ts, ATT_W), lambda bb, i: (bb, i, 0)),
                   pl.BlockSpec((1, ts, ATT_KV_W), lambda bb, i: (bb, i, 0)),
                   pl.BlockSpec((1, ts, 2 * ATT_KV_W), lambda bb, i: (bb, i, 0)),
                   pl.BlockSpec((1, 1, 1, ATT_KV_W), lambda bb, i: (bb, i, 0, 0))),
        compiler_params=_cparams(("arbitrary", "arbitrary")),
        name="att_prep",
    )(proj, proj, proj, cos, sin_signed, qg.reshape(1, ATT_DH), kg.reshape(1, ATT_DH))


MAX_FIXED_SHIFT = 60.0


def _flash_body(q_ref, k_ref, v_ref, kn_ref, o_ref, *, tk, nkv):
    tq = q_ref.shape[1]
    heads = range(ATT_GROUP)
    qs = [q_ref[0, :, h * ATT_DH:(h + 1) * ATT_DH] for h in heads]
    unroll = max(u for u in (4, 2, 1) if nkv % u == 0)

    def tiles(t):
        start = pl.multiple_of(t * tk, tk)
        return k_ref[0, pl.ds(start, tk), :], v_ref[0, pl.ds(start, tk), :]

    def scores(h, ks):
        return lax.dot_general(qs[h], ks, (((1,), (1,)), ((), ())), preferred_element_type=F32)

    def finish(accs):
        for h in heads:
            o_ref[0, :, h * ATT_DH:(h + 1) * ATT_DH] = (accs[h][:, :ATT_DH] / accs[h][:, ATT_DH:]).astype(o_ref.dtype)

    kmax = jnp.sqrt(jnp.max(kn_ref[0, :, 0, :], axis=0, keepdims=True))[:, 0:1]
    shifts = []
    for h in heads:
        qf = qs[h].astype(F32)
        shifts.append(jnp.sqrt(jnp.sum(qf * qf, axis=-1, keepdims=True)) * (kmax * 1.001))
    worst = shifts[0]
    for h in heads[1:]:
        worst = jnp.maximum(worst, shifts[h])
    fixed_ok = jnp.max(worst) <= MAX_FIXED_SHIFT

    @pl.when(fixed_ok)
    def _():
        def step(t, accs):
            ks, vs = tiles(t)
            return tuple(accs[h] + jnp.dot(jnp.exp2(scores(h, ks) - shifts[h]).astype(BF16), vs,
                                           preferred_element_type=F32) for h in heads)

        init = tuple(jnp.zeros((tq, 2 * ATT_DH), F32) for _ in heads)
        finish(lax.fori_loop(0, nkv, step, init, unroll=unroll))

    @pl.when(jnp.logical_not(fixed_ok))
    def _():
        def step(t, carry):
            ks, vs = tiles(t)
            new = []
            for h in heads:
                m, acc = carry[h]
                sc = scores(h, ks)
                m_new = jnp.maximum(m, jnp.max(sc, axis=-1, keepdims=True))
                p = jnp.exp2(sc - m_new)
                acc = jnp.exp2(m - m_new) * acc + jnp.dot(p.astype(BF16), vs, preferred_element_type=F32)
                new.append((m_new, acc))
            return tuple(new)

        init = tuple((jnp.full((tq, 1), -1e30, F32), jnp.zeros((tq, 2 * ATT_DH), F32)) for _ in heads)
        res = lax.fori_loop(0, nkv, step, init, unroll=unroll)
        finish([res[h][1] for h in heads])


def flash_attention(q, k, v, kn, tq=512, tk=512):
    b, s, _ = q.shape
    tq = _tile(s, tq, 16)
    tk = _tile(s, tk, 16)
    gw = ATT_GROUP * ATT_DH
    return pl.pallas_call(
        functools.partial(_flash_body, tk=tk, nkv=s // tk),
        out_shape=jax.ShapeDtypeStruct((b, s, ATT_W), BF16),
        grid=(b, ATT_KV_HEADS, s // tq),
        in_specs=[pl.BlockSpec((1, tq, gw), lambda bb, g, i: (bb, i, g)),
                  pl.BlockSpec((1, s, ATT_DH), lambda bb, g, i: (bb, 0, g)),
                  pl.BlockSpec((1, s, 2 * ATT_DH), lambda bb, g, i: (bb, 0, g)),
                  pl.BlockSpec((1, kn.shape[1], 1, ATT_DH), lambda bb, g, i: (bb, 0, 0, g))],
        out_specs=pl.BlockSpec((1, tq, gw), lambda bb, g, i: (bb, i, g)),
        compiler_params=_cparams(("arbitrary", "arbitrary", "arbitrary")),
        name="flash_attention",
    )(q, k, v, kn)


def _gla_chunk(qs, k, v, gk, state_ref, d, reverse):
    c = CHUNK
    nsub = c // SUB
    ri = lax.broadcasted_iota(jnp.int32, (c, c), 0)
    ci = lax.broadcasted_iota(jnp.int32, (c, c), 1)
    incl = (ci >= ri) if reverse else (ri >= ci)
    gc = _hdot(incl.astype(F32), gk)
    last = 0 if reverse else c - 1
    gl = gc[last:last + 1, :]
    qd = qs * jnp.exp2(gc)
    kt = k * jnp.exp2(gl - gc)
    egl = jnp.exp2(gl)
    row = lax.broadcasted_iota(jnp.int32, (c, GLA_QK), 0)
    yield

    head_of_lane = lax.broadcasted_iota(jnp.int32, (SUB, GLA_QK), 1) // GLA_DK
    a_off = [[] for _ in range(GLA_HEADS)]
    for i in range(nsub):
        yield
        if i == (nsub - 1 if reverse else 0):
            for h in range(GLA_HEADS):
                a_off[h].append(jnp.zeros((SUB, c), F32))
            continue
        first = (i + 1) * SUB - 1 if reverse else i * SUB
        rs = slice(i * SUB, (i + 1) * SUB)
        ref = gc[first:first + 1, :]
        qi = qs[rs, :] * jnp.exp2(gc[rs, :] - ref)
        early = (row > first) if reverse else (row < first)
        kf = k * jnp.exp2(jnp.where(early, ref - gc, MASKED_LOG))
        stacked = jnp.concatenate([jnp.where(head_of_lane == h, qi, 0.0) for h in range(GLA_HEADS)], axis=0)
        prod = _bdot_nt(stacked, kf)
        for h in range(GLA_HEADS):
            a_off[h].append(prod[h * SUB:(h + 1) * SUB])

    lane = lax.broadcasted_iota(jnp.int32, (LANES, 2 * GLA_DV), 0)
    col = lax.broadcasted_iota(jnp.int32, (LANES, 2 * GLA_DV), 1)
    head_sum = ((lane // GLA_DK) == (col // GLA_DV)).astype(BF16)
    sub_row = lax.broadcasted_iota(jnp.int32, (SUB, LANES), 0)
    npair = GLA_HEADS // 2
    terms = []
    for p in range(npair):
        ls = slice(p * LANES, (p + 1) * LANES)
        for i in range(nsub):
            yield
            rs = slice(i * SUB, (i + 1) * SUB)
            gci, qsi = gc[rs, ls], qs[rs, ls]
            for jj in range(SUB):
                j = i * SUB + jj
                later = (sub_row <= jj) if reverse else (sub_row >= jj)
                e = jnp.exp2(jnp.where(later, gci - gc[j:j + 1, ls], MASKED_LOG))
                terms.append(qsi * k[j:j + 1, ls] * e)
    yield
    r = jnp.dot(jnp.concatenate(terms, axis=0).astype(BF16), head_sum, preferred_element_type=F32)
    o_diag = []
    for p in range(npair):
        yield
        vs = slice(p * 2 * GLA_DV, (p + 1) * 2 * GLA_DV)
        blocks = []
        for i in range(nsub):
            base = (p * nsub + i) * SUB * SUB
            acc = r[base:base + SUB, :] * v[i * SUB:i * SUB + 1, vs]
            for jj in range(1, SUB):
                acc = acc + r[base + jj * SUB:base + (jj + 1) * SUB, :] * v[i * SUB + jj:i * SUB + jj + 1, vs]
            blocks.append(acc)
        o_diag.append(jnp.concatenate(blocks, axis=0))

    outs = []
    for h in range(GLA_HEADS):
        yield
        sl = slice(h * GLA_DK, (h + 1) * GLA_DK)
        vh = v[:, h * GLA_DV:(h + 1) * GLA_DV]
        st = state_ref[d, h]
        a = jnp.concatenate(a_off[h], axis=0)
        od = o_diag[h // 2][:, (h % 2) * GLA_DV:(h % 2 + 1) * GLA_DV]
        outs.append(_bdot_nt(qd[:, sl], st) + _bdot(a, vh) + od)
        state_ref[d, h] = st * egl[:, sl] + lax.dot_general(
            vh.astype(BF16), kt[:, sl].astype(BF16), (((0,), (0,)), ((), ())), preferred_element_type=F32)
    return jnp.concatenate(outs, axis=1)


def _gla_scan_body(qf_ref, kf_ref, vf_ref, smf_ref, qb_ref, kb_ref, vb_ref, smb_ref, up_ref, upb_ref,
                   of_ref, ob_ref, state_ref, *, nc):
    n = pl.program_id(1)

    @pl.when(n == 0)
    def _():
        state_ref[...] = jnp.zeros_like(state_ref)

    ins = ((qf_ref, kf_ref, vf_ref, smf_ref, of_ref), (qb_ref, kb_ref, vb_ref, smb_ref, ob_ref))

    def chunk_pair(t, carry):
        running = []
        for d, (q_ref, k_ref, v_ref, sm_ref, o_ref) in enumerate(ins):
            c = t if d == 0 else nc - 1 - t
            rows = pl.ds(pl.multiple_of(c * CHUNK, CHUNK), CHUNK)
            gk = jax.nn.log_sigmoid(_hdot(sm_ref[0, rows, :], up_ref[d]) + upb_ref[d]) * (LOG2E / GLA_NORMALIZER)
            qs = q_ref[0, rows, :].astype(F32) * (GLA_DK ** -0.5)
            gen = _gla_chunk(qs, k_ref[0, rows, :].astype(F32), v_ref[0, rows, :].astype(F32), gk,
                             state_ref, d, reverse=(d == 1))
            running.append((gen, o_ref, rows))
        while running:
            unfinished = []
            for gen, o_ref, rows in running:
                try:
                    next(gen)
                    unfinished.append((gen, o_ref, rows))
                except StopIteration as done:
                    o_ref[0, rows, :] = done.value.astype(o_ref.dtype)
            running = unfinished
        return carry

    lax.fori_loop(0, nc, chunk_pair, 0)


def gla_scan(proj, small, up_pad, upb):
    b, s, _ = proj.shape
    nchunk = s // CHUNK
    nc = max(c for c in (4, 2, 1) if nchunk % c == 0)
    nblk = nchunk // nc
    rows = nc * CHUNK

    def specs(d):
        def cidx(n):
            return n if d == 0 else nblk - 1 - n
        return [pl.BlockSpec((1, rows, GLA_QK), lambda bb, n: (bb, cidx(n), OFF_GQ // GLA_QK)),
                pl.BlockSpec((1, rows, GLA_QK), lambda bb, n: (bb, cidx(n), OFF_GK // GLA_QK)),
                pl.BlockSpec((1, rows, GLA_W), lambda bb, n: (bb, cidx(n), OFF_GV // GLA_W)),
                pl.BlockSpec((1, rows, LANES), lambda bb, n: (bb, cidx(n), 0))]

    out = jax.ShapeDtypeStruct((b, s, GLA_W), BF16)
    return pl.pallas_call(
        functools.partial(_gla_scan_body, nc=nc),
        out_shape=(out, out),
        grid=(b, nblk),
        in_specs=specs(0) + specs(1) + [pl.BlockSpec((2, LANES, GLA_QK), lambda bb, n: (0, 0, 0)),
                                        pl.BlockSpec((2, 1, GLA_QK), lambda bb, n: (0, 0, 0))],
        out_specs=(pl.BlockSpec((1, rows, GLA_W), lambda bb, n: (bb, n, 0)),
                   pl.BlockSpec((1, rows, GLA_W), lambda bb, n: (bb, nblk - 1 - n, 0))),
        scratch_shapes=[pltpu.VMEM((2, GLA_HEADS, GLA_DV, GLA_DK), F32)],
        compiler_params=_cparams(("arbitrary", "arbitrary")),
        name="gla_scan",
    )(proj, proj, proj, small, proj, proj, proj, small, up_pad, upb)


def _gated_norm(o, gate, g, heads, d):
    outs = []
    for h in range(heads):
        sl = slice(h * d, (h + 1) * d)
        t = o[:, sl]
        y = t * lax.rsqrt(jnp.mean(t * t, axis=-1, keepdims=True) + EPS) * g
        gt = gate[:, sl].astype(F32)
        outs.append(y * (gt * jax.nn.sigmoid(gt)))
    return jnp.concatenate(outs, axis=1)


def _merge_body(dnf_ref, dnb_ref, dgate_ref, dng_ref, att_ref, glf_ref, glb_ref, ggate_ref, glg_ref, o_ref):
    o_dn = _gated_norm(dnf_ref[0].astype(F32) + dnb_ref[0].astype(F32), dgate_ref[0], dng_ref[...],
                       DN_HEADS, DN_D)
    o_gla = _gated_norm(glf_ref[0].astype(F32) + glb_ref[0].astype(F32), ggate_ref[0], glg_ref[...],
                        GLA_HEADS, GLA_DV)
    o_ref[0, :, 0:DN_W] = o_dn.astype(BF16)
    o_ref[0, :, DN_W:DN_W + ATT_W] = att_ref[0]
    o_ref[0, :, DN_W + ATT_W:] = o_gla.astype(BF16)


def merge_heads(dn_f, dn_b, proj, dn_g, att, gl_f, gl_b, gla_g, ts=512):
    b, s, _ = proj.shape
    ts = _tile(s, ts, 16)
    mix_w = DN_W + ATT_W + GLA_W

    def spec(w, blk=0):
        return pl.BlockSpec((1, ts, w), lambda bb, i: (bb, i, blk))

    vec = pl.BlockSpec((1, LANES), lambda bb, i: (0, 0))
    return pl.pallas_call(
        _merge_body,
        out_shape=jax.ShapeDtypeStruct((b, s, mix_w), BF16),
        grid=(b, s // ts),
        in_specs=[spec(DN_W), spec(DN_W), spec(DN_W, OFF_DGATE // DN_W), vec,
                  spec(ATT_W), spec(GLA_W), spec(GLA_W), spec(GLA_W, OFF_GGATE // GLA_W), vec],
        out_specs=spec(mix_w),
        compiler_params=_cparams(("arbitrary", "arbitrary")),
        name="merge_heads",
    )(dn_f, dn_b, proj, dn_g.reshape(1, DN_D), att, gl_f, gl_b, proj, gla_g.reshape(1, GLA_DV))


def _split_points():
    pts, acc = [], 0
    for sz in IN_SIZES[:-1]:
        acc += sz
        pts.append(acc)
    return pts


def _relayout_w_in(w):
    d = w.shape[0]
    (dq, dk, dv, dgate, a_f, a_b, b_f, b_b, aq, ak, av, gq, gkk, gv, ggate, lr_f, lr_b) = jnp.split(
        w, _split_points(), axis=1)
    small = jnp.concatenate([a_f, a_b, b_f, b_b, lr_f, lr_b], axis=1)
    small = jnp.pad(small, ((0, 0), (0, LANES - small.shape[1])))
    cols = jnp.concatenate([dq, dk, dv, dgate, aq, ak, av, gq, gv, ggate, gkk,
                            jnp.zeros((d, PROJ_COLS - OFF_PAD), w.dtype)], axis=1)
    return cols.astype(BF16)[None, None], small.astype(BF16)[None, None]


def _rope_tables(s):
    rows = s // GRID_W
    row = jnp.repeat(jnp.arange(rows, dtype=jnp.int32), GRID_W).astype(F32)
    col = jnp.tile(jnp.arange(GRID_W, dtype=jnp.int32), rows).astype(F32)
    inv_freq = ROPE_THETA ** (-jnp.arange(0, ROPE_SUB, 2, dtype=F32) / ROPE_SUB)
    ang_r = row[:, None] * inv_freq[None, :]
    ang_c = col[:, None] * inv_freq[None, :]
    cos = jnp.concatenate([jnp.cos(ang_r), jnp.cos(ang_r), jnp.cos(ang_c), jnp.cos(ang_c)], axis=1)
    sin = jnp.concatenate([-jnp.sin(ang_r), jnp.sin(ang_r), -jnp.sin(ang_c), jnp.sin(ang_c)], axis=1)
    return cos, sin


def _dn_params(dn_a_log, dn_dt_bias):
    rows = jnp.stack([dn_a_log.reshape(-1), dn_dt_bias.reshape(-1)], axis=0)
    return jnp.pad(rows, ((0, 0), (SM_A, LANES - SM_A - 2 * DN_HEADS)))


def _gla_params(gla_up, gla_up_b):
    ups = []
    for d in range(2):
        lo = SM_LR + d * GLA_RANK
        ups.append(jnp.pad(gla_up[d], ((lo, LANES - lo - GLA_RANK), (0, 0))))
    return jnp.stack(ups, axis=0), gla_up_b[:, None, :]


def _mixer(x16, bsz, s, w_in, dn_conv, dn_a_log, dn_dt_bias, dn_norm_g, att_qn_g, att_kn_g,
           gla_up, gla_up_b, gla_norm_g, rope):
    t = bsz * s
    bm = _tile(t, 1024, 16)
    one_group = _block_table(jnp.zeros((t // bm,), jnp.int32))
    w_main, w_small = _relayout_w_in(w_in)
    proj = grouped_matmul(x16, w_main, 0, one_group, bm, 1024, BF16).reshape(bsz, s, PROJ_COLS)
    small = grouped_matmul(x16, w_small, 0, one_group, bm, LANES, F32).reshape(bsz, s, LANES)

    q, k, v = dn_prep(proj, dn_conv)
    u, wq, ktt, qk, egl = dn_wy(q, k, v, small, _dn_params(dn_a_log, dn_dt_bias))
    dn_f, dn_b = dn_rec(u, wq, ktt, qk, egl)

    cos, sin_signed = rope
    aq, ak, av, kn = att_prep(proj, cos, sin_signed, att_qn_g, att_kn_g)
    o_att = flash_attention(aq, ak, av, kn)

    gl_f, gl_b = gla_scan(proj, small, *_gla_params(gla_up, gla_up_b))

    return merge_heads(dn_f, dn_b, proj, dn_norm_g, o_att, gl_f, gl_b, gla_norm_g).reshape(t, -1)


def _moe(x32, x16, router_w, w_gate, w_up, w_down, layer, ln_g, ln_b, alpha):
    t, d = x32.shape
    rw = jnp.pad(router_w, ((0, 0), (0, LANES - N_EXPERTS)))
    logits = router_logits(x32, rw)[:, :N_EXPERTS]
    top_val, top_idx = lax.top_k(logits, TOP_K)
    gates = jax.nn.softmax(top_val, axis=-1)
    e_flat = top_idx.reshape(-1).astype(jnp.int32)
    tok_flat = jnp.repeat(jnp.arange(t, dtype=jnp.int32), TOP_K)
    onehot = (e_flat[:, None] == jnp.arange(N_EXPERTS, dtype=jnp.int32)[None, :]).astype(jnp.int32)
    csum = jnp.cumsum(onehot, axis=0)
    counts = csum[-1]
    rank = jnp.take_along_axis(csum, e_flat[:, None], axis=1)[:, 0] - 1
    padded = (counts + MOE_BLOCK - 1) // MOE_BLOCK * MOE_BLOCK
    pstart = jnp.cumsum(padded) - padded
    pend = pstart + padded
    dest = pstart[e_flat] + rank
    nb = -(-(TOP_K * t) // MOE_BLOCK) + N_EXPERTS
    cap = nb * MOE_BLOCK
    buf_tok = jnp.zeros((cap,), jnp.int32).at[dest].set(tok_flat)
    block_start = jnp.arange(nb, dtype=jnp.int32) * MOE_BLOCK
    block_exp = jnp.minimum(jnp.sum(block_start[:, None] >= pend[None, :], axis=-1), N_EXPERTS - 1)
    table = _block_table(block_exp, block_start < pend[-1])
    xb = x16[buf_tok]
    h = grouped_swiglu(xb, w_gate, w_up, layer, table, MOE_BLOCK, 1024)
    yb = grouped_matmul(h, w_down, layer, table, MOE_BLOCK, 512, BF16)
    dest_by_choice = dest.reshape(t, TOP_K).T.reshape(-1)
    return combine_ln(x32, yb[dest_by_choice], gates, ln_g, ln_b, alpha)


def _dense_ffn(x32, x16, w_gate, w_up, w_down, layer, ln_g, ln_b, alpha):
    t = x32.shape[0]
    bm = _tile(t, 1024, 16)
    h = grouped_swiglu(x16, w_gate, w_up, layer, _block_table(jnp.zeros((t // bm,), jnp.int32)), bm, 512)
    return matmul_res_ln(h, w_down[layer].astype(BF16), x32, ln_g, ln_b, alpha)


def kernel(x, w_in, dn_conv, dn_a_log, dn_dt_bias, dn_norm_g, att_qn_g, att_kn_g, gla_up, gla_up_b,
           gla_norm_g, w_out, ln1_g, ln1_b, ln2_g, ln2_b, ffn_w_gate, ffn_w_up, ffn_w_down, router_w,
           exp_w_gate, exp_w_up, exp_w_down):
    bsz, s, d = x.shape
    depth = w_in.shape[0]
    alpha = (2.0 * depth) ** 0.25
    t = bsz * s
    rope = _rope_tables(s)
    x32 = x.reshape(t, d)
    x16 = x32.astype(BF16)
    for layer in range(depth):
        mix = _mixer(x16, bsz, s, w_in[layer], dn_conv[layer], dn_a_log[layer], dn_dt_bias[layer],
                     dn_norm_g[layer], att_qn_g[layer], att_kn_g[layer], gla_up[layer],
                     gla_up_b[layer], gla_norm_g[layer], rope)
        x32, x16 = matmul_res_ln(mix, w_out[layer].astype(BF16), x32, ln1_g[layer], ln1_b[layer], alpha)
        j = layer // 2
        if layer % 2 == 0:
            x32, x16 = _dense_ffn(x32, x16, ffn_w_gate, ffn_w_up, ffn_w_down, j,
                                  ln2_g[layer], ln2_b[layer], alpha)
        else:
            x32, x16 = _moe(x32, x16, router_w[j], exp_w_gate, exp_w_up, exp_w_down, j,
                            ln2_g[layer], ln2_b[layer], alpha)
    return x32.reshape(bsz, s, d)
```

```python
import functools
import math

import jax
import jax.numpy as jnp
from jax import lax
from jax.experimental import pallas as pl
from jax.experimental.pallas import tpu as pltpu

F32 = jnp.float32
BF16 = jnp.bfloat16
HIGHEST = lax.Precision.HIGHEST

DN_HEADS, DN_D = 6, 128
ATT_HEADS, ATT_KV_HEADS, ATT_DH = 6, 2, 128
ATT_GROUP = ATT_HEADS // ATT_KV_HEADS
ROPE_SUB, ROPE_THETA, GRID_W = 64, 10000.0, 64
GLA_HEADS, GLA_DK, GLA_DV, GLA_RANK = 4, 64, 128, 16
GLA_NORMALIZER = 16.0
CHUNK = 64
SUB = 8
MASKED_LOG = -1e30
WY_ROWS = 2 * CHUNK
EGL_ROWS = 8
CONV_W = 5
N_EXPERTS, TOP_K, MOE_BLOCK = 8, 2, 256
EPS = 1e-6
LOG2E = 1.4426950408889634
LANES = 128
HALO = 16

DN_W = DN_HEADS * DN_D
ATT_W = ATT_HEADS * ATT_DH
ATT_KV_W = ATT_KV_HEADS * ATT_DH
GLA_QK = GLA_HEADS * GLA_DK
GLA_W = GLA_HEADS * GLA_DV
IN_SIZES = (DN_W, DN_W, DN_W, DN_W, DN_HEADS, DN_HEADS, DN_HEADS, DN_HEADS,
            ATT_W, ATT_KV_W, ATT_KV_W, GLA_QK, GLA_QK, GLA_W, GLA_W, GLA_RANK, GLA_RANK)
OFF_DQ, OFF_DGATE, OFF_AQ, OFF_AK, OFF_AV = 0, 2304, 3072, 3840, 4096
OFF_GQ, OFF_GV, OFF_GGATE, OFF_GK, OFF_PAD = 4352, 4608, 5120, 5632, 5888
PROJ_COLS = 6144
SM_A, SM_B, SM_LR = 0, 2 * DN_HEADS, 4 * DN_HEADS

VMEM_LIMIT = 56 * 1024 * 1024


def _cparams(sem, vmem=VMEM_LIMIT):
    return pltpu.CompilerParams(dimension_semantics=sem, vmem_limit_bytes=vmem)


def _tile(n, pref, quantum=LANES):
    if n <= pref:
        return n
    t = pref - pref % quantum
    while n % t:
        t -= quantum
    return t


def _bdot(a, b):
    return jnp.dot(a.astype(BF16), b.astype(BF16), preferred_element_type=F32)


def _bdot_nt(a, b):
    return lax.dot_general(a.astype(BF16), b.astype(BF16), (((1,), (1,)), ((), ())),
                           preferred_element_type=F32)


def _hdot(a, b):
    return jnp.dot(a, b, preferred_element_type=F32, precision=HIGHEST)


TB_GROUP, TB_USED, TB_FIRST, TB_SEG, TB_NEXT, TB_LAST = range(6)


def _block_table(group, used=None):
    group = group.astype(jnp.int32)
    used = jnp.ones_like(group) if used is None else used.astype(jnp.int32)
    first = jnp.concatenate([jnp.ones((1,), jnp.int32), (group[1:] != group[:-1]).astype(jnp.int32)])
    seg = jnp.cumsum(first) - 1
    nseg = seg[-1] + 1
    run_group = jnp.zeros_like(group).at[seg].set(group)
    nxt = run_group[jnp.where(seg + 1 < nseg, seg + 1, 0)]
    last = (seg == nseg - 1).astype(jnp.int32)
    return jnp.stack([group, used, first, seg, nxt, last], axis=0)


def _if_used(be_ref, o_ref, compute):
    used = be_ref[TB_USED, pl.program_id(1)] != 0

    @pl.when(used)
    def _():
        compute()

    @pl.when(jnp.logical_not(used))
    def _():
        o_ref[...] = jnp.zeros_like(o_ref)


def _stream_weights(be_ref, w_hbm, wbuf, w16, sem, layer, tn):
    j, i = pl.program_id(0), pl.program_id(1)
    nj, nb = pl.num_programs(0), pl.num_programs(1)
    nseg = be_ref[TB_SEG, nb - 1] + 1
    slot = lax.rem(j * nseg + be_ref[TB_SEG, i], 2)

    def copies(group, col, to_slot):
        start = pl.multiple_of(col * tn, tn)
        out = []
        for m, (w, buf) in enumerate(zip(w_hbm, wbuf)):
            src = w.at[layer, :, pl.ds(start, tn)] if len(w.shape) == 3 else w.at[layer, group, :, pl.ds(start, tn)]
            out.append(pltpu.make_async_copy(src, buf.at[to_slot], sem.at[m, to_slot]))
        return out

    @pl.when((j == 0) & (i == 0))
    def _():
        for c in copies(be_ref[TB_GROUP, i], j, slot):
            c.start()

    @pl.when(be_ref[TB_FIRST, i] != 0)
    def _():
        for c in copies(be_ref[TB_GROUP, i], j, slot):
            c.wait()
        for m in range(len(w_hbm)):
            w16[m][...] = wbuf[m][slot].astype(BF16)
        last = be_ref[TB_LAST, i] != 0

        @pl.when(jnp.logical_not(last) | (j + 1 < nj))
        def _():
            for c in copies(be_ref[TB_NEXT, i], jnp.where(last, j + 1, j), 1 - slot):
                c.start()


def _gmm_body(be_ref, x_ref, w_ref, o_ref, *scratch, layer, tn):
    if scratch:
        wbuf, w16, sem = scratch
        _stream_weights(be_ref, [w_ref], [wbuf], [w16], sem, layer, tn)
        w_use = w16
    else:
        w_use = w_ref

    def compute():
        o_ref[...] = jnp.dot(x_ref[...], w_use[...], preferred_element_type=F32).astype(o_ref.dtype)

    _if_used(be_ref, o_ref, compute)


def _stream_scratch(k, tn, n_mats):
    return ([pltpu.VMEM((2, k, tn), F32) for _ in range(n_mats)] + [pltpu.VMEM((k, tn), BF16) for _ in range(n_mats)]
            + [pltpu.SemaphoreType.DMA((n_mats, 2))])


def grouped_matmul(x, w, layer, block_table, bm, tn, out_dtype):
    m, k = x.shape
    n = w.shape[-1]
    tn = _tile(n, tn)
    if w.dtype == BF16:
        w_spec = pl.BlockSpec((None, None, k, tn), lambda j, i, be: (layer, 0, 0, j))
        scratch = []
    else:
        w_spec = pl.BlockSpec(memory_space=pl.ANY)
        scratch = _stream_scratch(k, tn, 1)
    return pl.pallas_call(
        functools.partial(_gmm_body, layer=layer, tn=tn),
        out_shape=jax.ShapeDtypeStruct((m, n), out_dtype),
        grid_spec=pltpu.PrefetchScalarGridSpec(
            num_scalar_prefetch=1, grid=(n // tn, m // bm),
            in_specs=[pl.BlockSpec((bm, k), lambda j, i, be: (i, 0)), w_spec],
            out_specs=pl.BlockSpec((bm, tn), lambda j, i, be: (i, j)),
            scratch_shapes=scratch),
        compiler_params=_cparams(("arbitrary", "arbitrary")),
        name="grouped_matmul",
    )(block_table, x, w)


def _gswiglu_body(be_ref, x_ref, wg_ref, wu_ref, o_ref, wg_buf, wu_buf, wg16, wu16, sem, *, layer, tn):
    _stream_weights(be_ref, [wg_ref, wu_ref], [wg_buf, wu_buf], [wg16, wu16], sem, layer, tn)

    def compute():
        x = x_ref[...]
        g = jnp.dot(x, wg16[...], preferred_element_type=F32)
        u = jnp.dot(x, wu16[...], preferred_element_type=F32)
        o_ref[...] = (g * jax.nn.sigmoid(g) * u).astype(o_ref.dtype)

    _if_used(be_ref, o_ref, compute)


def grouped_swiglu(x, wg, wu, layer, block_table, bm, tn):
    m, k = x.shape
    n = wg.shape[-1]
    tn = _tile(n, tn)
    hbm = pl.BlockSpec(memory_space=pl.ANY)
    return pl.pallas_call(
        functools.partial(_gswiglu_body, layer=layer, tn=tn),
        out_shape=jax.ShapeDtypeStruct((m, n), BF16),
        grid_spec=pltpu.PrefetchScalarGridSpec(
            num_scalar_prefetch=1, grid=(n // tn, m // bm),
            in_specs=[pl.BlockSpec((bm, k), lambda j, i, be: (i, 0)), hbm, hbm],
            out_specs=pl.BlockSpec((bm, tn), lambda j, i, be: (i, j)),
            scratch_shapes=_stream_scratch(k, tn, 2)),
        compiler_params=_cparams(("arbitrary", "arbitrary")),
        name="grouped_swiglu",
    )(block_table, x, wg, wu)


def _res_ln(res, y, g, b, alpha):
    z = alpha * res + y
    mu = jnp.mean(z, axis=-1, keepdims=True)
    zc = z - mu
    var = jnp.mean(zc * zc, axis=-1, keepdims=True)
    return zc * lax.rsqrt(var + EPS) * g + b


def _mm_res_ln_body(x_ref, w_ref, res_ref, g_ref, b_ref, o32_ref, o16_ref, *acc, nk, alpha):
    kk = pl.program_id(1)
    part = jnp.dot(x_ref[...], w_ref[...], preferred_element_type=F32)

    def finish(y):
        out = _res_ln(res_ref[...], y, g_ref[...], b_ref[...], alpha)
        o32_ref[...] = out
        o16_ref[...] = out.astype(BF16)

    if nk == 1:
        finish(part)
        return
    acc_ref = acc[0]

    @pl.when(kk == 0)
    def _():
        acc_ref[...] = part

    @pl.when((kk > 0) & (kk < nk - 1))
    def _():
        acc_ref[...] += part

    @pl.when(kk == nk - 1)
    def _():
        finish(acc_ref[...] + part)


def matmul_res_ln(x, w, res, g, b, alpha, tm=512, tk=2048):
    m, k = x.shape
    n = w.shape[-1]
    tm = _tile(m, tm, 8)
    tk = _tile(k, tk)
    nk = k // tk
    return pl.pallas_call(
        functools.partial(_mm_res_ln_body, nk=nk, alpha=alpha),
        out_shape=(jax.ShapeDtypeStruct((m, n), F32), jax.ShapeDtypeStruct((m, n), BF16)),
        grid=(m // tm, nk),
        in_specs=[pl.BlockSpec((tm, tk), lambda i, kk: (i, kk)),
                  pl.BlockSpec((tk, n), lambda i, kk: (kk, 0)),
                  pl.BlockSpec((tm, n), lambda i, kk: (i, 0)),
                  pl.BlockSpec((1, n), lambda i, kk: (0, 0)),
                  pl.BlockSpec((1, n), lambda i, kk: (0, 0))],
        out_specs=(pl.BlockSpec((tm, n), lambda i, kk: (i, 0)),
                   pl.BlockSpec((tm, n), lambda i, kk: (i, 0))),
        scratch_shapes=[pltpu.VMEM((tm, n), F32)] if nk > 1 else [],
        compiler_params=_cparams(("arbitrary", "arbitrary")),
        name="matmul_res_ln",
    )(x, w, res, g.reshape(1, n), b.reshape(1, n))


def _combine_ln_body(res_ref, *refs, alpha):
    y_refs, (gate_ref, g_ref, b_ref, o32_ref, o16_ref) = refs[:TOP_K], refs[TOP_K:]
    gate = gate_ref[...]
    y = y_refs[0][...].astype(F32) * gate[:, 0:1]
    for kk in range(1, TOP_K):
        y = y + y_refs[kk][...].astype(F32) * gate[:, kk:kk + 1]
    out = _res_ln(res_ref[...], y, g_ref[...], b_ref[...], alpha)
    o32_ref[...] = out
    o16_ref[...] = out.astype(BF16)


def combine_ln(res, y, gates, g, b, alpha, tm=512):
    m, n = res.shape
    tm = _tile(m, tm, 16)
    nblk = m // tm
    row = pl.BlockSpec((tm, n), lambda i: (i, 0))
    vec = pl.BlockSpec((1, n), lambda i: (0, 0))

    def choice(kk):
        return pl.BlockSpec((tm, n), lambda i: (kk * nblk + i, 0))

    return pl.pallas_call(
        functools.partial(_combine_ln_body, alpha=alpha),
        out_shape=(jax.ShapeDtypeStruct((m, n), F32), jax.ShapeDtypeStruct((m, n), BF16)),
        grid=(nblk,),
        in_specs=[row] + [choice(kk) for kk in range(TOP_K)] + [pl.BlockSpec((tm, TOP_K), lambda i: (i, 0)), vec, vec],
        out_specs=(row, row),
        compiler_params=_cparams(("arbitrary",)),
        name="combine_ln",
    )(res, *([y] * TOP_K), gates, g.reshape(1, n), b.reshape(1, n))


def _router_body(x_ref, w_ref, o_ref):
    o_ref[...] = _hdot(x_ref[...], w_ref[...])


def router_logits(x, w_pad, tm=512):
    m, k = x.shape
    n = w_pad.shape[-1]
    tm = _tile(m, tm, 8)
    return pl.pallas_call(
        _router_body,
        out_shape=jax.ShapeDtypeStruct((m, n), F32),
        grid=(m // tm,),
        in_specs=[pl.BlockSpec((tm, k), lambda i: (i, 0)), pl.BlockSpec((k, n), lambda i: (0, 0))],
        out_specs=pl.BlockSpec((tm, n), lambda i: (i, 0)),
        compiler_params=_cparams(("arbitrary",)),
        name="router_logits",
    )(x, w_pad)


def _dn_prep_body(cur_ref, prev_ref, next_ref, w_ref, q_ref, k_ref, v_ref, buf_ref, *, ts, nblk, sub):
    i = pl.program_id(1)
    buf_ref[0:HALO, :] = jnp.where(i > 0, prev_ref[0], jnp.zeros_like(prev_ref[0]))
    buf_ref[HALO:HALO + ts, :] = cur_ref[0]
    buf_ref[HALO + ts:2 * HALO + ts, :] = jnp.where(i < nblk - 1, next_ref[0], jnp.zeros_like(next_ref[0]))
    win = sub + 2 * HALO
    ri = lax.broadcasted_iota(jnp.int32, (sub, win), 0)
    ci = lax.broadcasted_iota(jnp.int32, (sub, win), 1)
    centre = CONV_W // 2
    shift = {j: (ci == ri + HALO + j - centre).astype(BF16) for j in range(CONV_W) if j != centre}
    for r in range(ts // sub):
        window = buf_ref[r * sub:r * sub + win, :]
        acc = window[HALO:HALO + sub, :].astype(F32) * w_ref[centre:centre + 1, :]
        for j, sel in shift.items():
            acc = acc + jnp.dot(sel, window, preferred_element_type=F32) * w_ref[j:j + 1, :]
        y = acc * jax.nn.sigmoid(acc)
        rows = slice(r * sub, (r + 1) * sub)
        for h in range(DN_HEADS):
            for off, ref in ((0, q_ref), (DN_W, k_ref)):
                t = y[:, off + h * DN_D:off + (h + 1) * DN_D]
                ref[0, rows, h * DN_D:(h + 1) * DN_D] = t * lax.rsqrt(jnp.sum(t * t, axis=-1, keepdims=True) + EPS)
        v_ref[0, rows, :] = y[:, 2 * DN_W:3 * DN_W]


def dn_prep(proj, conv_w, ts=512):
    b, s, _ = proj.shape
    ts = _tile(s, ts, HALO)
    sub = _tile(ts, LANES, HALO)
    nblk = s // ts
    c = 3 * DN_W
    hb = ts // HALO
    out = jax.ShapeDtypeStruct((b, s, DN_W), F32)
    ospec = pl.BlockSpec((1, ts, DN_W), lambda bb, i: (bb, i, 0))
    return pl.pallas_call(
        functools.partial(_dn_prep_body, ts=ts, nblk=nblk, sub=sub),
        out_shape=(out, out, out),
        grid=(b, nblk),
        in_specs=[pl.BlockSpec((1, ts, c), lambda bb, i: (bb, i, 0)),
                  pl.BlockSpec((1, HALO, c), lambda bb, i: (bb, jnp.maximum(i * hb - 1, 0), 0)),
                  pl.BlockSpec((1, HALO, c), lambda bb, i: (bb, jnp.minimum((i + 1) * hb, nblk * hb - 1), 0)),
                  pl.BlockSpec((CONV_W, c), lambda bb, i: (0, 0))],
        out_specs=(ospec, ospec, ospec),
        scratch_shapes=[pltpu.VMEM((ts + 2 * HALO, c), BF16)],
        compiler_params=_cparams(("arbitrary", "arbitrary")),
        name="dn_prep",
    )(proj, proj, proj, conv_w)


def _dn_wy_body(q_ref, k_ref, v_ref, sm_ref, par_ref, u_ref, wq_ref, ktt_ref, qk_ref, egl_ref, *, tiles):
    def tile(t, carry):
        _dn_wy_tile(t, q_ref, k_ref, v_ref, sm_ref, par_ref, u_ref, wq_ref, ktt_ref, qk_ref, egl_ref)
        return carry

    lax.fori_loop(0, tiles, tile, 0)


def _dn_wy_tile(t, q_ref, k_ref, v_ref, sm_ref, par_ref, u_ref, wq_ref, ktt_ref, qk_ref, egl_ref):
    r = WY_ROWS
    nck = r // CHUNK
    rows_t = pl.ds(pl.multiple_of(t * r, r), r)
    sm = sm_ref[0, rows_t, :]
    par = par_ref[...]
    g_all = -jnp.exp(par[0:1, :]) * jax.nn.softplus(sm + par[1:2, :])
    beta_all = jax.nn.sigmoid(sm)
    ri = lax.broadcasted_iota(jnp.int32, (r, r), 0)
    ci = lax.broadcasted_iota(jnp.int32, (r, r), 1)
    same = (ri // CHUNK) == (ci // CHUNK)
    eye = (ri == ci).astype(F32)
    masks = []
    for d in range(2):
        delta = ri - ci if d == 0 else ci - ri
        masks.append((same & (delta >= 0), same & (delta > 0)))
    scale = DN_D ** -0.5
    gtot = _hdot(same.astype(F32), g_all)
    gcs = []
    for d in range(2):
        gc = _hdot(masks[d][0].astype(F32), g_all)
        gcs.append((gc, gc.T))
    chains = [(h, d) for h in range(DN_HEADS) for d in range(2)]
    qk_cols = ([], [])
    nm, tinv, rhs = {}, {}, {}
    for h in range(DN_HEADS):
        sl = slice(h * DN_D, (h + 1) * DN_D)
        q = q_ref[0, rows_t, sl] * scale
        k = k_ref[0, rows_t, sl]
        v = v_ref[0, rows_t, sl]
        kq = _bdot_nt(jnp.concatenate([k, q], axis=0), k)
        kk, qk = kq[:r], kq[r:]
        for d in range(2):
            lane = SM_A + d * DN_HEADS + h
            incl, strict = masks[d]
            gc, gct = gcs[d]
            gcol = jnp.broadcast_to(gc[:, lane:lane + 1], (r, r))
            gt = jnp.broadcast_to(gtot[:, lane:lane + 1], (r, r))
            bcol = jnp.broadcast_to(beta_all[:, SM_B + d * DN_HEADS + h:SM_B + d * DN_HEADS + h + 1], (r, r))
            grow = gct[lane:lane + 1, :]
            decay = jnp.where(incl, jnp.exp(jnp.where(incl, gcol - grow, 0.0)), 0.0)
            nm[h, d] = jnp.where(strict, -(bcol * kk) * decay, 0.0)
            tinv[h, d] = eye + nm[h, d]
            eg = jnp.exp(gcol)
            rhs[h, d] = jnp.concatenate([v * bcol, k * (bcol * eg)], axis=1).astype(BF16)
            qd16 = (q * eg).astype(BF16)
            ktt = _bdot_nt(eye, (k * jnp.exp(gt - gcol)).astype(BF16)).astype(BF16)
            for c in range(nck):
                rows = slice(c * CHUNK, (c + 1) * CHUNK)
                wq_ref[d, 0, t * nck + c, CHUNK:2 * CHUNK, sl] = qd16[rows]
                ktt_ref[d, 0, t * nck + c, :, h * CHUNK:(h + 1) * CHUNK] = ktt[:, rows]
            qkd = qk * decay
            qk_cols[d].append(qkd[:, :CHUNK] + qkd[:, CHUNK:])
            egt = jnp.exp(gt)
            for c in range(nck):
                egl_ref[d, 0, t * nck + c, :, sl] = egt[c * CHUNK:c * CHUNK + EGL_ROWS]
    for d in range(2):
        qk_ref[d, 0, rows_t, :] = jnp.concatenate(qk_cols[d], axis=1).astype(BF16)
    for _ in range(int(math.log2(CHUNK)) - 1):
        for hd in chains:
            nm[hd] = _bdot(nm[hd], nm[hd])
        for hd in chains:
            tinv[hd] = tinv[hd] + _bdot(tinv[hd], nm[hd])
    for h, d in chains:
        sl = slice(h * DN_D, (h + 1) * DN_D)
        uw = _bdot(tinv[h, d], rhs[h, d])
        u_ref[d, 0, rows_t, sl] = uw[:, :DN_D].astype(BF16)
        w16 = uw[:, DN_D:].astype(BF16)
        for c in range(nck):
            wq_ref[d, 0, t * nck + c, 0:CHUNK, sl] = w16[c * CHUNK:(c + 1) * CHUNK]


def dn_wy(q, k, v, small, par):
    b, s, w = q.shape
    nchunk = s // CHUNK
    tiles = max(c for c in (4, 2, 1) if s % (c * WY_ROWS) == 0)
    r = tiles * WY_ROWS
    nck = r // CHUNK
    seq = pl.BlockSpec((1, r, w), lambda bb, i: (bb, i, 0))
    dseq = pl.BlockSpec((2, 1, r, w), lambda bb, i: (0, bb, i, 0))
    return pl.pallas_call(
        functools.partial(_dn_wy_body, tiles=tiles),
        out_shape=(jax.ShapeDtypeStruct((2, b, s, w), BF16),
                   jax.ShapeDtypeStruct((2, b, nchunk, 2 * CHUNK, w), BF16),
                   jax.ShapeDtypeStruct((2, b, nchunk, DN_D, DN_HEADS * CHUNK), BF16),
                   jax.ShapeDtypeStruct((2, b, s, DN_HEADS * CHUNK), BF16),
                   jax.ShapeDtypeStruct((2, b, nchunk, EGL_ROWS, w), F32)),
        grid=(b, s // r),
        in_specs=[seq, seq, seq,
                  pl.BlockSpec((1, r, LANES), lambda bb, i: (bb, i, 0)),
                  pl.BlockSpec((2, LANES), lambda bb, i: (0, 0))],
        out_specs=(dseq,
                   pl.BlockSpec((2, 1, nck, 2 * CHUNK, w), lambda bb, i: (0, bb, i, 0, 0)),
                   pl.BlockSpec((2, 1, nck, DN_D, DN_HEADS * CHUNK), lambda bb, i: (0, bb, i, 0, 0)),
                   pl.BlockSpec((2, 1, r, DN_HEADS * CHUNK), lambda bb, i: (0, bb, i, 0)),
                   pl.BlockSpec((2, 1, nck, EGL_ROWS, w), lambda bb, i: (0, bb, i, 0, 0))),
        compiler_params=_cparams(("arbitrary", "arbitrary")),
        name="dn_wy",
    )(q, k, v, small, par)


def _dn_rec_body(*refs, nc):
    ins = (refs[0:5], refs[5:10])
    outs = refs[10:12]
    state_ref = refs[12]
    n = pl.program_id(1)

    @pl.when(n == 0)
    def _():
        state_ref[...] = jnp.zeros_like(state_ref)

    chains = [(d, h) for d in range(2) for h in range(DN_HEADS)]
    state = {(d, h): state_ref[d, h] for d, h in chains}
    for step in range(nc):
        chunk = (step, nc - 1 - step)
        wq, v_new = {}, {}
        for d, h in chains:
            sl = slice(h * DN_D, (h + 1) * DN_D)
            wq[d, h] = jnp.dot(ins[d][1][0, 0, chunk[d], :, sl], state[d, h].astype(BF16),
                               preferred_element_type=F32)
        for d, h in chains:
            sl = slice(h * DN_D, (h + 1) * DN_D)
            rows = slice(chunk[d] * CHUNK, (chunk[d] + 1) * CHUNK)
            v_new[d, h] = (ins[d][0][0, 0, rows, sl].astype(F32) - wq[d, h][:CHUNK]).astype(BF16)
        for d, h in chains:
            sl = slice(h * DN_D, (h + 1) * DN_D)
            rows = slice(chunk[d] * CHUNK, (chunk[d] + 1) * CHUNK)
            qk = ins[d][3][0, 0, rows, h * CHUNK:(h + 1) * CHUNK]
            outs[d][0, rows, sl] = (wq[d, h][CHUNK:]
                                    + jnp.dot(qk, v_new[d, h], preferred_element_type=F32)).astype(outs[d].dtype)
        for d, h in chains:
            sl = slice(h * DN_D, (h + 1) * DN_D)
            ktt = ins[d][2][0, 0, chunk[d], :, h * CHUNK:(h + 1) * CHUNK]
            state[d, h] = (state[d, h] * ins[d][4][0, 0, chunk[d], 0:1, sl]
                           + jnp.dot(ktt, v_new[d, h], preferred_element_type=F32))
    for d, h in chains:
        state_ref[d, h] = state[d, h]


def dn_rec(u, wq, ktt, qk, egl):
    _, b, s, w = u.shape
    nchunk = s // CHUNK
    nc = max(c for c in (4, 2, 1) if nchunk % c == 0)
    nblk = nchunk // nc
    rows = nc * CHUNK

    def specs(d):
        def blk(n):
            return n if d == 0 else nblk - 1 - n
        return [pl.BlockSpec((1, 1, rows, w), lambda bb, n: (d, bb, blk(n), 0)),
                pl.BlockSpec((1, 1, nc, 2 * CHUNK, w), lambda bb, n: (d, bb, blk(n), 0, 0)),
                pl.BlockSpec((1, 1, nc, DN_D, DN_HEADS * CHUNK), lambda bb, n: (d, bb, blk(n), 0, 0)),
                pl.BlockSpec((1, 1, rows, DN_HEADS * CHUNK), lambda bb, n: (d, bb, blk(n), 0)),
                pl.BlockSpec((1, 1, nc, EGL_ROWS, w), lambda bb, n: (d, bb, blk(n), 0, 0))]

    out = jax.ShapeDtypeStruct((b, s, w), BF16)
    return pl.pallas_call(
        functools.partial(_dn_rec_body, nc=nc),
        out_shape=(out, out),
        grid=(b, nblk),
        in_specs=specs(0) + specs(1),
        out_specs=(pl.BlockSpec((1, rows, w), lambda bb, n: (bb, n, 0)),
                   pl.BlockSpec((1, rows, w), lambda bb, n: (bb, nblk - 1 - n, 0))),
        scratch_shapes=[pltpu.VMEM((2, DN_HEADS, DN_D, DN_D), F32)],
        compiler_params=_cparams(("arbitrary", "arbitrary")),
        name="dn_rec",
    )(u, wq, ktt, qk, egl, u, wq, ktt, qk, egl)


def _rope_head(x, g, cos, sin_signed, first_half):
    x = x.astype(F32)
    xf = x * lax.rsqrt(jnp.mean(x * x, axis=-1, keepdims=True) + EPS) * g
    partner = jnp.where(first_half, pltpu.roll(xf, LANES - ROPE_SUB // 2, axis=1),
                        pltpu.roll(xf, ROPE_SUB // 2, axis=1))
    return xf * cos + partner * sin_signed


def _att_prep_body(q_ref, k_ref, v_ref, cos_ref, sin_ref, qg_ref, kg_ref, qo_ref, ko_ref, vo_ref, kn_ref):
    cos = cos_ref[...]
    sin_signed = sin_ref[...]
    lane = lax.broadcasted_iota(jnp.int32, cos.shape, 1)
    first_half = (lane % ROPE_SUB) < ROPE_SUB // 2
    scale = ATT_DH ** -0.5 * LOG2E
    for h in range(ATT_HEADS):
        sl = slice(h * ATT_DH, (h + 1) * ATT_DH)
        qo_ref[0, :, sl] = (_rope_head(q_ref[0, :, sl], qg_ref[...], cos, sin_signed, first_half)
                            * scale).astype(BF16)
    for h in range(ATT_KV_HEADS):
        sl = slice(h * ATT_DH, (h + 1) * ATT_DH)
        k16 = _rope_head(k_ref[0, :, sl], kg_ref[...], cos, sin_signed, first_half).astype(BF16)
        ko_ref[0, :, sl] = k16
        kf = k16.astype(F32)
        n2 = jnp.max(jnp.sum(kf * kf, axis=-1, keepdims=True), axis=0, keepdims=True)
        kn_ref[0, 0, :, sl] = jnp.broadcast_to(n2, (1, ATT_DH))
    for h in range(ATT_KV_HEADS):
        vo_ref[0, :, 2 * h * ATT_DH:(2 * h + 1) * ATT_DH] = v_ref[0, :, h * ATT_DH:(h + 1) * ATT_DH]
        vo_ref[0, :, (2 * h + 1) * ATT_DH:(2 * h + 2) * ATT_DH] = jnp.ones((v_ref.shape[1], ATT_DH), BF16)


def att_prep(proj, cos, sin_signed, qg, kg, ts=512):
    b, s, _ = proj.shape
    ts = _tile(s, ts, 16)
    tab = pl.BlockSpec((ts, ATT_DH), lambda bb, i: (i, 0))
    vec = pl.BlockSpec((1, ATT_DH), lambda bb, i: (0, 0))
    return pl.pallas_call(
        _att_prep_body,
        out_shape=(jax.ShapeDtypeStruct((b, s, ATT_W), BF16),
                   jax.ShapeDtypeStruct((b, s, ATT_KV_W), BF16),
                   jax.ShapeDtypeStruct((b, s, 2 * ATT_KV_W), BF16),
                   jax.ShapeDtypeStruct((b, s // ts, 1, ATT_KV_W), F32)),
        grid=(b, s // ts),
        in_specs=[pl.BlockSpec((1, ts, ATT_W), lambda bb, i: (bb, i, OFF_AQ // ATT_W)),
                  pl.BlockSpec((1, ts, ATT_KV_W), lambda bb, i: (bb, i, OFF_AK // ATT_KV_W)),
                  pl.BlockSpec((1, ts, ATT_KV_W), lambda bb, i: (bb, i, OFF_AV // ATT_KV_W)),
                  tab, tab, vec, vec],
        out_specs=(pl.BlockSpec((1, ts, ATT_W), lambda bb, i: (bb, i, 0)),
                   pl.BlockSpec((1, ts, ATT_KV_W), lambda bb, i: (bb, i, 0)),
                   pl.BlockSpec((1, ts, 2 * ATT_KV_W), lambda bb, i: (bb, i, 0)),
                   pl.BlockSpec((1, 1, 1, ATT_KV_W), lambda bb, i: (bb, i, 0, 0))),
        compiler_params=_cparams(("arbitrary", "arbitrary")),
        name="att_prep",
    )(proj, proj, proj, cos, sin_signed, qg.reshape(1, ATT_DH), kg.reshape(1, ATT_DH))


MAX_FIXED_SHIFT = 60.0


def _flash_body(q_ref, k_ref, v_ref, kn_ref, o_ref, *, tk, nkv):
    tq = q_ref.shape[1]
    heads = range(ATT_GROUP)
    qs = [q_ref[0, :, h * ATT_DH:(h + 1) * ATT_DH] for h in heads]
    unroll = max(u for u in (4, 2, 1) if nkv % u == 0)

    def tiles(t):
        start = pl.multiple_of(t * tk, tk)
        return k_ref[0, pl.ds(start, tk), :], v_ref[0, pl.ds(start, tk), :]

    def scores(h, ks):
        return lax.dot_general(qs[h], ks, (((1,), (1,)), ((), ())), preferred_element_type=F32)

    def finish(accs):
        for h in heads:
            o_ref[0, :, h * ATT_DH:(h + 1) * ATT_DH] = (accs[h][:, :ATT_DH] / accs[h][:, ATT_DH:]).astype(o_ref.dtype)

    kmax = jnp.sqrt(jnp.max(kn_ref[0, :, 0, :], axis=0, keepdims=True))[:, 0:1]
    shifts = []
    for h in heads:
        qf = qs[h].astype(F32)
        shifts.append(jnp.sqrt(jnp.sum(qf * qf, axis=-1, keepdims=True)) * (kmax * 1.001))
    worst = shifts[0]
    for h in heads[1:]:
        worst = jnp.maximum(worst, shifts[h])
    fixed_ok = jnp.max(worst) <= MAX_FIXED_SHIFT

    @pl.when(fixed_ok)
    def _():
        def step(t, accs):
            ks, vs = tiles(t)
            return tuple(accs[h] + jnp.dot(jnp.exp2(scores(h, ks) - shifts[h]).astype(BF16), vs,
                                           preferred_element_type=F32) for h in heads)

        init = tuple(jnp.zeros((tq, 2 * ATT_DH), F32) for _ in heads)
        finish(lax.fori_loop(0, nkv, step, init, unroll=unroll))

    @pl.when(jnp.logical_not(fixed_ok))
    def _():
        def step(t, carry):
            ks, vs = tiles(t)
            new = []
            for h in heads:
                m, acc = carry[h]
                sc = scores(h, ks)
                m_new = jnp.maximum(m, jnp.max(sc, axis=-1, keepdims=True))
                p = jnp.exp2(sc - m_new)
                acc = jnp.exp2(m - m_new) * acc + jnp.dot(p.astype(BF16), vs, preferred_element_type=F32)
                new.append((m_new, acc))
            return tuple(new)

        init = tuple((jnp.full((tq, 1), -1e30, F32), jnp.zeros((tq, 2 * ATT_DH), F32)) for _ in heads)
        res = lax.fori_loop(0, nkv, step, init, unroll=unroll)
        finish([res[h][1] for h in heads])


def flash_attention(q, k, v, kn, tq=512, tk=512):
    b, s, _ = q.shape
    tq = _tile(s, tq, 16)
    tk = _tile(s, tk, 16)
    gw = ATT_GROUP * ATT_DH
    return pl.pallas_call(
        functools.partial(_flash_body, tk=tk, nkv=s // tk),
        out_shape=jax.ShapeDtypeStruct((b, s, ATT_W), BF16),
        grid=(b, ATT_KV_HEADS, s // tq),
        in_specs=[pl.BlockSpec((1, tq, gw), lambda bb, g, i: (bb, i, g)),
                  pl.BlockSpec((1, s, ATT_DH), lambda bb, g, i: (bb, 0, g)),
                  pl.BlockSpec((1, s, 2 * ATT_DH), lambda bb, g, i: (bb, 0, g)),
                  pl.BlockSpec((1, kn.shape[1], 1, ATT_DH), lambda bb, g, i: (bb, 0, 0, g))],
        out_specs=pl.BlockSpec((1, tq, gw), lambda bb, g, i: (bb, i, g)),
        compiler_params=_cparams(("arbitrary", "arbitrary", "arbitrary")),
        name="flash_attention",
    )(q, k, v, kn)


def _gla_chunk(qs, k, v, gk, state_ref, d, reverse):
    c = CHUNK
    nsub = c // SUB
    ri = lax.broadcasted_iota(jnp.int32, (c, c), 0)
    ci = lax.broadcasted_iota(jnp.int32, (c, c), 1)
    incl = (ci >= ri) if reverse else (ri >= ci)
    gc = _hdot(incl.astype(F32), gk)
    last = 0 if reverse else c - 1
    gl = gc[last:last + 1, :]
    qd = qs * jnp.exp2(gc)
    kt = k * jnp.exp2(gl - gc)
    egl = jnp.exp2(gl)
    row = lax.broadcasted_iota(jnp.int32, (c, GLA_QK), 0)
    yield

    head_of_lane = lax.broadcasted_iota(jnp.int32, (SUB, GLA_QK), 1) // GLA_DK
    a_off = [[] for _ in range(GLA_HEADS)]
    for i in range(nsub):
        yield
        if i == (nsub - 1 if reverse else 0):
            for h in range(GLA_HEADS):
                a_off[h].append(jnp.zeros((SUB, c), F32))
            continue
        first = (i + 1) * SUB - 1 if reverse else i * SUB
        rs = slice(i * SUB, (i + 1) * SUB)
        ref = gc[first:first + 1, :]
        qi = qs[rs, :] * jnp.exp2(gc[rs, :] - ref)
        early = (row > first) if reverse else (row < first)
        kf = k * jnp.exp2(jnp.where(early, ref - gc, MASKED_LOG))
        stacked = jnp.concatenate([jnp.where(head_of_lane == h, qi, 0.0) for h in range(GLA_HEADS)], axis=0)
        prod = _bdot_nt(stacked, kf)
        for h in range(GLA_HEADS):
            a_off[h].append(prod[h * SUB:(h + 1) * SUB])

    lane = lax.broadcasted_iota(jnp.int32, (LANES, 2 * GLA_DV), 0)
    col = lax.broadcasted_iota(jnp.int32, (LANES, 2 * GLA_DV), 1)
    head_sum = ((lane // GLA_DK) == (col // GLA_DV)).astype(BF16)
    sub_row = lax.broadcasted_iota(jnp.int32, (SUB, LANES), 0)
    npair = GLA_HEADS // 2
    terms = []
    for p in range(npair):
        ls = slice(p * LANES, (p + 1) * LANES)
        for i in range(nsub):
            yield
            rs = slice(i * SUB, (i + 1) * SUB)
            gci, qsi = gc[rs, ls], qs[rs, ls]
            for jj in range(SUB):
                j = i * SUB + jj
                later = (sub_row <= jj) if reverse else (sub_row >= jj)
                e = jnp.exp2(jnp.where(later, gci - gc[j:j + 1, ls], MASKED_LOG))
                terms.append(qsi * k[j:j + 1, ls] * e)
    yield
    r = jnp.dot(jnp.concatenate(terms, axis=0).astype(BF16), head_sum, preferred_element_type=F32)
    o_diag = []
    for p in range(npair):
        yield
        vs = slice(p * 2 * GLA_DV, (p + 1) * 2 * GLA_DV)
        blocks = []
        for i in range(nsub):
            base = (p * nsub + i) * SUB * SUB
            acc = r[base:base + SUB, :] * v[i * SUB:i * SUB + 1, vs]
            for jj in range(1, SUB):
                acc = acc + r[base + jj * SUB:base + (jj + 1) * SUB, :] * v[i * SUB + jj:i * SUB + jj + 1, vs]
            blocks.append(acc)
        o_diag.append(jnp.concatenate(blocks, axis=0))

    outs = []
    for h in range(GLA_HEADS):
        yield
        sl = slice(h * GLA_DK, (h + 1) * GLA_DK)
        vh = v[:, h * GLA_DV:(h + 1) * GLA_DV]
        st = state_ref[d, h]
        a = jnp.concatenate(a_off[h], axis=0)
        od = o_diag[h // 2][:, (h % 2) * GLA_DV:(h % 2 + 1) * GLA_DV]
        outs.append(_bdot_nt(qd[:, sl], st) + _bdot(a, vh) + od)
        state_ref[d, h] = st * egl[:, sl] + lax.dot_general(
            vh.astype(BF16), kt[:, sl].astype(BF16), (((0,), (0,)), ((), ())), preferred_element_type=F32)
    return jnp.concatenate(outs, axis=1)


def _gla_scan_body(qf_ref, kf_ref, vf_ref, smf_ref, qb_ref, kb_ref, vb_ref, smb_ref, up_ref, upb_ref,
                   of_ref, ob_ref, state_ref, *, nc):
    n = pl.program_id(1)

    @pl.when(n == 0)
    def _():
        state_ref[...] = jnp.zeros_like(state_ref)

    ins = ((qf_ref, kf_ref, vf_ref, smf_ref, of_ref), (qb_ref, kb_ref, vb_ref, smb_ref, ob_ref))

    together = 2 if nc % 2 == 0 else 1

    def scan_steps(tt, carry):
        running = []
        for t in [tt * together + s for s in range(together)]:
            for d, (q_ref, k_ref, v_ref, sm_ref, o_ref) in enumerate(ins):
                c = t if d == 0 else nc - 1 - t
                rows = pl.ds(pl.multiple_of(c * CHUNK, CHUNK), CHUNK)
                gk = jax.nn.log_sigmoid(_hdot(sm_ref[0, rows, :], up_ref[d]) + upb_ref[d]) * (LOG2E / GLA_NORMALIZER)
                qs = q_ref[0, rows, :].astype(F32) * (GLA_DK ** -0.5)
                gen = _gla_chunk(qs, k_ref[0, rows, :].astype(F32), v_ref[0, rows, :].astype(F32), gk,
                                 state_ref, d, reverse=(d == 1))
                running.append((gen, o_ref, rows))
        while running:
            unfinished = []
            for gen, o_ref, rows in running:
                try:
                    next(gen)
                    unfinished.append((gen, o_ref, rows))
                except StopIteration as done:
                    o_ref[0, rows, :] = done.value.astype(o_ref.dtype)
            running = unfinished
        return carry

    lax.fori_loop(0, nc // together, scan_steps, 0)


def gla_scan(proj, small, up_pad, upb):
    b, s, _ = proj.shape
    nchunk = s // CHUNK
    nc = max(c for c in (4, 2, 1) if nchunk % c == 0)
    nblk = nchunk // nc
    rows = nc * CHUNK

    def specs(d):
        def cidx(n):
            return n if d == 0 else nblk - 1 - n
        return [pl.BlockSpec((1, rows, GLA_QK), lambda bb, n: (bb, cidx(n), OFF_GQ // GLA_QK)),
                pl.BlockSpec((1, rows, GLA_QK), lambda bb, n: (bb, cidx(n), OFF_GK // GLA_QK)),
                pl.BlockSpec((1, rows, GLA_W), lambda bb, n: (bb, cidx(n), OFF_GV // GLA_W)),
                pl.BlockSpec((1, rows, LANES), lambda bb, n: (bb, cidx(n), 0))]

    out = jax.ShapeDtypeStruct((b, s, GLA_W), BF16)
    return pl.pallas_call(
        functools.partial(_gla_scan_body, nc=nc),
        out_shape=(out, out),
        grid=(b, nblk),
        in_specs=specs(0) + specs(1) + [pl.BlockSpec((2, LANES, GLA_QK), lambda bb, n: (0, 0, 0)),
                                        pl.BlockSpec((2, 1, GLA_QK), lambda bb, n: (0, 0, 0))],
        out_specs=(pl.BlockSpec((1, rows, GLA_W), lambda bb, n: (bb, n, 0)),
                   pl.BlockSpec((1, rows, GLA_W), lambda bb, n: (bb, nblk - 1 - n, 0))),
        scratch_shapes=[pltpu.VMEM((2, GLA_HEADS, GLA_DV, GLA_DK), F32)],
        compiler_params=_cparams(("arbitrary", "arbitrary")),
        name="gla_scan",
    )(proj, proj, proj, small, proj, proj, proj, small, up_pad, upb)


def _gated_norm(o, gate, g, heads, d):
    outs = []
    for h in range(heads):
        sl = slice(h * d, (h + 1) * d)
        t = o[:, sl]
        y = t * lax.rsqrt(jnp.mean(t * t, axis=-1, keepdims=True) + EPS) * g
        gt = gate[:, sl].astype(F32)
        outs.append(y * (gt * jax.nn.sigmoid(gt)))
    return jnp.concatenate(outs, axis=1)


def _merge_body(dnf_ref, dnb_ref, dgate_ref, dng_ref, att_ref, glf_ref, glb_ref, ggate_ref, glg_ref, o_ref):
    o_dn = _gated_norm(dnf_ref[0].astype(F32) + dnb_ref[0].astype(F32), dgate_ref[0], dng_ref[...],
                       DN_HEADS, DN_D)
    o_gla = _gated_norm(glf_ref[0].astype(F32) + glb_ref[0].astype(F32), ggate_ref[0], glg_ref[...],
                        GLA_HEADS, GLA_DV)
    o_ref[0, :, 0:DN_W] = o_dn.astype(BF16)
    o_ref[0, :, DN_W:DN_W + ATT_W] = att_ref[0]
    o_ref[0, :, DN_W + ATT_W:] = o_gla.astype(BF16)


def merge_heads(dn_f, dn_b, proj, dn_g, att, gl_f, gl_b, gla_g, ts=512):
    b, s, _ = proj.shape
    ts = _tile(s, ts, 16)
    mix_w = DN_W + ATT_W + GLA_W

    def spec(w, blk=0):
        return pl.BlockSpec((1, ts, w), lambda bb, i: (bb, i, blk))

    vec = pl.BlockSpec((1, LANES), lambda bb, i: (0, 0))
    return pl.pallas_call(
        _merge_body,
        out_shape=jax.ShapeDtypeStruct((b, s, mix_w), BF16),
        grid=(b, s // ts),
        in_specs=[spec(DN_W), spec(DN_W), spec(DN_W, OFF_DGATE // DN_W), vec,
                  spec(ATT_W), spec(GLA_W), spec(GLA_W), spec(GLA_W, OFF_GGATE // GLA_W), vec],
        out_specs=spec(mix_w),
        compiler_params=_cparams(("arbitrary", "arbitrary")),
        name="merge_heads",
    )(dn_f, dn_b, proj, dn_g.reshape(1, DN_D), att, gl_f, gl_b, proj, gla_g.reshape(1, GLA_DV))


def _split_points():
    pts, acc = [], 0
    for sz in IN_SIZES[:-1]:
        acc += sz
        pts.append(acc)
    return pts


def _relayout_w_in(w):
    d = w.shape[0]
    (dq, dk, dv, dgate, a_f, a_b, b_f, b_b, aq, ak, av, gq, gkk, gv, ggate, lr_f, lr_b) = jnp.split(
        w, _split_points(), axis=1)
    small = jnp.concatenate([a_f, a_b, b_f, b_b, lr_f, lr_b], axis=1)
    small = jnp.pad(small, ((0, 0), (0, LANES - small.shape[1])))
    cols = jnp.concatenate([dq, dk, dv, dgate, aq, ak, av, gq, gv, ggate, gkk,
                            jnp.zeros((d, PROJ_COLS - OFF_PAD), w.dtype)], axis=1)
    return cols.astype(BF16)[None, None], small.astype(BF16)[None, None]


def _rope_tables(s):
    rows = s // GRID_W
    row = jnp.repeat(jnp.arange(rows, dtype=jnp.int32), GRID_W).astype(F32)
    col = jnp.tile(jnp.arange(GRID_W, dtype=jnp.int32), rows).astype(F32)
    inv_freq = ROPE_THETA ** (-jnp.arange(0, ROPE_SUB, 2, dtype=F32) / ROPE_SUB)
    ang_r = row[:, None] * inv_freq[None, :]
    ang_c = col[:, None] * inv_freq[None, :]
    cos = jnp.concatenate([jnp.cos(ang_r), jnp.cos(ang_r), jnp.cos(ang_c), jnp.cos(ang_c)], axis=1)
    sin = jnp.concatenate([-jnp.sin(ang_r), jnp.sin(ang_r), -jnp.sin(ang_c), jnp.sin(ang_c)], axis=1)
    return cos, sin


def _dn_params(dn_a_log, dn_dt_bias):
    rows = jnp.stack([dn_a_log.reshape(-1), dn_dt_bias.reshape(-1)], axis=0)
    return jnp.pad(rows, ((0, 0), (SM_A, LANES - SM_A - 2 * DN_HEADS)))


def _gla_params(gla_up, gla_up_b):
    ups = []
    for d in range(2):
        lo = SM_LR + d * GLA_RANK
        ups.append(jnp.pad(gla_up[d], ((lo, LANES - lo - GLA_RANK), (0, 0))))
    return jnp.stack(ups, axis=0), gla_up_b[:, None, :]


def _mixer(x16, bsz, s, w_in, dn_conv, dn_a_log, dn_dt_bias, dn_norm_g, att_qn_g, att_kn_g,
           gla_up, gla_up_b, gla_norm_g, rope):
    t = bsz * s
    bm = _tile(t, 1024, 16)
    one_group = _block_table(jnp.zeros((t // bm,), jnp.int32))
    w_main, w_small = _relayout_w_in(w_in)
    proj = grouped_matmul(x16, w_main, 0, one_group, bm, 1024, BF16).reshape(bsz, s, PROJ_COLS)
    small = grouped_matmul(x16, w_small, 0, one_group, bm, LANES, F32).reshape(bsz, s, LANES)

    q, k, v = dn_prep(proj, dn_conv)
    u, wq, ktt, qk, egl = dn_wy(q, k, v, small, _dn_params(dn_a_log, dn_dt_bias))
    dn_f, dn_b = dn_rec(u, wq, ktt, qk, egl)

    cos, sin_signed = rope
    aq, ak, av, kn = att_prep(proj, cos, sin_signed, att_qn_g, att_kn_g)
    o_att = flash_attention(aq, ak, av, kn)

    gl_f, gl_b = gla_scan(proj, small, *_gla_params(gla_up, gla_up_b))

    return merge_heads(dn_f, dn_b, proj, dn_norm_g, o_att, gl_f, gl_b, gla_norm_g).reshape(t, -1)


def _moe(x32, x16, router_w, w_gate, w_up, w_down, layer, ln_g, ln_b, alpha):
    t, d = x32.shape
    rw = jnp.pad(router_w, ((0, 0), (0, LANES - N_EXPERTS)))
    logits = router_logits(x32, rw)[:, :N_EXPERTS]
    top_val, top_idx = lax.top_k(logits, TOP_K)
    gates = jax.nn.softmax(top_val, axis=-1)
    e_flat = top_idx.reshape(-1).astype(jnp.int32)
    tok_flat = jnp.repeat(jnp.arange(t, dtype=jnp.int32), TOP_K)
    onehot = (e_flat[:, None] == jnp.arange(N_EXPERTS, dtype=jnp.int32)[None, :]).astype(jnp.int32)
    csum = jnp.cumsum(onehot, axis=0)
    counts = csum[-1]
    rank = jnp.take_along_axis(csum, e_flat[:, None], axis=1)[:, 0] - 1
    padded = (counts + MOE_BLOCK - 1) // MOE_BLOCK * MOE_BLOCK
    pstart = jnp.cumsum(padded) - padded
    pend = pstart + padded
    dest = pstart[e_flat] + rank
    nb = -(-(TOP_K * t) // MOE_BLOCK) + N_EXPERTS
    cap = nb * MOE_BLOCK
    buf_tok = jnp.zeros((cap,), jnp.int32).at[dest].set(tok_flat)
    block_start = jnp.arange(nb, dtype=jnp.int32) * MOE_BLOCK
    block_exp = jnp.minimum(jnp.sum(block_start[:, None] >= pend[None, :], axis=-1), N_EXPERTS - 1)
    table = _block_table(block_exp, block_start < pend[-1])
    xb = x16[buf_tok]
    h = grouped_swiglu(xb, w_gate, w_up, layer, table, MOE_BLOCK, 1024)
    yb = grouped_matmul(h, w_down, layer, table, MOE_BLOCK, 512, BF16)
    dest_by_choice = dest.reshape(t, TOP_K).T.reshape(-1)
    return combine_ln(x32, yb[dest_by_choice], gates, ln_g, ln_b, alpha)


def _dense_ffn(x32, x16, w_gate, w_up, w_down, layer, ln_g, ln_b, alpha):
    t = x32.shape[0]
    bm = _tile(t, 1024, 16)
    h = grouped_swiglu(x16, w_gate, w_up, layer, _block_table(jnp.zeros((t // bm,), jnp.int32)), bm, 512)
    return matmul_res_ln(h, w_down[layer].astype(BF16), x32, ln_g, ln_b, alpha)


def kernel(x, w_in, dn_conv, dn_a_log, dn_dt_bias, dn_norm_g, att_qn_g, att_kn_g, gla_up, gla_up_b,
           gla_norm_g, w_out, ln1_g, ln1_b, ln2_g, ln2_b, ffn_w_gate, ffn_w_up, ffn_w_down, router_w,
           exp_w_gate, exp_w_up, exp_w_down):
    bsz, s, d = x.shape
    depth = w_in.shape[0]
    alpha = (2.0 * depth) ** 0.25
    t = bsz * s
    rope = _rope_tables(s)
    x32 = x.reshape(t, d)
    x16 = x32.astype(BF16)
    for layer in range(depth):
        mix = _mixer(x16, bsz, s, w_in[layer], dn_conv[layer], dn_a_log[layer], dn_dt_bias[layer],
                     dn_norm_g[layer], att_qn_g[layer], att_kn_g[layer], gla_up[layer],
                     gla_up_b[layer], gla_norm_g[layer], rope)
        x32, x16 = matmul_res_ln(mix, w_out[layer].astype(BF16), x32, ln1_g[layer], ln1_b[layer], alpha)
        j = layer // 2
        if layer % 2 == 0:
            x32, x16 = _dense_ffn(x32, x16, ffn_w_gate, ffn_w_up, ffn_w_down, j,
                                  ln2_g[layer], ln2_b[layer], alpha)
        else:
            x32, x16 = _moe(x32, x16, router_w[j], exp_w_gate, exp_w_up, exp_w_down, j,
                            ln2_g[layer], ln2_b[layer], alpha)
    return x32.reshape(bsz, s, d)
```

```python
import functools
import math

import jax
import jax.numpy as jnp
from jax import lax
from jax.experimental import pallas as pl
from jax.experimental.pallas import tpu as pltpu

F32 = jnp.float32
BF16 = jnp.bfloat16
HIGHEST = lax.Precision.HIGHEST

DN_HEADS, DN_D = 6, 128
ATT_HEADS, ATT_KV_HEADS, ATT_DH = 6, 2, 128
ATT_GROUP = ATT_HEADS // ATT_KV_HEADS
ROPE_SUB, ROPE_THETA, GRID_W = 64, 10000.0, 64
GLA_HEADS, GLA_DK, GLA_DV, GLA_RANK = 4, 64, 128, 16
GLA_NORMALIZER = 16.0
CHUNK = 64
SUB = 8
MASKED_LOG = -1e30
WY_ROWS = 2 * CHUNK
EGL_ROWS = 8
CONV_W = 5
N_EXPERTS, TOP_K, MOE_BLOCK = 8, 2, 256
EPS = 1e-6
LOG2E = 1.4426950408889634
LANES = 128
HALO = 16

DN_W = DN_HEADS * DN_D
ATT_W = ATT_HEADS * ATT_DH
ATT_KV_W = ATT_KV_HEADS * ATT_DH
GLA_QK = GLA_HEADS * GLA_DK
GLA_W = GLA_HEADS * GLA_DV
IN_SIZES = (DN_W, DN_W, DN_W, DN_W, DN_HEADS, DN_HEADS, DN_HEADS, DN_HEADS,
            ATT_W, ATT_KV_W, ATT_KV_W, GLA_QK, GLA_QK, GLA_W, GLA_W, GLA_RANK, GLA_RANK)
OFF_DQ, OFF_DGATE, OFF_AQ, OFF_AK, OFF_AV = 0, 2304, 3072, 3840, 4096
OFF_GQ, OFF_GV, OFF_GGATE, OFF_GK, OFF_PAD = 4352, 4608, 5120, 5632, 5888
PROJ_COLS = 6144
SM_A, SM_B, SM_LR = 0, 2 * DN_HEADS, 4 * DN_HEADS

VMEM_LIMIT = 56 * 1024 * 1024


def _cparams(sem, vmem=VMEM_LIMIT):
    return pltpu.CompilerParams(dimension_semantics=sem, vmem_limit_bytes=vmem)


def _tile(n, pref, quantum=LANES):
    if n <= pref:
        return n
    t = pref - pref % quantum
    while n % t:
        t -= quantum
    return t


def _bdot(a, b):
    return jnp.dot(a.astype(BF16), b.astype(BF16), preferred_element_type=F32)


def _bdot_nt(a, b):
    return lax.dot_general(a.astype(BF16), b.astype(BF16), (((1,), (1,)), ((), ())),
                           preferred_element_type=F32)


def _hdot(a, b):
    return jnp.dot(a, b, preferred_element_type=F32, precision=HIGHEST)


TB_GROUP, TB_USED, TB_FIRST, TB_SEG, TB_NEXT, TB_LAST = range(6)


def _block_table(group, used=None):
    group = group.astype(jnp.int32)
    used = jnp.ones_like(group) if used is None else used.astype(jnp.int32)
    first = jnp.concatenate([jnp.ones((1,), jnp.int32), (group[1:] != group[:-1]).astype(jnp.int32)])
    seg = jnp.cumsum(first) - 1
    nseg = seg[-1] + 1
    run_group = jnp.zeros_like(group).at[seg].set(group)
    nxt = run_group[jnp.where(seg + 1 < nseg, seg + 1, 0)]
    last = (seg == nseg - 1).astype(jnp.int32)
    return jnp.stack([group, used, first, seg, nxt, last], axis=0)


def _if_used(be_ref, o_ref, compute):
    used = be_ref[TB_USED, pl.program_id(1)] != 0

    @pl.when(used)
    def _():
        compute()

    @pl.when(jnp.logical_not(used))
    def _():
        o_ref[...] = jnp.zeros_like(o_ref)


def _stream_weights(be_ref, w_hbm, wbuf, w16, sem, layer, tn):
    j, i = pl.program_id(0), pl.program_id(1)
    nj, nb = pl.num_programs(0), pl.num_programs(1)
    nseg = be_ref[TB_SEG, nb - 1] + 1
    nslot = wbuf[0].shape[0]
    run = j * nseg + be_ref[TB_SEG, i]
    slot = lax.rem(run, nslot)

    def copies(group, col, to_slot):
        start = pl.multiple_of(col * tn, tn)
        out = []
        for m, (w, buf) in enumerate(zip(w_hbm, wbuf)):
            src = w.at[layer, :, pl.ds(start, tn)] if len(w.shape) == 3 else w.at[layer, group, :, pl.ds(start, tn)]
            out.append(pltpu.make_async_copy(src, buf.at[to_slot], sem.at[m, to_slot]))
        return out

    @pl.when((j == 0) & (i == 0))
    def _():
        for c in copies(be_ref[TB_GROUP, i], j, slot):
            c.start()

    @pl.when(be_ref[TB_FIRST, i] != 0)
    def _():
        for c in copies(be_ref[TB_GROUP, i], j, slot):
            c.wait()
        for m in range(len(w_hbm)):
            w16[m][...] = wbuf[m][slot].astype(BF16)
        last = be_ref[TB_LAST, i] != 0

        @pl.when(jnp.logical_not(last) | (j + 1 < nj))
        def _():
            for c in copies(be_ref[TB_NEXT, i], jnp.where(last, j + 1, j), lax.rem(run + 1, nslot)):
                c.start()


def _gmm_body(be_ref, x_ref, w_ref, o_ref, *scratch, layer, tn):
    if scratch:
        wbuf, w16, sem = scratch
        _stream_weights(be_ref, [w_ref], [wbuf], [w16], sem, layer, tn)
        w_use = w16
    else:
        w_use = w_ref

    def compute():
        o_ref[...] = jnp.dot(x_ref[...], w_use[...], preferred_element_type=F32).astype(o_ref.dtype)

    _if_used(be_ref, o_ref, compute)


def _stream_scratch(k, tn, n_mats, slots):
    return ([pltpu.VMEM((slots, k, tn), F32) for _ in range(n_mats)]
            + [pltpu.VMEM((k, tn), BF16) for _ in range(n_mats)] + [pltpu.SemaphoreType.DMA((n_mats, slots))])


def grouped_matmul(x, w, layer, block_table, bm, tn, out_dtype, slots=2):
    m, k = x.shape
    n = w.shape[-1]
    tn = _tile(n, tn)
    if w.dtype == BF16:
        w_spec = pl.BlockSpec((None, None, k, tn), lambda j, i, be: (layer, 0, 0, j))
        scratch = []
    else:
        w_spec = pl.BlockSpec(memory_space=pl.ANY)
        scratch = _stream_scratch(k, tn, 1, slots)
    return pl.pallas_call(
        functools.partial(_gmm_body, layer=layer, tn=tn),
        out_shape=jax.ShapeDtypeStruct((m, n), out_dtype),
        grid_spec=pltpu.PrefetchScalarGridSpec(
            num_scalar_prefetch=1, grid=(n // tn, m // bm),
            in_specs=[pl.BlockSpec((bm, k), lambda j, i, be: (i, 0)), w_spec],
            out_specs=pl.BlockSpec((bm, tn), lambda j, i, be: (i, j)),
            scratch_shapes=scratch),
        compiler_params=_cparams(("arbitrary", "arbitrary")),
        name="grouped_matmul",
    )(block_table, x, w)


def _gswiglu_body(be_ref, x_ref, wg_ref, wu_ref, o_ref, wg_buf, wu_buf, wg16, wu16, sem, *, layer, tn):
    _stream_weights(be_ref, [wg_ref, wu_ref], [wg_buf, wu_buf], [wg16, wu16], sem, layer, tn)

    def compute():
        x = x_ref[...]
        g = jnp.dot(x, wg16[...], preferred_element_type=F32)
        u = jnp.dot(x, wu16[...], preferred_element_type=F32)
        o_ref[...] = (g * jax.nn.sigmoid(g) * u).astype(o_ref.dtype)

    _if_used(be_ref, o_ref, compute)


def grouped_swiglu(x, wg, wu, layer, block_table, bm, tn, slots=2):
    m, k = x.shape
    n = wg.shape[-1]
    tn = _tile(n, tn)
    hbm = pl.BlockSpec(memory_space=pl.ANY)
    return pl.pallas_call(
        functools.partial(_gswiglu_body, layer=layer, tn=tn),
        out_shape=jax.ShapeDtypeStruct((m, n), BF16),
        grid_spec=pltpu.PrefetchScalarGridSpec(
            num_scalar_prefetch=1, grid=(n // tn, m // bm),
            in_specs=[pl.BlockSpec((bm, k), lambda j, i, be: (i, 0)), hbm, hbm],
            out_specs=pl.BlockSpec((bm, tn), lambda j, i, be: (i, j)),
            scratch_shapes=_stream_scratch(k, tn, 2, slots)),
        compiler_params=_cparams(("arbitrary", "arbitrary")),
        name="grouped_swiglu",
    )(block_table, x, wg, wu)


def _res_ln(res, y, g, b, alpha):
    z = alpha * res + y
    mu = jnp.mean(z, axis=-1, keepdims=True)
    zc = z - mu
    var = jnp.mean(zc * zc, axis=-1, keepdims=True)
    return zc * lax.rsqrt(var + EPS) * g + b


def _mm_res_ln_body(x_ref, w_ref, res_ref, g_ref, b_ref, o32_ref, o16_ref, *acc, nk, alpha):
    kk = pl.program_id(1)
    part = jnp.dot(x_ref[...], w_ref[...], preferred_element_type=F32)

    def finish(y):
        out = _res_ln(res_ref[...], y, g_ref[...], b_ref[...], alpha)
        o32_ref[...] = out
        o16_ref[...] = out.astype(BF16)

    if nk == 1:
        finish(part)
        return
    acc_ref = acc[0]

    @pl.when(kk == 0)
    def _():
        acc_ref[...] = part

    @pl.when((kk > 0) & (kk < nk - 1))
    def _():
        acc_ref[...] += part

    @pl.when(kk == nk - 1)
    def _():
        finish(acc_ref[...] + part)


def matmul_res_ln(x, w, res, g, b, alpha, tm=512, tk=2048):
    m, k = x.shape
    n = w.shape[-1]
    tm = _tile(m, tm, 8)
    tk = _tile(k, tk)
    nk = k // tk
    return pl.pallas_call(
        functools.partial(_mm_res_ln_body, nk=nk, alpha=alpha),
        out_shape=(jax.ShapeDtypeStruct((m, n), F32), jax.ShapeDtypeStruct((m, n), BF16)),
        grid=(m // tm, nk),
        in_specs=[pl.BlockSpec((tm, tk), lambda i, kk: (i, kk)),
                  pl.BlockSpec((tk, n), lambda i, kk: (kk, 0)),
                  pl.BlockSpec((tm, n), lambda i, kk: (i, 0)),
                  pl.BlockSpec((1, n), lambda i, kk: (0, 0)),
                  pl.BlockSpec((1, n), lambda i, kk: (0, 0))],
        out_specs=(pl.BlockSpec((tm, n), lambda i, kk: (i, 0)),
                   pl.BlockSpec((tm, n), lambda i, kk: (i, 0))),
        scratch_shapes=[pltpu.VMEM((tm, n), F32)] if nk > 1 else [],
        compiler_params=_cparams(("arbitrary", "arbitrary")),
        name="matmul_res_ln",
    )(x, w, res, g.reshape(1, n), b.reshape(1, n))


def _combine_ln_body(res_ref, *refs, alpha):
    y_refs, (gate_ref, g_ref, b_ref, o32_ref, o16_ref) = refs[:TOP_K], refs[TOP_K:]
    gate = gate_ref[...]
    y = y_refs[0][...].astype(F32) * gate[:, 0:1]
    for kk in range(1, TOP_K):
        y = y + y_refs[kk][...].astype(F32) * gate[:, kk:kk + 1]
    out = _res_ln(res_ref[...], y, g_ref[...], b_ref[...], alpha)
    o32_ref[...] = out
    o16_ref[...] = out.astype(BF16)


def combine_ln(res, y, gates, g, b, alpha, tm=512):
    m, n = res.shape
    tm = _tile(m, tm, 16)
    nblk = m // tm
    row = pl.BlockSpec((tm, n), lambda i: (i, 0))
    vec = pl.BlockSpec((1, n), lambda i: (0, 0))

    def choice(kk):
        return pl.BlockSpec((tm, n), lambda i: (kk * nblk + i, 0))

    return pl.pallas_call(
        functools.partial(_combine_ln_body, alpha=alpha),
        out_shape=(jax.ShapeDtypeStruct((m, n), F32), jax.ShapeDtypeStruct((m, n), BF16)),
        grid=(nblk,),
        in_specs=[row] + [choice(kk) for kk in range(TOP_K)] + [pl.BlockSpec((tm, TOP_K), lambda i: (i, 0)), vec, vec],
        out_specs=(row, row),
        compiler_params=_cparams(("arbitrary",)),
        name="combine_ln",
    )(res, *([y] * TOP_K), gates, g.reshape(1, n), b.reshape(1, n))


def _router_body(x_ref, w_ref, o_ref):
    o_ref[...] = _hdot(x_ref[...], w_ref[...])


def router_logits(x, w_pad, tm=512):
    m, k = x.shape
    n = w_pad.shape[-1]
    tm = _tile(m, tm, 8)
    return pl.pallas_call(
        _router_body,
        out_shape=jax.ShapeDtypeStruct((m, n), F32),
        grid=(m // tm,),
        in_specs=[pl.BlockSpec((tm, k), lambda i: (i, 0)), pl.BlockSpec((k, n), lambda i: (0, 0))],
        out_specs=pl.BlockSpec((tm, n), lambda i: (i, 0)),
        compiler_params=_cparams(("arbitrary",)),
        name="router_logits",
    )(x, w_pad)


def _dn_prep_body(cur_ref, prev_ref, next_ref, w_ref, q_ref, k_ref, v_ref, buf_ref, *, ts, nblk, sub):
    i = pl.program_id(1)
    buf_ref[0:HALO, :] = jnp.where(i > 0, prev_ref[0], jnp.zeros_like(prev_ref[0]))
    buf_ref[HALO:HALO + ts, :] = cur_ref[0]
    buf_ref[HALO + ts:2 * HALO + ts, :] = jnp.where(i < nblk - 1, next_ref[0], jnp.zeros_like(next_ref[0]))
    win = sub + 2 * HALO
    ri = lax.broadcasted_iota(jnp.int32, (sub, win), 0)
    ci = lax.broadcasted_iota(jnp.int32, (sub, win), 1)
    centre = CONV_W // 2
    shift = {j: (ci == ri + HALO + j - centre).astype(BF16) for j in range(CONV_W) if j != centre}
    for r in range(ts // sub):
        window = buf_ref[r * sub:r * sub + win, :]
        acc = window[HALO:HALO + sub, :].astype(F32) * w_ref[centre:centre + 1, :]
        for j, sel in shift.items():
            acc = acc + jnp.dot(sel, window, preferred_element_type=F32) * w_ref[j:j + 1, :]
        y = acc * jax.nn.sigmoid(acc)
        rows = slice(r * sub, (r + 1) * sub)
        for h in range(DN_HEADS):
            for off, ref in ((0, q_ref), (DN_W, k_ref)):
                t = y[:, off + h * DN_D:off + (h + 1) * DN_D]
                ref[0, rows, h * DN_D:(h + 1) * DN_D] = t * lax.rsqrt(jnp.sum(t * t, axis=-1, keepdims=True) + EPS)
        v_ref[0, rows, :] = y[:, 2 * DN_W:3 * DN_W]


def dn_prep(proj, conv_w, ts=512):
    b, s, _ = proj.shape
    ts = _tile(s, ts, HALO)
    sub = _tile(ts, LANES, HALO)
    nblk = s // ts
    c = 3 * DN_W
    hb = ts // HALO
    out = jax.ShapeDtypeStruct((b, s, DN_W), F32)
    ospec = pl.BlockSpec((1, ts, DN_W), lambda bb, i: (bb, i, 0))
    return pl.pallas_call(
        functools.partial(_dn_prep_body, ts=ts, nblk=nblk, sub=sub),
        out_shape=(out, out, out),
        grid=(b, nblk),
        in_specs=[pl.BlockSpec((1, ts, c), lambda bb, i: (bb, i, 0)),
                  pl.BlockSpec((1, HALO, c), lambda bb, i: (bb, jnp.maximum(i * hb - 1, 0), 0)),
                  pl.BlockSpec((1, HALO, c), lambda bb, i: (bb, jnp.minimum((i + 1) * hb, nblk * hb - 1), 0)),
                  pl.BlockSpec((CONV_W, c), lambda bb, i: (0, 0))],
        out_specs=(ospec, ospec, ospec),
        scratch_shapes=[pltpu.VMEM((ts + 2 * HALO, c), BF16)],
        compiler_params=_cparams(("arbitrary", "arbitrary")),
        name="dn_prep",
    )(proj, proj, proj, conv_w)


def _dn_wy_body(q_ref, k_ref, v_ref, sm_ref, par_ref, u_ref, wq_ref, ktt_ref, qk_ref, egl_ref, *, tiles):
    def tile(t, carry):
        _dn_wy_tile(t, q_ref, k_ref, v_ref, sm_ref, par_ref, u_ref, wq_ref, ktt_ref, qk_ref, egl_ref)
        return carry

    lax.fori_loop(0, tiles, tile, 0)


def _dn_wy_tile(t, q_ref, k_ref, v_ref, sm_ref, par_ref, u_ref, wq_ref, ktt_ref, qk_ref, egl_ref):
    r = WY_ROWS
    nck = r // CHUNK
    rows_t = pl.ds(pl.multiple_of(t * r, r), r)
    sm = sm_ref[0, rows_t, :]
    par = par_ref[...]
    g_all = -jnp.exp(par[0:1, :]) * jax.nn.softplus(sm + par[1:2, :])
    beta_all = jax.nn.sigmoid(sm)
    ri = lax.broadcasted_iota(jnp.int32, (r, r), 0)
    ci = lax.broadcasted_iota(jnp.int32, (r, r), 1)
    same = (ri // CHUNK) == (ci // CHUNK)
    eye = (ri == ci).astype(F32)
    masks = []
    for d in range(2):
        delta = ri - ci if d == 0 else ci - ri
        masks.append((same & (delta >= 0), same & (delta > 0)))
    scale = DN_D ** -0.5
    gtot = _hdot(same.astype(F32), g_all)
    gcs = []
    for d in range(2):
        gc = _hdot(masks[d][0].astype(F32), g_all)
        gcs.append((gc, gc.T))
    chains = [(h, d) for h in range(DN_HEADS) for d in range(2)]
    qk_cols = ([], [])
    nm, tinv, rhs = {}, {}, {}
    for h in range(DN_HEADS):
        sl = slice(h * DN_D, (h + 1) * DN_D)
        q = q_ref[0, rows_t, sl] * scale
        k = k_ref[0, rows_t, sl]
        v = v_ref[0, rows_t, sl]
        kq = _bdot_nt(jnp.concatenate([k, q], axis=0), k)
        kk, qk = kq[:r], kq[r:]
        for d in range(2):
            lane = SM_A + d * DN_HEADS + h
            incl, strict = masks[d]
            gc, gct = gcs[d]
            gcol = jnp.broadcast_to(gc[:, lane:lane + 1], (r, r))
            gt = jnp.broadcast_to(gtot[:, lane:lane + 1], (r, r))
            bcol = jnp.broadcast_to(beta_all[:, SM_B + d * DN_HEADS + h:SM_B + d * DN_HEADS + h + 1], (r, r))
            grow = gct[lane:lane + 1, :]
            decay = jnp.where(incl, jnp.exp(jnp.where(incl, gcol - grow, 0.0)), 0.0)
            nm[h, d] = jnp.where(strict, -(bcol * kk) * decay, 0.0)
            tinv[h, d] = eye + nm[h, d]
            eg = jnp.exp(gcol)
            rhs[h, d] = jnp.concatenate([v * bcol, k * (bcol * eg)], axis=1).astype(BF16)
            qd16 = (q * eg).astype(BF16)
            ktt = _bdot_nt(eye, (k * jnp.exp(gt - gcol)).astype(BF16)).astype(BF16)
            for c in range(nck):
                rows = slice(c * CHUNK, (c + 1) * CHUNK)
                wq_ref[d, 0, t * nck + c, CHUNK:2 * CHUNK, sl] = qd16[rows]
                ktt_ref[d, 0, t * nck + c, :, h * CHUNK:(h + 1) * CHUNK] = ktt[:, rows]
            qkd = qk * decay
            qk_cols[d].append(qkd[:, :CHUNK] + qkd[:, CHUNK:])
            egt = jnp.exp(gt)
            for c in range(nck):
                egl_ref[d, 0, t * nck + c, :, sl] = egt[c * CHUNK:c * CHUNK + EGL_ROWS]
    for d in range(2):
        qk_ref[d, 0, rows_t, :] = jnp.concatenate(qk_cols[d], axis=1).astype(BF16)
    for _ in range(int(math.log2(CHUNK)) - 1):
        for hd in chains:
            nm[hd] = _bdot(nm[hd], nm[hd])
        for hd in chains:
            tinv[hd] = tinv[hd] + _bdot(tinv[hd], nm[hd])
    for h, d in chains:
        sl = slice(h * DN_D, (h + 1) * DN_D)
        uw = _bdot(tinv[h, d], rhs[h, d])
        u_ref[d, 0, rows_t, sl] = uw[:, :DN_D].astype(BF16)
        w16 = uw[:, DN_D:].astype(BF16)
        for c in range(nck):
            wq_ref[d, 0, t * nck + c, 0:CHUNK, sl] = w16[c * CHUNK:(c + 1) * CHUNK]


def dn_wy(q, k, v, small, par):
    b, s, w = q.shape
    nchunk = s // CHUNK
    tiles = max(c for c in (4, 2, 1) if s % (c * WY_ROWS) == 0)
    r = tiles * WY_ROWS
    nck = r // CHUNK
    seq = pl.BlockSpec((1, r, w), lambda bb, i: (bb, i, 0))
    dseq = pl.BlockSpec((2, 1, r, w), lambda bb, i: (0, bb, i, 0))
    return pl.pallas_call(
        functools.partial(_dn_wy_body, tiles=tiles),
        out_shape=(jax.ShapeDtypeStruct((2, b, s, w), BF16),
                   jax.ShapeDtypeStruct((2, b, nchunk, 2 * CHUNK, w), BF16),
                   jax.ShapeDtypeStruct((2, b, nchunk, DN_D, DN_HEADS * CHUNK), BF16),
                   jax.ShapeDtypeStruct((2, b, s, DN_HEADS * CHUNK), BF16),
                   jax.ShapeDtypeStruct((2, b, nchunk, EGL_ROWS, w), F32)),
        grid=(b, s // r),
        in_specs=[seq, seq, seq,
                  pl.BlockSpec((1, r, LANES), lambda bb, i: (bb, i, 0)),
                  pl.BlockSpec((2, LANES), lambda bb, i: (0, 0))],
        out_specs=(dseq,
                   pl.BlockSpec((2, 1, nck, 2 * CHUNK, w), lambda bb, i: (0, bb, i, 0, 0)),
                   pl.BlockSpec((2, 1, nck, DN_D, DN_HEADS * CHUNK), lambda bb, i: (0, bb, i, 0, 0)),
                   pl.BlockSpec((2, 1, r, DN_HEADS * CHUNK), lambda bb, i: (0, bb, i, 0)),
                   pl.BlockSpec((2, 1, nck, EGL_ROWS, w), lambda bb, i: (0, bb, i, 0, 0))),
        compiler_params=_cparams(("arbitrary", "arbitrary")),
        name="dn_wy",
    )(q, k, v, small, par)


def _dn_rec_body(*refs, nc):
    ins = (refs[0:5], refs[5:10])
    outs = refs[10:12]
    state_ref = refs[12]
    n = pl.program_id(1)

    @pl.when(n == 0)
    def _():
        state_ref[...] = jnp.zeros_like(state_ref)

    chains = [(d, h) for d in range(2) for h in range(DN_HEADS)]
    state = {(d, h): state_ref[d, h] for d, h in chains}
    for step in range(nc):
        chunk = (step, nc - 1 - step)
        wq, v_new = {}, {}
        for d, h in chains:
            sl = slice(h * DN_D, (h + 1) * DN_D)
            wq[d, h] = jnp.dot(ins[d][1][0, 0, chunk[d], :, sl], state[d, h].astype(BF16),
                               preferred_element_type=F32)
        for d, h in chains:
            sl = slice(h * DN_D, (h + 1) * DN_D)
            rows = slice(chunk[d] * CHUNK, (chunk[d] + 1) * CHUNK)
            v_new[d, h] = (ins[d][0][0, 0, rows, sl].astype(F32) - wq[d, h][:CHUNK]).astype(BF16)
        for d, h in chains:
            sl = slice(h * DN_D, (h + 1) * DN_D)
            rows = slice(chunk[d] * CHUNK, (chunk[d] + 1) * CHUNK)
            qk = ins[d][3][0, 0, rows, h * CHUNK:(h + 1) * CHUNK]
            outs[d][0, rows, sl] = (wq[d, h][CHUNK:]
                                    + jnp.dot(qk, v_new[d, h], preferred_element_type=F32)).astype(outs[d].dtype)
        for d, h in chains:
            sl = slice(h * DN_D, (h + 1) * DN_D)
            ktt = ins[d][2][0, 0, chunk[d], :, h * CHUNK:(h + 1) * CHUNK]
            state[d, h] = (state[d, h] * ins[d][4][0, 0, chunk[d], 0:1, sl]
                           + jnp.dot(ktt, v_new[d, h], preferred_element_type=F32))
    for d, h in chains:
        state_ref[d, h] = state[d, h]


def dn_rec(u, wq, ktt, qk, egl):
    _, b, s, w = u.shape
    nchunk = s // CHUNK
    nc = max(c for c in (4, 2, 1) if nchunk % c == 0)
    nblk = nchunk // nc
    rows = nc * CHUNK

    def specs(d):
        def blk(n):
            return n if d == 0 else nblk - 1 - n
        return [pl.BlockSpec((1, 1, rows, w), lambda bb, n: (d, bb, blk(n), 0)),
                pl.BlockSpec((1, 1, nc, 2 * CHUNK, w), lambda bb, n: (d, bb, blk(n), 0, 0)),
                pl.BlockSpec((1, 1, nc, DN_D, DN_HEADS * CHUNK), lambda bb, n: (d, bb, blk(n), 0, 0)),
                pl.BlockSpec((1, 1, rows, DN_HEADS * CHUNK), lambda bb, n: (d, bb, blk(n), 0)),
                pl.BlockSpec((1, 1, nc, EGL_ROWS, w), lambda bb, n: (d, bb, blk(n), 0, 0))]

    out = jax.ShapeDtypeStruct((b, s, w), BF16)
    return pl.pallas_call(
        functools.partial(_dn_rec_body, nc=nc),
        out_shape=(out, out),
        grid=(b, nblk),
        in_specs=specs(0) + specs(1),
        out_specs=(pl.BlockSpec((1, rows, w), lambda bb, n: (bb, n, 0)),
                   pl.BlockSpec((1, rows, w), lambda bb, n: (bb, nblk - 1 - n, 0))),
        scratch_shapes=[pltpu.VMEM((2, DN_HEADS, DN_D, DN_D), F32)],
        compiler_params=_cparams(("arbitrary", "arbitrary")),
        name="dn_rec",
    )(u, wq, ktt, qk, egl, u, wq, ktt, qk, egl)


def _rope_head(x, g, cos, sin_signed, first_half):
    x = x.astype(F32)
    xf = x * lax.rsqrt(jnp.mean(x * x, axis=-1, keepdims=True) + EPS) * g
    partner = jnp.where(first_half, pltpu.roll(xf, LANES - ROPE_SUB // 2, axis=1),
                        pltpu.roll(xf, ROPE_SUB // 2, axis=1))
    return xf * cos + partner * sin_signed


def _att_prep_body(q_ref, k_ref, v_ref, cos_ref, sin_ref, qg_ref, kg_ref, qo_ref, ko_ref, vo_ref, kn_ref):
    cos = cos_ref[...]
    sin_signed = sin_ref[...]
    lane = lax.broadcasted_iota(jnp.int32, cos.shape, 1)
    first_half = (lane % ROPE_SUB) < ROPE_SUB // 2
    scale = ATT_DH ** -0.5 * LOG2E
    for h in range(ATT_HEADS):
        sl = slice(h * ATT_DH, (h + 1) * ATT_DH)
        qo_ref[0, :, sl] = (_rope_head(q_ref[0, :, sl], qg_ref[...], cos, sin_signed, first_half)
                            * scale).astype(BF16)
    for h in range(ATT_KV_HEADS):
        sl = slice(h * ATT_DH, (h + 1) * ATT_DH)
        k16 = _rope_head(k_ref[0, :, sl], kg_ref[...], cos, sin_signed, first_half).astype(BF16)
        ko_ref[0, :, sl] = k16
        kf = k16.astype(F32)
        n2 = jnp.max(jnp.sum(kf * kf, axis=-1, keepdims=True), axis=0, keepdims=True)
        kn_ref[0, 0, :, sl] = jnp.broadcast_to(n2, (1, ATT_DH))
    for h in range(ATT_KV_HEADS):
        vo_ref[0, :, 2 * h * ATT_DH:(2 * h + 1) * ATT_DH] = v_ref[0, :, h * ATT_DH:(h + 1) * ATT_DH]
        vo_ref[0, :, (2 * h + 1) * ATT_DH:(2 * h + 2) * ATT_DH] = jnp.ones((v_ref.shape[1], ATT_DH), BF16)


def att_prep(proj, cos, sin_signed, qg, kg, ts=512):
    b, s, _ = proj.shape
    ts = _tile(s, ts, 16)
    tab = pl.BlockSpec((ts, ATT_DH), lambda bb, i: (i, 0))
    vec = pl.BlockSpec((1, ATT_DH), lambda bb, i: (0, 0))
    return pl.pallas_call(
        _att_prep_body,
        out_shape=(jax.ShapeDtypeStruct((b, s, ATT_W), BF16),
                   jax.ShapeDtypeStruct((b, s, ATT_KV_W), BF16),
                   jax.ShapeDtypeStruct((b, s, 2 * ATT_KV_W), BF16),
                   jax.ShapeDtypeStruct((b, s // ts, 1, ATT_KV_W), F32)),
        grid=(b, s // ts),
        in_specs=[pl.BlockSpec((1, ts, ATT_W), lambda bb, i: (bb, i, OFF_AQ // ATT_W)),
                  pl.BlockSpec((1, ts, ATT_KV_W), lambda bb, i: (bb, i, OFF_AK // ATT_KV_W)),
                  pl.BlockSpec((1, ts, ATT_KV_W), lambda bb, i: (bb, i, OFF_AV // ATT_KV_W)),
                  tab, tab, vec, vec],
        out_specs=(pl.BlockSpec((1, ts, ATT_W), lambda bb, i: (bb, i, 0)),
                   pl.BlockSpec((1, ts, ATT_KV_W), lambda bb, i: (bb, i, 0)),
                   pl.BlockSpec((1, ts, 2 * ATT_KV_W), lambda bb, i: (bb, i, 0)),
                   pl.BlockSpec((1, 1, 1, ATT_KV_W), lambda bb, i: (bb, i, 0, 0))),
        compiler_params=_cparams(("arbitrary", "arbitrary")),
        name="att_prep",
    )(proj, proj, proj, cos, sin_signed, qg.reshape(1, ATT_DH), kg.reshape(1, ATT_DH))


MAX_FIXED_SHIFT = 60.0


def _flash_body(q_ref, k_ref, v_ref, kn_ref, o_ref, *, tk, nkv):
    tq = q_ref.shape[1]
    heads = range(ATT_GROUP)
    qs = [q_ref[0, :, h * ATT_DH:(h + 1) * ATT_DH] for h in heads]
    unroll = max(u for u in (4, 2, 1) if nkv % u == 0)

    def tiles(t):
        start = pl.multiple_of(t * tk, tk)
        return k_ref[0, pl.ds(start, tk), :], v_ref[0, pl.ds(start, tk), :]

    def scores(h, ks):
        return lax.dot_general(qs[h], ks, (((1,), (1,)), ((), ())), preferred_element_type=F32)

    def finish(accs):
        for h in heads:
            o_ref[0, :, h * ATT_DH:(h + 1) * ATT_DH] = (accs[h][:, :ATT_DH] / accs[h][:, ATT_DH:]).astype(o_ref.dtype)

    kmax = jnp.sqrt(jnp.max(kn_ref[0, :, 0, :], axis=0, keepdims=True))[:, 0:1]
    shifts = []
    for h in heads:
        qf = qs[h].astype(F32)
        shifts.append(jnp.sqrt(jnp.sum(qf * qf, axis=-1, keepdims=True)) * (kmax * 1.001))
    worst = shifts[0]
    for h in heads[1:]:
        worst = jnp.maximum(worst, shifts[h])
    fixed_ok = jnp.max(worst) <= MAX_FIXED_SHIFT

    @pl.when(fixed_ok)
    def _():
        def step(t, accs):
            ks, vs = tiles(t)
            return tuple(accs[h] + jnp.dot(jnp.exp2(scores(h, ks) - shifts[h]).astype(BF16), vs,
                                           preferred_element_type=F32) for h in heads)

        init = tuple(jnp.zeros((tq, 2 * ATT_DH), F32) for _ in heads)
        finish(lax.fori_loop(0, nkv, step, init, unroll=unroll))

    @pl.when(jnp.logical_not(fixed_ok))
    def _():
        def step(t, carry):
            ks, vs = tiles(t)
            new = []
            for h in heads:
                m, acc = carry[h]
                sc = scores(h, ks)
                m_new = jnp.maximum(m, jnp.max(sc, axis=-1, keepdims=True))
                p = jnp.exp2(sc - m_new)
                acc = jnp.exp2(m - m_new) * acc + jnp.dot(p.astype(BF16), vs, preferred_element_type=F32)
                new.append((m_new, acc))
            return tuple(new)

        init = tuple((jnp.full((tq, 1), -1e30, F32), jnp.zeros((tq, 2 * ATT_DH), F32)) for _ in heads)
        res = lax.fori_loop(0, nkv, step, init, unroll=unroll)
        finish([res[h][1] for h in heads])


def flash_attention(q, k, v, kn, tq=512, tk=512):
    b, s, _ = q.shape
    tq = _tile(s, tq, 16)
    tk = _tile(s, tk, 16)
    gw = ATT_GROUP * ATT_DH
    return pl.pallas_call(
        functools.partial(_flash_body, tk=tk, nkv=s // tk),
        out_shape=jax.ShapeDtypeStruct((b, s, ATT_W), BF16),
        grid=(b, ATT_KV_HEADS, s // tq),
        in_specs=[pl.BlockSpec((1, tq, gw), lambda bb, g, i: (bb, i, g)),
                  pl.BlockSpec((1, s, ATT_DH), lambda bb, g, i: (bb, 0, g)),
                  pl.BlockSpec((1, s, 2 * ATT_DH), lambda bb, g, i: (bb, 0, g)),
                  pl.BlockSpec((1, kn.shape[1], 1, ATT_DH), lambda bb, g, i: (bb, 0, 0, g))],
        out_specs=pl.BlockSpec((1, tq, gw), lambda bb, g, i: (bb, i, g)),
        compiler_params=_cparams(("arbitrary", "arbitrary", "arbitrary")),
        name="flash_attention",
    )(q, k, v, kn)


def _gla_chunk(qs, k, v, gk, state_ref, d, reverse):
    c = CHUNK
    nsub = c // SUB
    ri = lax.broadcasted_iota(jnp.int32, (c, c), 0)
    ci = lax.broadcasted_iota(jnp.int32, (c, c), 1)
    incl = (ci >= ri) if reverse else (ri >= ci)
    gc = _hdot(incl.astype(F32), gk)
    last = 0 if reverse else c - 1
    gl = gc[last:last + 1, :]
    qd = qs * jnp.exp2(gc)
    kt = k * jnp.exp2(gl - gc)
    egl = jnp.exp2(gl)
    row = lax.broadcasted_iota(jnp.int32, (c, GLA_QK), 0)
    yield

    head_of_lane = lax.broadcasted_iota(jnp.int32, (SUB, GLA_QK), 1) // GLA_DK
    a_off = [[] for _ in range(GLA_HEADS)]
    for i in range(nsub):
        yield
        if i == (nsub - 1 if reverse else 0):
            for h in range(GLA_HEADS):
                a_off[h].append(jnp.zeros((SUB, c), F32))
            continue
        first = (i + 1) * SUB - 1 if reverse else i * SUB
        rs = slice(i * SUB, (i + 1) * SUB)
        ref = gc[first:first + 1, :]
        qi = qs[rs, :] * jnp.exp2(gc[rs, :] - ref)
        early = (row > first) if reverse else (row < first)
        kf = k * jnp.exp2(jnp.where(early, ref - gc, MASKED_LOG))
        stacked = jnp.concatenate([jnp.where(head_of_lane == h, qi, 0.0) for h in range(GLA_HEADS)], axis=0)
        prod = _bdot_nt(stacked, kf)
        for h in range(GLA_HEADS):
            a_off[h].append(prod[h * SUB:(h + 1) * SUB])

    lane = lax.broadcasted_iota(jnp.int32, (LANES, 2 * GLA_DV), 0)
    col = lax.broadcasted_iota(jnp.int32, (LANES, 2 * GLA_DV), 1)
    head_sum = ((lane // GLA_DK) == (col // GLA_DV)).astype(BF16)
    sub_row = lax.broadcasted_iota(jnp.int32, (SUB, LANES), 0)
    npair = GLA_HEADS // 2
    terms = []
    for p in range(npair):
        ls = slice(p * LANES, (p + 1) * LANES)
        for i in range(nsub):
            yield
            rs = slice(i * SUB, (i + 1) * SUB)
            gci, qsi = gc[rs, ls], qs[rs, ls]
            for jj in range(SUB):
                j = i * SUB + jj
                later = (sub_row <= jj) if reverse else (sub_row >= jj)
                e = jnp.exp2(jnp.where(later, gci - gc[j:j + 1, ls], MASKED_LOG))
                terms.append(qsi * k[j:j + 1, ls] * e)
    yield
    r = jnp.dot(jnp.concatenate(terms, axis=0).astype(BF16), head_sum, preferred_element_type=F32)
    o_diag = []
    for p in range(npair):
        yield
        vs = slice(p * 2 * GLA_DV, (p + 1) * 2 * GLA_DV)
        blocks = []
        for i in range(nsub):
            base = (p * nsub + i) * SUB * SUB
            acc = r[base:base + SUB, :] * v[i * SUB:i * SUB + 1, vs]
            for jj in range(1, SUB):
                acc = acc + r[base + jj * SUB:base + (jj + 1) * SUB, :] * v[i * SUB + jj:i * SUB + jj + 1, vs]
            blocks.append(acc)
        o_diag.append(jnp.concatenate(blocks, axis=0))

    outs = []
    for h in range(GLA_HEADS):
        yield
        sl = slice(h * GLA_DK, (h + 1) * GLA_DK)
        vh = v[:, h * GLA_DV:(h + 1) * GLA_DV]
        st = state_ref[d, h]
        a = jnp.concatenate(a_off[h], axis=0)
        od = o_diag[h // 2][:, (h % 2) * GLA_DV:(h % 2 + 1) * GLA_DV]
        outs.append(_bdot_nt(qd[:, sl], st) + _bdot(a, vh) + od)
        state_ref[d, h] = st * egl[:, sl] + lax.dot_general(
            vh.astype(BF16), kt[:, sl].astype(BF16), (((0,), (0,)), ((), ())), preferred_element_type=F32)
    return jnp.concatenate(outs, axis=1)


def _gla_scan_body(qf_ref, kf_ref, vf_ref, smf_ref, qb_ref, kb_ref, vb_ref, smb_ref, up_ref, upb_ref,
                   of_ref, ob_ref, state_ref, *, nc):
    n = pl.program_id(1)

    @pl.when(n == 0)
    def _():
        state_ref[...] = jnp.zeros_like(state_ref)

    ins = ((qf_ref, kf_ref, vf_ref, smf_ref, of_ref), (qb_ref, kb_ref, vb_ref, smb_ref, ob_ref))

    together = 2 if nc % 2 == 0 else 1

    def scan_steps(tt, carry):
        running = []
        for t in [tt * together + s for s in range(together)]:
            for d, (q_ref, k_ref, v_ref, sm_ref, o_ref) in enumerate(ins):
                c = t if d == 0 else nc - 1 - t
                rows = pl.ds(pl.multiple_of(c * CHUNK, CHUNK), CHUNK)
                gk = jax.nn.log_sigmoid(_hdot(sm_ref[0, rows, :], up_ref[d]) + upb_ref[d]) * (LOG2E / GLA_NORMALIZER)
                qs = q_ref[0, rows, :].astype(F32) * (GLA_DK ** -0.5)
                gen = _gla_chunk(qs, k_ref[0, rows, :].astype(F32), v_ref[0, rows, :].astype(F32), gk,
                                 state_ref, d, reverse=(d == 1))
                running.append((gen, o_ref, rows))
        while running:
            unfinished = []
            for gen, o_ref, rows in running:
                try:
                    next(gen)
                    unfinished.append((gen, o_ref, rows))
                except StopIteration as done:
                    o_ref[0, rows, :] = done.value.astype(o_ref.dtype)
            running = unfinished
        return carry

    lax.fori_loop(0, nc // together, scan_steps, 0)


def gla_scan(proj, small, up_pad, upb):
    b, s, _ = proj.shape
    nchunk = s // CHUNK
    nc = max(c for c in (4, 2, 1) if nchunk % c == 0)
    nblk = nchunk // nc
    rows = nc * CHUNK

    def specs(d):
        def cidx(n):
            return n if d == 0 else nblk - 1 - n
        return [pl.BlockSpec((1, rows, GLA_QK), lambda bb, n: (bb, cidx(n), OFF_GQ // GLA_QK)),
                pl.BlockSpec((1, rows, GLA_QK), lambda bb, n: (bb, cidx(n), OFF_GK // GLA_QK)),
                pl.BlockSpec((1, rows, GLA_W), lambda bb, n: (bb, cidx(n), OFF_GV // GLA_W)),
                pl.BlockSpec((1, rows, LANES), lambda bb, n: (bb, cidx(n), 0))]

    out = jax.ShapeDtypeStruct((b, s, GLA_W), BF16)
    return pl.pallas_call(
        functools.partial(_gla_scan_body, nc=nc),
        out_shape=(out, out),
        grid=(b, nblk),
        in_specs=specs(0) + specs(1) + [pl.BlockSpec((2, LANES, GLA_QK), lambda bb, n: (0, 0, 0)),
                                        pl.BlockSpec((2, 1, GLA_QK), lambda bb, n: (0, 0, 0))],
        out_specs=(pl.BlockSpec((1, rows, GLA_W), lambda bb, n: (bb, n, 0)),
                   pl.BlockSpec((1, rows, GLA_W), lambda bb, n: (bb, nblk - 1 - n, 0))),
        scratch_shapes=[pltpu.VMEM((2, GLA_HEADS, GLA_DV, GLA_DK), F32)],
        compiler_params=_cparams(("arbitrary", "arbitrary")),
        name="gla_scan",
    )(proj, proj, proj, small, proj, proj, proj, small, up_pad, upb)


def _gated_norm(o, gate, g, heads, d):
    outs = []
    for h in range(heads):
        sl = slice(h * d, (h + 1) * d)
        t = o[:, sl]
        y = t * lax.rsqrt(jnp.mean(t * t, axis=-1, keepdims=True) + EPS) * g
        gt = gate[:, sl].astype(F32)
        outs.append(y * (gt * jax.nn.sigmoid(gt)))
    return jnp.concatenate(outs, axis=1)


def _merge_body(dnf_ref, dnb_ref, dgate_ref, dng_ref, att_ref, glf_ref, glb_ref, ggate_ref, glg_ref, o_ref):
    o_dn = _gated_norm(dnf_ref[0].astype(F32) + dnb_ref[0].astype(F32), dgate_ref[0], dng_ref[...],
                       DN_HEADS, DN_D)
    o_gla = _gated_norm(glf_ref[0].astype(F32) + glb_ref[0].astype(F32), ggate_ref[0], glg_ref[...],
                        GLA_HEADS, GLA_DV)
    o_ref[0, :, 0:DN_W] = o_dn.astype(BF16)
    o_ref[0, :, DN_W:DN_W + ATT_W] = att_ref[0]
    o_ref[0, :, DN_W + ATT_W:] = o_gla.astype(BF16)


def merge_heads(dn_f, dn_b, proj, dn_g, att, gl_f, gl_b, gla_g, ts=512):
    b, s, _ = proj.shape
    ts = _tile(s, ts, 16)
    mix_w = DN_W + ATT_W + GLA_W

    def spec(w, blk=0):
        return pl.BlockSpec((1, ts, w), lambda bb, i: (bb, i, blk))

    vec = pl.BlockSpec((1, LANES), lambda bb, i: (0, 0))
    return pl.pallas_call(
        _merge_body,
        out_shape=jax.ShapeDtypeStruct((b, s, mix_w), BF16),
        grid=(b, s // ts),
        in_specs=[spec(DN_W), spec(DN_W), spec(DN_W, OFF_DGATE // DN_W), vec,
                  spec(ATT_W), spec(GLA_W), spec(GLA_W), spec(GLA_W, OFF_GGATE // GLA_W), vec],
        out_specs=spec(mix_w),
        compiler_params=_cparams(("arbitrary", "arbitrary")),
        name="merge_heads",
    )(dn_f, dn_b, proj, dn_g.reshape(1, DN_D), att, gl_f, gl_b, proj, gla_g.reshape(1, GLA_DV))


def _split_points():
    pts, acc = [], 0
    for sz in IN_SIZES[:-1]:
        acc += sz
        pts.append(acc)
    return pts


def _relayout_w_in(w):
    d = w.shape[0]
    (dq, dk, dv, dgate, a_f, a_b, b_f, b_b, aq, ak, av, gq, gkk, gv, ggate, lr_f, lr_b) = jnp.split(
        w, _split_points(), axis=1)
    small = jnp.concatenate([a_f, a_b, b_f, b_b, lr_f, lr_b], axis=1)
    small = jnp.pad(small, ((0, 0), (0, LANES - small.shape[1])))
    cols = jnp.concatenate([dq, dk, dv, dgate, aq, ak, av, gq, gv, ggate, gkk,
                            jnp.zeros((d, PROJ_COLS - OFF_PAD), w.dtype)], axis=1)
    return cols.astype(BF16)[None, None], small.astype(BF16)[None, None]


def _rope_tables(s):
    rows = s // GRID_W
    row = jnp.repeat(jnp.arange(rows, dtype=jnp.int32), GRID_W).astype(F32)
    col = jnp.tile(jnp.arange(GRID_W, dtype=jnp.int32), rows).astype(F32)
    inv_freq = ROPE_THETA ** (-jnp.arange(0, ROPE_SUB, 2, dtype=F32) / ROPE_SUB)
    ang_r = row[:, None] * inv_freq[None, :]
    ang_c = col[:, None] * inv_freq[None, :]
    cos = jnp.concatenate([jnp.cos(ang_r), jnp.cos(ang_r), jnp.cos(ang_c), jnp.cos(ang_c)], axis=1)
    sin = jnp.concatenate([-jnp.sin(ang_r), jnp.sin(ang_r), -jnp.sin(ang_c), jnp.sin(ang_c)], axis=1)
    return cos, sin


def _dn_params(dn_a_log, dn_dt_bias):
    rows = jnp.stack([dn_a_log.reshape(-1), dn_dt_bias.reshape(-1)], axis=0)
    return jnp.pad(rows, ((0, 0), (SM_A, LANES - SM_A - 2 * DN_HEADS)))


def _gla_params(gla_up, gla_up_b):
    ups = []
    for d in range(2):
        lo = SM_LR + d * GLA_RANK
        ups.append(jnp.pad(gla_up[d], ((lo, LANES - lo - GLA_RANK), (0, 0))))
    return jnp.stack(ups, axis=0), gla_up_b[:, None, :]


def _mixer(x16, bsz, s, w_in, dn_conv, dn_a_log, dn_dt_bias, dn_norm_g, att_qn_g, att_kn_g,
           gla_up, gla_up_b, gla_norm_g, rope):
    t = bsz * s
    bm = _tile(t, 1024, 16)
    one_group = _block_table(jnp.zeros((t // bm,), jnp.int32))
    w_main, w_small = _relayout_w_in(w_in)
    proj = grouped_matmul(x16, w_main, 0, one_group, bm, 1024, BF16).reshape(bsz, s, PROJ_COLS)
    small = grouped_matmul(x16, w_small, 0, one_group, bm, LANES, F32).reshape(bsz, s, LANES)

    q, k, v = dn_prep(proj, dn_conv)
    u, wq, ktt, qk, egl = dn_wy(q, k, v, small, _dn_params(dn_a_log, dn_dt_bias))
    dn_f, dn_b = dn_rec(u, wq, ktt, qk, egl)

    cos, sin_signed = rope
    aq, ak, av, kn = att_prep(proj, cos, sin_signed, att_qn_g, att_kn_g)
    o_att = flash_attention(aq, ak, av, kn)

    gl_f, gl_b = gla_scan(proj, small, *_gla_params(gla_up, gla_up_b))

    return merge_heads(dn_f, dn_b, proj, dn_norm_g, o_att, gl_f, gl_b, gla_norm_g).reshape(t, -1)


def _moe(x32, x16, router_w, w_gate, w_up, w_down, layer, ln_g, ln_b, alpha):
    t, d = x32.shape
    rw = jnp.pad(router_w, ((0, 0), (0, LANES - N_EXPERTS)))
    logits = router_logits(x32, rw)[:, :N_EXPERTS]
    top_val, top_idx = lax.top_k(logits, TOP_K)
    gates = jax.nn.softmax(top_val, axis=-1)
    e_flat = top_idx.reshape(-1).astype(jnp.int32)
    tok_flat = jnp.repeat(jnp.arange(t, dtype=jnp.int32), TOP_K)
    onehot = (e_flat[:, None] == jnp.arange(N_EXPERTS, dtype=jnp.int32)[None, :]).astype(jnp.int32)
    csum = jnp.cumsum(onehot, axis=0)
    counts = csum[-1]
    rank = jnp.take_along_axis(csum, e_flat[:, None], axis=1)[:, 0] - 1
    padded = (counts + MOE_BLOCK - 1) // MOE_BLOCK * MOE_BLOCK
    pstart = jnp.cumsum(padded) - padded
    pend = pstart + padded
    dest = pstart[e_flat] + rank
    nb = -(-(TOP_K * t) // MOE_BLOCK) + N_EXPERTS
    cap = nb * MOE_BLOCK
    buf_tok = jnp.zeros((cap,), jnp.int32).at[dest].set(tok_flat)
    block_start = jnp.arange(nb, dtype=jnp.int32) * MOE_BLOCK
    block_exp = jnp.minimum(jnp.sum(block_start[:, None] >= pend[None, :], axis=-1), N_EXPERTS - 1)
    table = _block_table(block_exp, block_start < pend[-1])
    xb = x16[buf_tok]
    h = grouped_swiglu(xb, w_gate, w_up, layer, table, MOE_BLOCK, 1792, slots=1)
    yb = grouped_matmul(h, w_down, layer, table, MOE_BLOCK, 1024, BF16, slots=1)
    dest_by_choice = dest.reshape(t, TOP_K).T.reshape(-1)
    return combine_ln(x32, yb[dest_by_choice], gates, ln_g, ln_b, alpha)


def _dense_ffn(x32, x16, w_gate, w_up, w_down, layer, ln_g, ln_b, alpha):
    t = x32.shape[0]
    bm = _tile(t, 1024, 16)
    h = grouped_swiglu(x16, w_gate, w_up, layer, _block_table(jnp.zeros((t // bm,), jnp.int32)), bm, 512)
    return matmul_res_ln(h, w_down[layer].astype(BF16), x32, ln_g, ln_b, alpha)


def kernel(x, w_in, dn_conv, dn_a_log, dn_dt_bias, dn_norm_g, att_qn_g, att_kn_g, gla_up, gla_up_b,
           gla_norm_g, w_out, ln1_g, ln1_b, ln2_g, ln2_b, ffn_w_gate, ffn_w_up, ffn_w_down, router_w,
           exp_w_gate, exp_w_up, exp_w_down):
    bsz, s, d = x.shape
    depth = w_in.shape[0]
    alpha = (2.0 * depth) ** 0.25
    t = bsz * s
    rope = _rope_tables(s)
    x32 = x.reshape(t, d)
    x16 = x32.astype(BF16)
    for layer in range(depth):
        mix = _mixer(x16, bsz, s, w_in[layer], dn_conv[layer], dn_a_log[layer], dn_dt_bias[layer],
                     dn_norm_g[layer], att_qn_g[layer], att_kn_g[layer], gla_up[layer],
                     gla_up_b[layer], gla_norm_g[layer], rope)
        x32, x16 = matmul_res_ln(mix, w_out[layer].astype(BF16), x32, ln1_g[layer], ln1_b[layer], alpha)
        j = layer // 2
        if layer % 2 == 0:
            x32, x16 = _dense_ffn(x32, x16, ffn_w_gate, ffn_w_up, ffn_w_down, j,
                                  ln2_g[layer], ln2_b[layer], alpha)
        else:
            x32, x16 = _moe(x32, x16, router_w[j], exp_w_gate, exp_w_up, exp_w_down, j,
                            ln2_g[layer], ln2_b[layer], alpha)
    return x32.reshape(bsz, s, d)
```

```python
import functools
import math

import jax
import jax.numpy as jnp
from jax import lax
from jax.experimental import pallas as pl
from jax.experimental.pallas import tpu as pltpu

F32 = jnp.float32
BF16 = jnp.bfloat16
HIGHEST = lax.Precision.HIGHEST

DN_HEADS, DN_D = 6, 128
ATT_HEADS, ATT_KV_HEADS, ATT_DH = 6, 2, 128
ATT_GROUP = ATT_HEADS // ATT_KV_HEADS
ROPE_SUB, ROPE_THETA, GRID_W = 64, 10000.0, 64
GLA_HEADS, GLA_DK, GLA_DV, GLA_RANK = 4, 64, 128, 16
GLA_NORMALIZER = 16.0
CHUNK = 64
SUB = 8
MASKED_LOG = -1e30
WY_ROWS = 2 * CHUNK
EGL_ROWS = 8
CONV_W = 5
N_EXPERTS, TOP_K, MOE_BLOCK = 8, 2, 256
EPS = 1e-6
LOG2E = 1.4426950408889634
LANES = 128
HALO = 16

DN_W = DN_HEADS * DN_D
ATT_W = ATT_HEADS * ATT_DH
ATT_KV_W = ATT_KV_HEADS * ATT_DH
GLA_QK = GLA_HEADS * GLA_DK
GLA_W = GLA_HEADS * GLA_DV
IN_SIZES = (DN_W, DN_W, DN_W, DN_W, DN_HEADS, DN_HEADS, DN_HEADS, DN_HEADS,
            ATT_W, ATT_KV_W, ATT_KV_W, GLA_QK, GLA_QK, GLA_W, GLA_W, GLA_RANK, GLA_RANK)
OFF_DQ, OFF_DGATE, OFF_AQ, OFF_AK, OFF_AV = 0, 2304, 3072, 3840, 4096
OFF_GQ, OFF_GV, OFF_GGATE, OFF_GK, OFF_PAD = 4352, 4608, 5120, 5632, 5888
PROJ_COLS = 6144
SM_A, SM_B, SM_LR = 0, 2 * DN_HEADS, 4 * DN_HEADS

VMEM_LIMIT = 56 * 1024 * 1024


def _cparams(sem, vmem=VMEM_LIMIT):
    return pltpu.CompilerParams(dimension_semantics=sem, vmem_limit_bytes=vmem)


def _tile(n, pref, quantum=LANES):
    if n <= pref:
        return n
    t = pref - pref % quantum
    while n % t:
        t -= quantum
    return t


def _bdot(a, b):
    return jnp.dot(a.astype(BF16), b.astype(BF16), preferred_element_type=F32)


def _bdot_nt(a, b):
    return lax.dot_general(a.astype(BF16), b.astype(BF16), (((1,), (1,)), ((), ())),
                           preferred_element_type=F32)


def _hdot(a, b):
    return jnp.dot(a, b, preferred_element_type=F32, precision=HIGHEST)


TB_GROUP, TB_USED, TB_FIRST, TB_SEG, TB_NEXT, TB_LAST = range(6)


def _block_table(group, used=None):
    group = group.astype(jnp.int32)
    used = jnp.ones_like(group) if used is None else used.astype(jnp.int32)
    first = jnp.concatenate([jnp.ones((1,), jnp.int32), (group[1:] != group[:-1]).astype(jnp.int32)])
    seg = jnp.cumsum(first) - 1
    nseg = seg[-1] + 1
    run_group = jnp.zeros_like(group).at[seg].set(group)
    nxt = run_group[jnp.where(seg + 1 < nseg, seg + 1, 0)]
    last = (seg == nseg - 1).astype(jnp.int32)
    return jnp.stack([group, used, first, seg, nxt, last], axis=0)


def _if_used(be_ref, o_ref, compute):
    used = be_ref[TB_USED, pl.program_id(1)] != 0

    @pl.when(used)
    def _():
        compute()

    @pl.when(jnp.logical_not(used))
    def _():
        o_ref[...] = jnp.zeros_like(o_ref)


def _stream_weights(be_ref, w_hbm, wbuf, w16, sem, layer, tn):
    j, i = pl.program_id(0), pl.program_id(1)
    nj, nb = pl.num_programs(0), pl.num_programs(1)
    nseg = be_ref[TB_SEG, nb - 1] + 1
    nslot = wbuf[0].shape[0]
    run = j * nseg + be_ref[TB_SEG, i]
    slot = lax.rem(run, nslot)

    def copies(group, col, to_slot):
        start = pl.multiple_of(col * tn, tn)
        out = []
        for m, (w, buf) in enumerate(zip(w_hbm, wbuf)):
            src = w.at[layer, :, pl.ds(start, tn)] if len(w.shape) == 3 else w.at[layer, group, :, pl.ds(start, tn)]
            out.append(pltpu.make_async_copy(src, buf.at[to_slot], sem.at[m, to_slot]))
        return out

    @pl.when((j == 0) & (i == 0))
    def _():
        for c in copies(be_ref[TB_GROUP, i], j, slot):
            c.start()

    @pl.when(be_ref[TB_FIRST, i] != 0)
    def _():
        for c in copies(be_ref[TB_GROUP, i], j, slot):
            c.wait()
        for m in range(len(w_hbm)):
            w16[m][...] = wbuf[m][slot].astype(BF16)
        last = be_ref[TB_LAST, i] != 0

        @pl.when(jnp.logical_not(last) | (j + 1 < nj))
        def _():
            for c in copies(be_ref[TB_NEXT, i], jnp.where(last, j + 1, j), lax.rem(run + 1, nslot)):
                c.start()


def _gmm_body(be_ref, x_ref, w_ref, o_ref, *scratch, layer, tn):
    if scratch:
        wbuf, w16, sem = scratch
        _stream_weights(be_ref, [w_ref], [wbuf], [w16], sem, layer, tn)
        w_use = w16
    else:
        w_use = w_ref

    def compute():
        o_ref[...] = jnp.dot(x_ref[...], w_use[...], preferred_element_type=F32).astype(o_ref.dtype)

    _if_used(be_ref, o_ref, compute)


def _stream_scratch(k, tn, n_mats, slots):
    return ([pltpu.VMEM((slots, k, tn), F32) for _ in range(n_mats)]
            + [pltpu.VMEM((k, tn), BF16) for _ in range(n_mats)] + [pltpu.SemaphoreType.DMA((n_mats, slots))])


def grouped_matmul(x, w, layer, block_table, bm, tn, out_dtype, slots=2):
    m, k = x.shape
    n = w.shape[-1]
    tn = _tile(n, tn)
    if w.dtype == BF16:
        w_spec = pl.BlockSpec((None, None, k, tn), lambda j, i, be: (layer, 0, 0, j))
        scratch = []
    else:
        w_spec = pl.BlockSpec(memory_space=pl.ANY)
        scratch = _stream_scratch(k, tn, 1, slots)
    return pl.pallas_call(
        functools.partial(_gmm_body, layer=layer, tn=tn),
        out_shape=jax.ShapeDtypeStruct((m, n), out_dtype),
        grid_spec=pltpu.PrefetchScalarGridSpec(
            num_scalar_prefetch=1, grid=(n // tn, m // bm),
            in_specs=[pl.BlockSpec((bm, k), lambda j, i, be: (i, 0)), w_spec],
            out_specs=pl.BlockSpec((bm, tn), lambda j, i, be: (i, j)),
            scratch_shapes=scratch),
        compiler_params=_cparams(("arbitrary", "arbitrary")),
        name="grouped_matmul",
    )(block_table, x, w)


def _gswiglu_body(be_ref, x_ref, wg_ref, wu_ref, o_ref, wg_buf, wu_buf, wg16, wu16, sem, *, layer, tn):
    _stream_weights(be_ref, [wg_ref, wu_ref], [wg_buf, wu_buf], [wg16, wu16], sem, layer, tn)

    def compute():
        x = x_ref[...]
        g = jnp.dot(x, wg16[...], preferred_element_type=F32)
        u = jnp.dot(x, wu16[...], preferred_element_type=F32)
        o_ref[...] = (g * jax.nn.sigmoid(g) * u).astype(o_ref.dtype)

    _if_used(be_ref, o_ref, compute)


def grouped_swiglu(x, wg, wu, layer, block_table, bm, tn, slots=2):
    m, k = x.shape
    n = wg.shape[-1]
    tn = _tile(n, tn)
    hbm = pl.BlockSpec(memory_space=pl.ANY)
    return pl.pallas_call(
        functools.partial(_gswiglu_body, layer=layer, tn=tn),
        out_shape=jax.ShapeDtypeStruct((m, n), BF16),
        grid_spec=pltpu.PrefetchScalarGridSpec(
            num_scalar_prefetch=1, grid=(n // tn, m // bm),
            in_specs=[pl.BlockSpec((bm, k), lambda j, i, be: (i, 0)), hbm, hbm],
            out_specs=pl.BlockSpec((bm, tn), lambda j, i, be: (i, j)),
            scratch_shapes=_stream_scratch(k, tn, 2, slots)),
        compiler_params=_cparams(("arbitrary", "arbitrary")),
        name="grouped_swiglu",
    )(block_table, x, wg, wu)


def _res_ln(res, y, g, b, alpha):
    z = alpha * res + y
    mu = jnp.mean(z, axis=-1, keepdims=True)
    zc = z - mu
    var = jnp.mean(zc * zc, axis=-1, keepdims=True)
    return zc * lax.rsqrt(var + EPS) * g + b


def _mm_res_ln_body(x_ref, w_ref, res_ref, g_ref, b_ref, o32_ref, o16_ref, *acc, nk, alpha):
    kk = pl.program_id(1)
    part = jnp.dot(x_ref[...], w_ref[...], preferred_element_type=F32)

    def finish(y):
        out = _res_ln(res_ref[...], y, g_ref[...], b_ref[...], alpha)
        o32_ref[...] = out
        o16_ref[...] = out.astype(BF16)

    if nk == 1:
        finish(part)
        return
    acc_ref = acc[0]

    @pl.when(kk == 0)
    def _():
        acc_ref[...] = part

    @pl.when((kk > 0) & (kk < nk - 1))
    def _():
        acc_ref[...] += part

    @pl.when(kk == nk - 1)
    def _():
        finish(acc_ref[...] + part)


def matmul_res_ln(x, w, res, g, b, alpha, tm=512, tk=2048):
    m, k = x.shape
    n = w.shape[-1]
    tm = _tile(m, tm, 8)
    tk = _tile(k, tk)
    nk = k // tk
    return pl.pallas_call(
        functools.partial(_mm_res_ln_body, nk=nk, alpha=alpha),
        out_shape=(jax.ShapeDtypeStruct((m, n), F32), jax.ShapeDtypeStruct((m, n), BF16)),
        grid=(m // tm, nk),
        in_specs=[pl.BlockSpec((tm, tk), lambda i, kk: (i, kk)),
                  pl.BlockSpec((tk, n), lambda i, kk: (kk, 0)),
                  pl.BlockSpec((tm, n), lambda i, kk: (i, 0)),
                  pl.BlockSpec((1, n), lambda i, kk: (0, 0)),
                  pl.BlockSpec((1, n), lambda i, kk: (0, 0))],
        out_specs=(pl.BlockSpec((tm, n), lambda i, kk: (i, 0)),
                   pl.BlockSpec((tm, n), lambda i, kk: (i, 0))),
        scratch_shapes=[pltpu.VMEM((tm, n), F32)] if nk > 1 else [],
        compiler_params=_cparams(("arbitrary", "arbitrary")),
        name="matmul_res_ln",
    )(x, w, res, g.reshape(1, n), b.reshape(1, n))


def _combine_ln_body(res_ref, *refs, alpha):
    y_refs, (gate_ref, g_ref, b_ref, o32_ref, o16_ref) = refs[:TOP_K], refs[TOP_K:]
    gate = gate_ref[...]
    y = y_refs[0][...].astype(F32) * gate[:, 0:1]
    for kk in range(1, TOP_K):
        y = y + y_refs[kk][...].astype(F32) * gate[:, kk:kk + 1]
    out = _res_ln(res_ref[...], y, g_ref[...], b_ref[...], alpha)
    o32_ref[...] = out
    o16_ref[...] = out.astype(BF16)


def combine_ln(res, y, gates, g, b, alpha, tm=512):
    m, n = res.shape
    tm = _tile(m, tm, 16)
    nblk = m // tm
    row = pl.BlockSpec((tm, n), lambda i: (i, 0))
    vec = pl.BlockSpec((1, n), lambda i: (0, 0))

    def choice(kk):
        return pl.BlockSpec((tm, n), lambda i: (kk * nblk + i, 0))

    return pl.pallas_call(
        functools.partial(_combine_ln_body, alpha=alpha),
        out_shape=(jax.ShapeDtypeStruct((m, n), F32), jax.ShapeDtypeStruct((m, n), BF16)),
        grid=(nblk,),
        in_specs=[row] + [choice(kk) for kk in range(TOP_K)] + [pl.BlockSpec((tm, TOP_K), lambda i: (i, 0)), vec, vec],
        out_specs=(row, row),
        compiler_params=_cparams(("arbitrary",)),
        name="combine_ln",
    )(res, *([y] * TOP_K), gates, g.reshape(1, n), b.reshape(1, n))


def _router_body(x_ref, w_ref, o_ref):
    o_ref[...] = _hdot(x_ref[...], w_ref[...])


def router_logits(x, w_pad, tm=512):
    m, k = x.shape
    n = w_pad.shape[-1]
    tm = _tile(m, tm, 8)
    return pl.pallas_call(
        _router_body,
        out_shape=jax.ShapeDtypeStruct((m, n), F32),
        grid=(m // tm,),
        in_specs=[pl.BlockSpec((tm, k), lambda i: (i, 0)), pl.BlockSpec((k, n), lambda i: (0, 0))],
        out_specs=pl.BlockSpec((tm, n), lambda i: (i, 0)),
        compiler_params=_cparams(("arbitrary",)),
        name="router_logits",
    )(x, w_pad)


def _dn_prep_body(cur_ref, prev_ref, next_ref, w_ref, q_ref, k_ref, v_ref, buf_ref, *, ts, nblk, sub):
    i = pl.program_id(1)
    buf_ref[0:HALO, :] = jnp.where(i > 0, prev_ref[0], jnp.zeros_like(prev_ref[0]))
    buf_ref[HALO:HALO + ts, :] = cur_ref[0]
    buf_ref[HALO + ts:2 * HALO + ts, :] = jnp.where(i < nblk - 1, next_ref[0], jnp.zeros_like(next_ref[0]))
    win = sub + 2 * HALO
    ri = lax.broadcasted_iota(jnp.int32, (sub, win), 0)
    ci = lax.broadcasted_iota(jnp.int32, (sub, win), 1)
    centre = CONV_W // 2
    shift = {j: (ci == ri + HALO + j - centre).astype(BF16) for j in range(CONV_W) if j != centre}
    for r in range(ts // sub):
        window = buf_ref[r * sub:r * sub + win, :]
        acc = window[HALO:HALO + sub, :].astype(F32) * w_ref[centre:centre + 1, :]
        for j, sel in shift.items():
            acc = acc + jnp.dot(sel, window, preferred_element_type=F32) * w_ref[j:j + 1, :]
        y = acc * jax.nn.sigmoid(acc)
        rows = slice(r * sub, (r + 1) * sub)
        for h in range(DN_HEADS):
            for off, ref in ((0, q_ref), (DN_W, k_ref)):
                t = y[:, off + h * DN_D:off + (h + 1) * DN_D]
                ref[0, rows, h * DN_D:(h + 1) * DN_D] = t * lax.rsqrt(jnp.sum(t * t, axis=-1, keepdims=True) + EPS)
        v_ref[0, rows, :] = y[:, 2 * DN_W:3 * DN_W]


def dn_prep(proj, conv_w, ts=512):
    b, s, _ = proj.shape
    ts = _tile(s, ts, HALO)
    sub = _tile(ts, LANES, HALO)
    nblk = s // ts
    c = 3 * DN_W
    hb = ts // HALO
    out = jax.ShapeDtypeStruct((b, s, DN_W), F32)
    ospec = pl.BlockSpec((1, ts, DN_W), lambda bb, i: (bb, i, 0))
    return pl.pallas_call(
        functools.partial(_dn_prep_body, ts=ts, nblk=nblk, sub=sub),
        out_shape=(out, out, out),
        grid=(b, nblk),
        in_specs=[pl.BlockSpec((1, ts, c), lambda bb, i: (bb, i, 0)),
                  pl.BlockSpec((1, HALO, c), lambda bb, i: (bb, jnp.maximum(i * hb - 1, 0), 0)),
                  pl.BlockSpec((1, HALO, c), lambda bb, i: (bb, jnp.minimum((i + 1) * hb, nblk * hb - 1), 0)),
                  pl.BlockSpec((CONV_W, c), lambda bb, i: (0, 0))],
        out_specs=(ospec, ospec, ospec),
        scratch_shapes=[pltpu.VMEM((ts + 2 * HALO, c), BF16)],
        compiler_params=_cparams(("arbitrary", "arbitrary")),
        name="dn_prep",
    )(proj, proj, proj, conv_w)


def _dn_wy_body(q_ref, k_ref, v_ref, sm_ref, par_ref, u_ref, wq_ref, ktt_ref, qk_ref, egl_ref, *, tiles):
    def tile(t, carry):
        _dn_wy_tile(t, q_ref, k_ref, v_ref, sm_ref, par_ref, u_ref, wq_ref, ktt_ref, qk_ref, egl_ref)
        return carry

    lax.fori_loop(0, tiles, tile, 0)


def _dn_wy_tile(t, q_ref, k_ref, v_ref, sm_ref, par_ref, u_ref, wq_ref, ktt_ref, qk_ref, egl_ref):
    r = WY_ROWS
    nck = r // CHUNK
    rows_t = pl.ds(pl.multiple_of(t * r, r), r)
    sm = sm_ref[0, rows_t, :]
    par = par_ref[...]
    g_all = -jnp.exp(par[0:1, :]) * jax.nn.softplus(sm + par[1:2, :])
    beta_all = jax.nn.sigmoid(sm)
    ri = lax.broadcasted_iota(jnp.int32, (r, r), 0)
    ci = lax.broadcasted_iota(jnp.int32, (r, r), 1)
    same = (ri // CHUNK) == (ci // CHUNK)
    eye = (ri == ci).astype(F32)
    masks = []
    for d in range(2):
        delta = ri - ci if d == 0 else ci - ri
        masks.append((same & (delta >= 0), same & (delta > 0)))
    scale = DN_D ** -0.5
    gtot = _hdot(same.astype(F32), g_all)
    gcs = []
    for d in range(2):
        gc = _hdot(masks[d][0].astype(F32), g_all)
        gcs.append((gc, gc.T))
    chains = [(h, d) for h in range(DN_HEADS) for d in range(2)]
    qk_cols = ([], [])
    nm, tinv, rhs = {}, {}, {}
    for h in range(DN_HEADS):
        sl = slice(h * DN_D, (h + 1) * DN_D)
        q = q_ref[0, rows_t, sl] * scale
        k = k_ref[0, rows_t, sl]
        v = v_ref[0, rows_t, sl]
        kq = _bdot_nt(jnp.concatenate([k, q], axis=0), k)
        kk, qk = kq[:r], kq[r:]
        for d in range(2):
            lane = SM_A + d * DN_HEADS + h
            incl, strict = masks[d]
            gc, gct = gcs[d]
            gcol = jnp.broadcast_to(gc[:, lane:lane + 1], (r, r))
            gt = jnp.broadcast_to(gtot[:, lane:lane + 1], (r, r))
            bcol = jnp.broadcast_to(beta_all[:, SM_B + d * DN_HEADS + h:SM_B + d * DN_HEADS + h + 1], (r, r))
            grow = gct[lane:lane + 1, :]
            decay = jnp.where(incl, jnp.exp(jnp.where(incl, gcol - grow, 0.0)), 0.0)
            nm[h, d] = jnp.where(strict, -(bcol * kk) * decay, 0.0)
            tinv[h, d] = eye + nm[h, d]
            eg = jnp.exp(gcol)
            rhs[h, d] = jnp.concatenate([v * bcol, k * (bcol * eg)], axis=1).astype(BF16)
            qd16 = (q * eg).astype(BF16)
            ktt = _bdot_nt(eye, (k * jnp.exp(gt - gcol)).astype(BF16)).astype(BF16)
            for c in range(nck):
                rows = slice(c * CHUNK, (c + 1) * CHUNK)
                wq_ref[d, 0, t * nck + c, CHUNK:2 * CHUNK, sl] = qd16[rows]
                ktt_ref[d, 0, t * nck + c, :, h * CHUNK:(h + 1) * CHUNK] = ktt[:, rows]
            qkd = qk * decay
            qk_cols[d].append(qkd[:, :CHUNK] + qkd[:, CHUNK:])
            egt = jnp.exp(gt)
            for c in range(nck):
                egl_ref[d, 0, t * nck + c, :, sl] = egt[c * CHUNK:c * CHUNK + EGL_ROWS]
    for d in range(2):
        qk_ref[d, 0, rows_t, :] = jnp.concatenate(qk_cols[d], axis=1).astype(BF16)
    for _ in range(int(math.log2(CHUNK)) - 1):
        for hd in chains:
            nm[hd] = _bdot(nm[hd], nm[hd])
        for hd in chains:
            tinv[hd] = tinv[hd] + _bdot(tinv[hd], nm[hd])
    for h, d in chains:
        sl = slice(h * DN_D, (h + 1) * DN_D)
        uw = _bdot(tinv[h, d], rhs[h, d])
        u_ref[d, 0, rows_t, sl] = uw[:, :DN_D].astype(BF16)
        w16 = uw[:, DN_D:].astype(BF16)
        for c in range(nck):
            wq_ref[d, 0, t * nck + c, 0:CHUNK, sl] = w16[c * CHUNK:(c + 1) * CHUNK]


def dn_wy(q, k, v, small, par):
    b, s, w = q.shape
    nchunk = s // CHUNK
    tiles = max(c for c in (4, 2, 1) if s % (c * WY_ROWS) == 0)
    r = tiles * WY_ROWS
    nck = r // CHUNK
    seq = pl.BlockSpec((1, r, w), lambda bb, i: (bb, i, 0))
    dseq = pl.BlockSpec((2, 1, r, w), lambda bb, i: (0, bb, i, 0))
    return pl.pallas_call(
        functools.partial(_dn_wy_body, tiles=tiles),
        out_shape=(jax.ShapeDtypeStruct((2, b, s, w), BF16),
                   jax.ShapeDtypeStruct((2, b, nchunk, 2 * CHUNK, w), BF16),
                   jax.ShapeDtypeStruct((2, b, nchunk, DN_D, DN_HEADS * CHUNK), BF16),
                   jax.ShapeDtypeStruct((2, b, s, DN_HEADS * CHUNK), BF16),
                   jax.ShapeDtypeStruct((2, b, nchunk, EGL_ROWS, w), F32)),
        grid=(b, s // r),
        in_specs=[seq, seq, seq,
                  pl.BlockSpec((1, r, LANES), lambda bb, i: (bb, i, 0)),
                  pl.BlockSpec((2, LANES), lambda bb, i: (0, 0))],
        out_specs=(dseq,
                   pl.BlockSpec((2, 1, nck, 2 * CHUNK, w), lambda bb, i: (0, bb, i, 0, 0)),
                   pl.BlockSpec((2, 1, nck, DN_D, DN_HEADS * CHUNK), lambda bb, i: (0, bb, i, 0, 0)),
                   pl.BlockSpec((2, 1, r, DN_HEADS * CHUNK), lambda bb, i: (0, bb, i, 0)),
                   pl.BlockSpec((2, 1, nck, EGL_ROWS, w), lambda bb, i: (0, bb, i, 0, 0))),
        compiler_params=_cparams(("arbitrary", "arbitrary")),
        name="dn_wy",
    )(q, k, v, small, par)


def _dn_rec_body(*refs, nc):
    ins = (refs[0:5], refs[5:10])
    outs = refs[10:12]
    state_ref = refs[12]
    n = pl.program_id(1)

    @pl.when(n == 0)
    def _():
        state_ref[...] = jnp.zeros_like(state_ref)

    chains = [(d, h) for d in range(2) for h in range(DN_HEADS)]
    state = {(d, h): state_ref[d, h] for d, h in chains}
    for step in range(nc):
        chunk = (step, nc - 1 - step)
        wq, v_new = {}, {}
        for d, h in chains:
            sl = slice(h * DN_D, (h + 1) * DN_D)
            wq[d, h] = jnp.dot(ins[d][1][0, 0, chunk[d], :, sl], state[d, h].astype(BF16),
                               preferred_element_type=F32)
        for d, h in chains:
            sl = slice(h * DN_D, (h + 1) * DN_D)
            rows = slice(chunk[d] * CHUNK, (chunk[d] + 1) * CHUNK)
            v_new[d, h] = (ins[d][0][0, 0, rows, sl].astype(F32) - wq[d, h][:CHUNK]).astype(BF16)
        for d, h in chains:
            sl = slice(h * DN_D, (h + 1) * DN_D)
            rows = slice(chunk[d] * CHUNK, (chunk[d] + 1) * CHUNK)
            qk = ins[d][3][0, 0, rows, h * CHUNK:(h + 1) * CHUNK]
            outs[d][0, rows, sl] = (wq[d, h][CHUNK:]
                                    + jnp.dot(qk, v_new[d, h], preferred_element_type=F32)).astype(outs[d].dtype)
        for d, h in chains:
            sl = slice(h * DN_D, (h + 1) * DN_D)
            ktt = ins[d][2][0, 0, chunk[d], :, h * CHUNK:(h + 1) * CHUNK]
            state[d, h] = (state[d, h] * ins[d][4][0, 0, chunk[d], 0:1, sl]
                           + jnp.dot(ktt, v_new[d, h], preferred_element_type=F32))
    for d, h in chains:
        state_ref[d, h] = state[d, h]


def dn_rec(u, wq, ktt, qk, egl):
    _, b, s, w = u.shape
    nchunk = s // CHUNK
    nc = max(c for c in (4, 2, 1) if nchunk % c == 0)
    nblk = nchunk // nc
    rows = nc * CHUNK

    def specs(d):
        def blk(n):
            return n if d == 0 else nblk - 1 - n
        return [pl.BlockSpec((1, 1, rows, w), lambda bb, n: (d, bb, blk(n), 0)),
                pl.BlockSpec((1, 1, nc, 2 * CHUNK, w), lambda bb, n: (d, bb, blk(n), 0, 0)),
                pl.BlockSpec((1, 1, nc, DN_D, DN_HEADS * CHUNK), lambda bb, n: (d, bb, blk(n), 0, 0)),
                pl.BlockSpec((1, 1, rows, DN_HEADS * CHUNK), lambda bb, n: (d, bb, blk(n), 0)),
                pl.BlockSpec((1, 1, nc, EGL_ROWS, w), lambda bb, n: (d, bb, blk(n), 0, 0))]

    out = jax.ShapeDtypeStruct((b, s, w), BF16)
    return pl.pallas_call(
        functools.partial(_dn_rec_body, nc=nc),
        out_shape=(out, out),
        grid=(b, nblk),
        in_specs=specs(0) + specs(1),
        out_specs=(pl.BlockSpec((1, rows, w), lambda bb, n: (bb, n, 0)),
                   pl.BlockSpec((1, rows, w), lambda bb, n: (bb, nblk - 1 - n, 0))),
        scratch_shapes=[pltpu.VMEM((2, DN_HEADS, DN_D, DN_D), F32)],
        compiler_params=_cparams(("arbitrary", "arbitrary")),
        name="dn_rec",
    )(u, wq, ktt, qk, egl, u, wq, ktt, qk, egl)


def _rope_head(x, g, cos, sin_signed, first_half):
    x = x.astype(F32)
    xf = x * lax.rsqrt(jnp.mean(x * x, axis=-1, keepdims=True) + EPS) * g
    partner = jnp.where(first_half, pltpu.roll(xf, LANES - ROPE_SUB // 2, axis=1),
                        pltpu.roll(xf, ROPE_SUB // 2, axis=1))
    return xf * cos + partner * sin_signed


def _att_prep_body(q_ref, k_ref, v_ref, cos_ref, sin_ref, qg_ref, kg_ref, qo_ref, ko_ref, vo_ref, kn_ref):
    cos = cos_ref[...]
    sin_signed = sin_ref[...]
    lane = lax.broadcasted_iota(jnp.int32, cos.shape, 1)
    first_half = (lane % ROPE_SUB) < ROPE_SUB // 2
    scale = ATT_DH ** -0.5 * LOG2E
    for h in range(ATT_HEADS):
        sl = slice(h * ATT_DH, (h + 1) * ATT_DH)
        qo_ref[0, :, sl] = (_rope_head(q_ref[0, :, sl], qg_ref[...], cos, sin_signed, first_half)
                            * scale).astype(BF16)
    for h in range(ATT_KV_HEADS):
        sl = slice(h * ATT_DH, (h + 1) * ATT_DH)
        k16 = _rope_head(k_ref[0, :, sl], kg_ref[...], cos, sin_signed, first_half).astype(BF16)
        ko_ref[0, :, sl] = k16
        kf = k16.astype(F32)
        n2 = jnp.max(jnp.sum(kf * kf, axis=-1, keepdims=True), axis=0, keepdims=True)
        kn_ref[0, 0, :, sl] = jnp.broadcast_to(n2, (1, ATT_DH))
    for h in range(ATT_KV_HEADS):
        vo_ref[0, :, 2 * h * ATT_DH:(2 * h + 1) * ATT_DH] = v_ref[0, :, h * ATT_DH:(h + 1) * ATT_DH]
        vo_ref[0, :, (2 * h + 1) * ATT_DH:(2 * h + 2) * ATT_DH] = jnp.ones((v_ref.shape[1], ATT_DH), BF16)


def att_prep(proj, cos, sin_signed, qg, kg, ts=512):
    b, s, _ = proj.shape
    ts = _tile(s, ts, 16)
    tab = pl.BlockSpec((ts, ATT_DH), lambda bb, i: (i, 0))
    vec = pl.BlockSpec((1, ATT_DH), lambda bb, i: (0, 0))
    return pl.pallas_call(
        _att_prep_body,
        out_shape=(jax.ShapeDtypeStruct((b, s, ATT_W), BF16),
                   jax.ShapeDtypeStruct((b, s, ATT_KV_W), BF16),
                   jax.ShapeDtypeStruct((b, s, 2 * ATT_KV_W), BF16),
                   jax.ShapeDtypeStruct((b, s // ts, 1, ATT_KV_W), F32)),
        grid=(b, s // ts),
        in_specs=[pl.BlockSpec((1, ts, ATT_W), lambda bb, i: (bb, i, OFF_AQ // ATT_W)),
                  pl.BlockSpec((1, ts, ATT_KV_W), lambda bb, i: (bb, i, OFF_AK // ATT_KV_W)),
                  pl.BlockSpec((1, ts, ATT_KV_W), lambda bb, i: (bb, i, OFF_AV // ATT_KV_W)),
                  tab, tab, vec, vec],
        out_specs=(pl.BlockSpec((1, ts, ATT_W), lambda bb, i: (bb, i, 0)),
                   pl.BlockSpec((1, ts, ATT_KV_W), lambda bb, i: (bb, i, 0)),
                   pl.BlockSpec((1, ts, 2 * ATT_KV_W), lambda bb, i: (bb, i, 0)),
                   pl.BlockSpec((1, 1, 1, ATT_KV_W), lambda bb, i: (bb, i, 0, 0))),
        compiler_params=_cparams(("arbitrary", "arbitrary")),
        name="att_prep",
    )(proj, proj, proj, cos, sin_signed, qg.reshape(1, ATT_DH), kg.reshape(1, ATT_DH))


MAX_FIXED_SHIFT = 60.0


def _flash_body(q_ref, k_ref, v_ref, kn_ref, o_ref, *, tk, nkv):
    tq = q_ref.shape[1]
    heads = range(ATT_GROUP)
    qs = [q_ref[0, :, h * ATT_DH:(h + 1) * ATT_DH] for h in heads]
    unroll = max(u for u in (4, 2, 1) if nkv % u == 0)

    def tiles(t):
        start = pl.multiple_of(t * tk, tk)
        return k_ref[0, pl.ds(start, tk), :], v_ref[0, pl.ds(start, tk), :]

    def scores(h, ks):
        return lax.dot_general(qs[h], ks, (((1,), (1,)), ((), ())), preferred_element_type=F32)

    def finish(accs):
        for h in heads:
            o_ref[0, :, h * ATT_DH:(h + 1) * ATT_DH] = (accs[h][:, :ATT_DH] / accs[h][:, ATT_DH:]).astype(o_ref.dtype)

    kmax = jnp.sqrt(jnp.max(kn_ref[0, :, 0, :], axis=0, keepdims=True))[:, 0:1]
    shifts = []
    for h in heads:
        qf = qs[h].astype(F32)
        shifts.append(jnp.sqrt(jnp.sum(qf * qf, axis=-1, keepdims=True)) * (kmax * 1.001))
    worst = shifts[0]
    for h in heads[1:]:
        worst = jnp.maximum(worst, shifts[h])
    fixed_ok = jnp.max(worst) <= MAX_FIXED_SHIFT

    @pl.when(fixed_ok)
    def _():
        def step(t, accs):
            ks, vs = tiles(t)
            return tuple(accs[h] + jnp.dot(jnp.exp2(scores(h, ks) - shifts[h]).astype(BF16), vs,
                                           preferred_element_type=F32) for h in heads)

        init = tuple(jnp.zeros((tq, 2 * ATT_DH), F32) for _ in heads)
        finish(lax.fori_loop(0, nkv, step, init, unroll=unroll))

    @pl.when(jnp.logical_not(fixed_ok))
    def _():
        def step(t, carry):
            ks, vs = tiles(t)
            new = []
            for h in heads:
                m, acc = carry[h]
                sc = scores(h, ks)
                m_new = jnp.maximum(m, jnp.max(sc, axis=-1, keepdims=True))
                p = jnp.exp2(sc - m_new)
                acc = jnp.exp2(m - m_new) * acc + jnp.dot(p.astype(BF16), vs, preferred_element_type=F32)
                new.append((m_new, acc))
            return tuple(new)

        init = tuple((jnp.full((tq, 1), -1e30, F32), jnp.zeros((tq, 2 * ATT_DH), F32)) for _ in heads)
        res = lax.fori_loop(0, nkv, step, init, unroll=unroll)
        finish([res[h][1] for h in heads])


def flash_attention(q, k, v, kn, tq=1024, tk=512):
    b, s, _ = q.shape
    tq = _tile(s, tq, 16)
    tk = _tile(s, tk, 16)
    gw = ATT_GROUP * ATT_DH
    return pl.pallas_call(
        functools.partial(_flash_body, tk=tk, nkv=s // tk),
        out_shape=jax.ShapeDtypeStruct((b, s, ATT_W), BF16),
        grid=(b, ATT_KV_HEADS, s // tq),
        in_specs=[pl.BlockSpec((1, tq, gw), lambda bb, g, i: (bb, i, g)),
                  pl.BlockSpec((1, s, ATT_DH), lambda bb, g, i: (bb, 0, g)),
                  pl.BlockSpec((1, s, 2 * ATT_DH), lambda bb, g, i: (bb, 0, g)),
                  pl.BlockSpec((1, kn.shape[1], 1, ATT_DH), lambda bb, g, i: (bb, 0, 0, g))],
        out_specs=pl.BlockSpec((1, tq, gw), lambda bb, g, i: (bb, i, g)),
        compiler_params=_cparams(("arbitrary", "arbitrary", "arbitrary")),
        name="flash_attention",
    )(q, k, v, kn)


def _gla_chunk(qs, k, v, gk, state_ref, d, reverse):
    c = CHUNK
    nsub = c // SUB
    ri = lax.broadcasted_iota(jnp.int32, (c, c), 0)
    ci = lax.broadcasted_iota(jnp.int32, (c, c), 1)
    incl = (ci >= ri) if reverse else (ri >= ci)
    gc = _hdot(incl.astype(F32), gk)
    last = 0 if reverse else c - 1
    gl = gc[last:last + 1, :]
    qd = qs * jnp.exp2(gc)
    kt = k * jnp.exp2(gl - gc)
    egl = jnp.exp2(gl)
    row = lax.broadcasted_iota(jnp.int32, (c, GLA_QK), 0)
    yield

    head_of_lane = lax.broadcasted_iota(jnp.int32, (SUB, GLA_QK), 1) // GLA_DK
    a_off = [[] for _ in range(GLA_HEADS)]
    for i in range(nsub):
        yield
        if i == (nsub - 1 if reverse else 0):
            for h in range(GLA_HEADS):
                a_off[h].append(jnp.zeros((SUB, c), F32))
            continue
        first = (i + 1) * SUB - 1 if reverse else i * SUB
        rs = slice(i * SUB, (i + 1) * SUB)
        ref = gc[first:first + 1, :]
        qi = qs[rs, :] * jnp.exp2(gc[rs, :] - ref)
        early = (row > first) if reverse else (row < first)
        kf = k * jnp.exp2(jnp.where(early, ref - gc, MASKED_LOG))
        stacked = jnp.concatenate([jnp.where(head_of_lane == h, qi, 0.0) for h in range(GLA_HEADS)], axis=0)
        prod = _bdot_nt(stacked, kf)
        for h in range(GLA_HEADS):
            a_off[h].append(prod[h * SUB:(h + 1) * SUB])

    lane = lax.broadcasted_iota(jnp.int32, (LANES, 2 * GLA_DV), 0)
    col = lax.broadcasted_iota(jnp.int32, (LANES, 2 * GLA_DV), 1)
    head_sum = ((lane // GLA_DK) == (col // GLA_DV)).astype(BF16)
    sub_row = lax.broadcasted_iota(jnp.int32, (SUB, LANES), 0)
    npair = GLA_HEADS // 2
    terms = []
    for p in range(npair):
        ls = slice(p * LANES, (p + 1) * LANES)
        for i in range(nsub):
            yield
            rs = slice(i * SUB, (i + 1) * SUB)
            gci, qsi = gc[rs, ls], qs[rs, ls]
            for jj in range(SUB):
                j = i * SUB + jj
                later = (sub_row <= jj) if reverse else (sub_row >= jj)
                e = jnp.exp2(jnp.where(later, gci - gc[j:j + 1, ls], MASKED_LOG))
                terms.append(qsi * k[j:j + 1, ls] * e)
    yield
    r = jnp.dot(jnp.concatenate(terms, axis=0).astype(BF16), head_sum, preferred_element_type=F32)
    o_diag = []
    for p in range(npair):
        yield
        vs = slice(p * 2 * GLA_DV, (p + 1) * 2 * GLA_DV)
        blocks = []
        for i in range(nsub):
            base = (p * nsub + i) * SUB * SUB
            acc = r[base:base + SUB, :] * v[i * SUB:i * SUB + 1, vs]
            for jj in range(1, SUB):
                acc = acc + r[base + jj * SUB:base + (jj + 1) * SUB, :] * v[i * SUB + jj:i * SUB + jj + 1, vs]
            blocks.append(acc)
        o_diag.append(jnp.concatenate(blocks, axis=0))

    outs = []
    for h in range(GLA_HEADS):
        yield
        sl = slice(h * GLA_DK, (h + 1) * GLA_DK)
        vh = v[:, h * GLA_DV:(h + 1) * GLA_DV]
        st = state_ref[d, h]
        a = jnp.concatenate(a_off[h], axis=0)
        od = o_diag[h // 2][:, (h % 2) * GLA_DV:(h % 2 + 1) * GLA_DV]
        outs.append(_bdot_nt(qd[:, sl], st) + _bdot(a, vh) + od)
        state_ref[d, h] = st * egl[:, sl] + lax.dot_general(
            vh.astype(BF16), kt[:, sl].astype(BF16), (((0,), (0,)), ((), ())), preferred_element_type=F32)
    return jnp.concatenate(outs, axis=1)


def _gla_scan_body(qf_ref, kf_ref, vf_ref, smf_ref, qb_ref, kb_ref, vb_ref, smb_ref, up_ref, upb_ref,
                   of_ref, ob_ref, state_ref, *, nc):
    n = pl.program_id(1)

    @pl.when(n == 0)
    def _():
        state_ref[...] = jnp.zeros_like(state_ref)

    ins = ((qf_ref, kf_ref, vf_ref, smf_ref, of_ref), (qb_ref, kb_ref, vb_ref, smb_ref, ob_ref))

    together = 2 if nc % 2 == 0 else 1

    def scan_steps(tt, carry):
        running = []
        for t in [tt * together + s for s in range(together)]:
            for d, (q_ref, k_ref, v_ref, sm_ref, o_ref) in enumerate(ins):
                c = t if d == 0 else nc - 1 - t
                rows = pl.ds(pl.multiple_of(c * CHUNK, CHUNK), CHUNK)
                gk = jax.nn.log_sigmoid(_hdot(sm_ref[0, rows, :], up_ref[d]) + upb_ref[d]) * (LOG2E / GLA_NORMALIZER)
                qs = q_ref[0, rows, :].astype(F32) * (GLA_DK ** -0.5)
                gen = _gla_chunk(qs, k_ref[0, rows, :].astype(F32), v_ref[0, rows, :].astype(F32), gk,
                                 state_ref, d, reverse=(d == 1))
                running.append((gen, o_ref, rows))
        while running:
            unfinished = []
            for gen, o_ref, rows in running:
                try:
                    next(gen)
                    unfinished.append((gen, o_ref, rows))
                except StopIteration as done:
                    o_ref[0, rows, :] = done.value.astype(o_ref.dtype)
            running = unfinished
        return carry

    lax.fori_loop(0, nc // together, scan_steps, 0)


def gla_scan(proj, small, up_pad, upb):
    b, s, _ = proj.shape
    nchunk = s // CHUNK
    nc = max(c for c in (4, 2, 1) if nchunk % c == 0)
    nblk = nchunk // nc
    rows = nc * CHUNK

    def specs(d):
        def cidx(n):
            return n if d == 0 else nblk - 1 - n
        return [pl.BlockSpec((1, rows, GLA_QK), lambda bb, n: (bb, cidx(n), OFF_GQ // GLA_QK)),
                pl.BlockSpec((1, rows, GLA_QK), lambda bb, n: (bb, cidx(n), OFF_GK // GLA_QK)),
                pl.BlockSpec((1, rows, GLA_W), lambda bb, n: (bb, cidx(n), OFF_GV // GLA_W)),
                pl.BlockSpec((1, rows, LANES), lambda bb, n: (bb, cidx(n), 0))]

    out = jax.ShapeDtypeStruct((b, s, GLA_W), BF16)
    return pl.pallas_call(
        functools.partial(_gla_scan_body, nc=nc),
        out_shape=(out, out),
        grid=(b, nblk),
        in_specs=specs(0) + specs(1) + [pl.BlockSpec((2, LANES, GLA_QK), lambda bb, n: (0, 0, 0)),
                                        pl.BlockSpec((2, 1, GLA_QK), lambda bb, n: (0, 0, 0))],
        out_specs=(pl.BlockSpec((1, rows, GLA_W), lambda bb, n: (bb, n, 0)),
                   pl.BlockSpec((1, rows, GLA_W), lambda bb, n: (bb, nblk - 1 - n, 0))),
        scratch_shapes=[pltpu.VMEM((2, GLA_HEADS, GLA_DV, GLA_DK), F32)],
        compiler_params=_cparams(("arbitrary", "arbitrary")),
        name="gla_scan",
    )(proj, proj, proj, small, proj, proj, proj, small, up_pad, upb)


def _gated_norm(o, gate, g, heads, d):
    outs = []
    for h in range(heads):
        sl = slice(h * d, (h + 1) * d)
        t = o[:, sl]
        y = t * lax.rsqrt(jnp.mean(t * t, axis=-1, keepdims=True) + EPS) * g
        gt = gate[:, sl].astype(F32)
        outs.append(y * (gt * jax.nn.sigmoid(gt)))
    return jnp.concatenate(outs, axis=1)


def _merge_body(dnf_ref, dnb_ref, dgate_ref, dng_ref, att_ref, glf_ref, glb_ref, ggate_ref, glg_ref, o_ref):
    o_dn = _gated_norm(dnf_ref[0].astype(F32) + dnb_ref[0].astype(F32), dgate_ref[0], dng_ref[...],
                       DN_HEADS, DN_D)
    o_gla = _gated_norm(glf_ref[0].astype(F32) + glb_ref[0].astype(F32), ggate_ref[0], glg_ref[...],
                        GLA_HEADS, GLA_DV)
    o_ref[0, :, 0:DN_W] = o_dn.astype(BF16)
    o_ref[0, :, DN_W:DN_W + ATT_W] = att_ref[0]
    o_ref[0, :, DN_W + ATT_W:] = o_gla.astype(BF16)


def merge_heads(dn_f, dn_b, proj, dn_g, att, gl_f, gl_b, gla_g, ts=512):
    b, s, _ = proj.shape
    ts = _tile(s, ts, 16)
    mix_w = DN_W + ATT_W + GLA_W

    def spec(w, blk=0):
        return pl.BlockSpec((1, ts, w), lambda bb, i: (bb, i, blk))

    vec = pl.BlockSpec((1, LANES), lambda bb, i: (0, 0))
    return pl.pallas_call(
        _merge_body,
        out_shape=jax.ShapeDtypeStruct((b, s, mix_w), BF16),
        grid=(b, s // ts),
        in_specs=[spec(DN_W), spec(DN_W), spec(DN_W, OFF_DGATE // DN_W), vec,
                  spec(ATT_W), spec(GLA_W), spec(GLA_W), spec(GLA_W, OFF_GGATE // GLA_W), vec],
        out_specs=spec(mix_w),
        compiler_params=_cparams(("arbitrary", "arbitrary")),
        name="merge_heads",
    )(dn_f, dn_b, proj, dn_g.reshape(1, DN_D), att, gl_f, gl_b, proj, gla_g.reshape(1, GLA_DV))


def _split_points():
    pts, acc = [], 0
    for sz in IN_SIZES[:-1]:
        acc += sz
        pts.append(acc)
    return pts


def _relayout_w_in(w):
    d = w.shape[0]
    (dq, dk, dv, dgate, a_f, a_b, b_f, b_b, aq, ak, av, gq, gkk, gv, ggate, lr_f, lr_b) = jnp.split(
        w, _split_points(), axis=1)
    small = jnp.concatenate([a_f, a_b, b_f, b_b, lr_f, lr_b], axis=1)
    small = jnp.pad(small, ((0, 0), (0, LANES - small.shape[1])))
    cols = jnp.concatenate([dq, dk, dv, dgate, aq, ak, av, gq, gv, ggate, gkk,
                            jnp.zeros((d, PROJ_COLS - OFF_PAD), w.dtype)], axis=1)
    return cols.astype(BF16)[None, None], small.astype(BF16)[None, None]


def _rope_tables(s):
    rows = s // GRID_W
    row = jnp.repeat(jnp.arange(rows, dtype=jnp.int32), GRID_W).astype(F32)
    col = jnp.tile(jnp.arange(GRID_W, dtype=jnp.int32), rows).astype(F32)
    inv_freq = ROPE_THETA ** (-jnp.arange(0, ROPE_SUB, 2, dtype=F32) / ROPE_SUB)
    ang_r = row[:, None] * inv_freq[None, :]
    ang_c = col[:, None] * inv_freq[None, :]
    cos = jnp.concatenate([jnp.cos(ang_r), jnp.cos(ang_r), jnp.cos(ang_c), jnp.cos(ang_c)], axis=1)
    sin = jnp.concatenate([-jnp.sin(ang_r), jnp.sin(ang_r), -jnp.sin(ang_c), jnp.sin(ang_c)], axis=1)
    return cos, sin


def _dn_params(dn_a_log, dn_dt_bias):
    rows = jnp.stack([dn_a_log.reshape(-1), dn_dt_bias.reshape(-1)], axis=0)
    return jnp.pad(rows, ((0, 0), (SM_A, LANES - SM_A - 2 * DN_HEADS)))


def _gla_params(gla_up, gla_up_b):
    ups = []
    for d in range(2):
        lo = SM_LR + d * GLA_RANK
        ups.append(jnp.pad(gla_up[d], ((lo, LANES - lo - GLA_RANK), (0, 0))))
    return jnp.stack(ups, axis=0), gla_up_b[:, None, :]


def _mixer(x16, bsz, s, w_in, dn_conv, dn_a_log, dn_dt_bias, dn_norm_g, att_qn_g, att_kn_g,
           gla_up, gla_up_b, gla_norm_g, rope):
    t = bsz * s
    bm = _tile(t, 1024, 16)
    one_group = _block_table(jnp.zeros((t // bm,), jnp.int32))
    w_main, w_small = _relayout_w_in(w_in)
    proj = grouped_matmul(x16, w_main, 0, one_group, bm, 1024, BF16).reshape(bsz, s, PROJ_COLS)
    small = grouped_matmul(x16, w_small, 0, one_group, bm, LANES, F32).reshape(bsz, s, LANES)

    q, k, v = dn_prep(proj, dn_conv)
    u, wq, ktt, qk, egl = dn_wy(q, k, v, small, _dn_params(dn_a_log, dn_dt_bias))
    dn_f, dn_b = dn_rec(u, wq, ktt, qk, egl)

    cos, sin_signed = rope
    aq, ak, av, kn = att_prep(proj, cos, sin_signed, att_qn_g, att_kn_g)
    o_att = flash_attention(aq, ak, av, kn)

    gl_f, gl_b = gla_scan(proj, small, *_gla_params(gla_up, gla_up_b))

    return merge_heads(dn_f, dn_b, proj, dn_norm_g, o_att, gl_f, gl_b, gla_norm_g).reshape(t, -1)


def _moe(x32, x16, router_w, w_gate, w_up, w_down, layer, ln_g, ln_b, alpha):
    t, d = x32.shape
    rw = jnp.pad(router_w, ((0, 0), (0, LANES - N_EXPERTS)))
    logits = router_logits(x32, rw)[:, :N_EXPERTS]
    top_val, top_idx = lax.top_k(logits, TOP_K)
    gates = jax.nn.softmax(top_val, axis=-1)
    e_flat = top_idx.reshape(-1).astype(jnp.int32)
    tok_flat = jnp.repeat(jnp.arange(t, dtype=jnp.int32), TOP_K)
    onehot = (e_flat[:, None] == jnp.arange(N_EXPERTS, dtype=jnp.int32)[None, :]).astype(jnp.int32)
    csum = jnp.cumsum(onehot, axis=0)
    counts = csum[-1]
    rank = jnp.take_along_axis(csum, e_flat[:, None], axis=1)[:, 0] - 1
    padded = (counts + MOE_BLOCK - 1) // MOE_BLOCK * MOE_BLOCK
    pstart = jnp.cumsum(padded) - padded
    pend = pstart + padded
    dest = pstart[e_flat] + rank
    nb = -(-(TOP_K * t) // MOE_BLOCK) + N_EXPERTS
    cap = nb * MOE_BLOCK
    buf_tok = jnp.zeros((cap,), jnp.int32).at[dest].set(tok_flat)
    block_start = jnp.arange(nb, dtype=jnp.int32) * MOE_BLOCK
    block_exp = jnp.minimum(jnp.sum(block_start[:, None] >= pend[None, :], axis=-1), N_EXPERTS - 1)
    table = _block_table(block_exp, block_start < pend[-1])
    xb = x16[buf_tok]
    h = grouped_swiglu(xb, w_gate, w_up, layer, table, MOE_BLOCK, 1792, slots=1)
    yb = grouped_matmul(h, w_down, layer, table, MOE_BLOCK, 1024, BF16, slots=1)
    dest_by_choice = dest.reshape(t, TOP_K).T.reshape(-1)
    return combine_ln(x32, yb[dest_by_choice], gates, ln_g, ln_b, alpha)


def _dense_ffn(x32, x16, w_gate, w_up, w_down, layer, ln_g, ln_b, alpha):
    t = x32.shape[0]
    bm = _tile(t, 1024, 16)
    h = grouped_swiglu(x16, w_gate, w_up, layer, _block_table(jnp.zeros((t // bm,), jnp.int32)), bm, 1024, slots=1)
    return matmul_res_ln(h, w_down[layer].astype(BF16), x32, ln_g, ln_b, alpha)


def kernel(x, w_in, dn_conv, dn_a_log, dn_dt_bias, dn_norm_g, att_qn_g, att_kn_g, gla_up, gla_up_b,
           gla_norm_g, w_out, ln1_g, ln1_b, ln2_g, ln2_b, ffn_w_gate, ffn_w_up, ffn_w_down, router_w,
           exp_w_gate, exp_w_up, exp_w_down):
    bsz, s, d = x.shape
    depth = w_in.shape[0]
    alpha = (2.0 * depth) ** 0.25
    t = bsz * s
    rope = _rope_tables(s)
    x32 = x.reshape(t, d)
    x16 = x32.astype(BF16)
    for layer in range(depth):
        mix = _mixer(x16, bsz, s, w_in[layer], dn_conv[layer], dn_a_log[layer], dn_dt_bias[layer],
                     dn_norm_g[layer], att_qn_g[layer], att_kn_g[layer], gla_up[layer],
                     gla_up_b[layer], gla_norm_g[layer], rope)
        x32, x16 = matmul_res_ln(mix, w_out[layer].astype(BF16), x32, ln1_g[layer], ln1_b[layer], alpha)
        j = layer // 2
        if layer % 2 == 0:
            x32, x16 = _dense_ffn(x32, x16, ffn_w_gate, ffn_w_up, ffn_w_down, j,
                                  ln2_g[layer], ln2_b[layer], alpha)
        else:
            x32, x16 = _moe(x32, x16, router_w[j], exp_w_gate, exp_w_up, exp_w_down, j,
                            ln2_g[layer], ln2_b[layer], alpha)
    return x32.reshape(bsz, s, d)
```
